```python
import jax, jax.numpy as jnp
from jax import lax
import numpy as np

D_MODEL = 1024
BATCH = 2
SEQ = 8192
DEPTH = 4
DEC_BATCH = 128
DEC_SEQ = 4
PAST_LEN = 8192
PAGE_SIZE = 128

N_MIXERS = 3
N_A = (DEPTH + 2) // 3
N_B = (DEPTH + 1) // 3
N_C = DEPTH // 3

NH_M = 4
DK_M = D_MODEL // 8
DV_M = D_MODEL // NH_M
M_CHUNK = 128
M_IN = 2 * NH_M * DK_M + 2 * NH_M * DV_M + 2 * NH_M

NH_B = 8
KVH_B = 4
G_B = NH_B // KVH_B
HD_B = D_MODEL // NH_B
MOBA_BLOCK = 256
MOBA_TOPK = 3
MOBA_QCHUNK = 64
B_IN = (NH_B + 2 * KVH_B) * HD_B

NH_C = 16
KVH_C = 2
G_C = NH_C // KVH_C
HD_C = D_MODEL // NH_C
WINDOW = 128
C_IN = (NH_C + 2 * KVH_C) * HD_C

N_GROUPS = 4
N_EXP = 8
N_EXPERTS = N_GROUPS * N_EXP
MOE_TOPK = 2
D_FF = D_MODEL // 4

ALPHA = (2.0 * DEPTH) ** 0.25
BETA = (8.0 * DEPTH) ** -0.25
EPS = 1e-5
F32 = jnp.float32

kernel_name = 'hybrid_mlstm_moba_swa_hmoe_step'


def layer_norm(x, g, b):
    xf = x.astype(F32)
    mu = jnp.mean(xf, axis=-1, keepdims=True)
    var = jnp.mean(jnp.square(xf - mu), axis=-1, keepdims=True)
    return ((xf - mu) * lax.rsqrt(var + EPS) * g + b).astype(x.dtype)


def alibi_slopes(n_heads):
    return jnp.exp2(-8.0 * jnp.arange(1, n_heads + 1, dtype=F32) / n_heads)


def _mlstm_chunk(carry, inp):
    C, n, m = carry
    q, k, v, li, lf = inp
    L = q.shape[1]
    b = jnp.cumsum(lf, axis=1)
    g = b + m[:, None, :]
    causal = jnp.tril(jnp.ones((L, L), dtype=bool))[None, :, :, None]
    dm = jnp.where(causal, b[:, :, None, :] - b[:, None, :, :] + li[:, None, :, :], -jnp.inf)
    mq = jnp.maximum(g, jnp.max(dm, axis=2))
    w_intra = jnp.exp(dm - mq[:, :, None, :])
    w_inter = jnp.exp(g - mq)
    s = jnp.einsum('bqhd,bshd->bqsh', q, k) * w_intra
    num = jnp.einsum('bqsh,bshv->bqhv', s, v) + w_inter[..., None] * jnp.einsum('bqhd,bhdv->bqhv', q, C)
    den = jnp.sum(s, axis=2) + w_inter * jnp.einsum('bqhd,bhd->bqh', q, n)
    h = num / jnp.maximum(jnp.abs(den), jnp.exp(-mq))[..., None]
    bl = b[:, -1, :]
    lw = bl[:, None, :] - b + li
    m_new = jnp.maximum(bl + m, jnp.max(lw, axis=1))
    wl = jnp.exp(lw - m_new[:, None, :])
    dec = jnp.exp(bl + m - m_new)
    C_new = dec[..., None, None] * C + jnp.einsum('blh,blhd,blhv->bhdv', wl, k, v)
    n_new = dec[..., None] * n + jnp.einsum('blh,blhd->bhd', wl, k)
    return (C_new, n_new, m_new), h


def mlstm_scan(q, k, v, li, lf, C0, n0, m0, chunk):
    B, T = q.shape[:2]
    nc = T // chunk

    def split(a):
        return jnp.moveaxis(a.reshape((B, nc, chunk) + a.shape[2:]), 1, 0)

    (C, n, m), h = lax.scan(_mlstm_chunk, (C0, n0, m0), (split(q), split(k), split(v), split(li), split(lf)))
    h = jnp.moveaxis(h, 0, 1).reshape((B, T) + h.shape[3:])
    return h, C, n, m


def mlstm_mixer(x, w_in, b_gate, norm_g, w_out, C0, n0, m0, chunk):
    B, T, _ = x.shape
    qk, vw = NH_M * DK_M, NH_M * DV_M
    p = x @ w_in
    q = p[..., :qk].reshape(B, T, NH_M, DK_M).astype(F32) * (DK_M ** -0.5)
    k = p[..., qk:2 * qk].reshape(B, T, NH_M, DK_M).astype(F32)
    v = p[..., 2 * qk:2 * qk + vw].reshape(B, T, NH_M, DV_M).astype(F32)
    o = p[..., 2 * qk + vw:2 * qk + 2 * vw].astype(F32)
    gates = p[..., 2 * qk + 2 * vw:].astype(F32) + b_gate.astype(F32)
    li = gates[..., :NH_M]
    lf = jax.nn.log_sigmoid(gates[..., NH_M:])
    h, C, n, m = mlstm_scan(q, k, v, li, lf, C0.astype(F32), n0.astype(F32), m0.astype(F32), chunk)
    mu = jnp.mean(h, axis=-1, keepdims=True)
    var = jnp.mean(jnp.square(h - mu), axis=-1, keepdims=True)
    hn = ((h - mu) * lax.rsqrt(var + EPS)).reshape(B, T, vw) * norm_g.astype(F32)
    y = (jax.nn.sigmoid(o) * hn).astype(x.dtype) @ w_out
    return y, C, n, m


def qkv_project(x, w_in, n_heads, n_kv, hd):
    B, T, _ = x.shape
    p = x @ w_in
    q = p[..., :n_heads * hd].reshape(B, T, n_heads, hd) * (hd ** -0.5)
    k = p[..., n_heads * hd:(n_heads + n_kv) * hd].reshape(B, T, n_kv, hd)
    v = p[..., (n_heads + n_kv) * hd:].reshape(B, T, n_kv, hd)
    return q, k, v


def moba_attend(q, qpos, k_sel, v_sel, pos_sel, ok_sel, k_own, v_own, pos_own):
    B, Q = q.shape[:2]
    slopes = alibi_slopes(NH_B)[:, None]
    qg = q.reshape(B, Q, KVH_B, G_B, HD_B)
    s_own = jnp.einsum('bqkgd,bksd->bqkgs', qg, k_own, preferred_element_type=F32).reshape(B, Q, NH_B, -1)
    d_own = (qpos[:, None] - pos_own[None, :]).astype(F32)
    s_own = jnp.where((d_own >= 0)[:, None, :], s_own - slopes * d_own[:, None, :], -jnp.inf)
    if k_sel is None:
        p_own = jax.nn.softmax(s_own, axis=-1)
        o_sel = None
    else:
        s_sel = jnp.einsum('bqhd,bqhsd->bqhs', q, k_sel, preferred_element_type=F32)
        d_sel = (qpos[None, :, None, None] - pos_sel).astype(F32)
        s_sel = jnp.where(ok_sel, s_sel - slopes * d_sel, -jnp.inf)
        p = jax.nn.softmax(jnp.concatenate([s_sel, s_own], axis=-1), axis=-1)
        n_sel = s_sel.shape[-1]
        p_own = p[..., n_sel:]
        o_sel = jnp.einsum('bqhs,bqhsd->bqhd', p[..., :n_sel], v_sel, preferred_element_type=F32)
    o = jnp.einsum('bqkgs,bksd->bqkgd', p_own.reshape(B, Q, KVH_B, G_B, -1), v_own,
                   preferred_element_type=F32).reshape(B, Q, NH_B, HD_B)
    return o if o_sel is None else o + o_sel


def moba_prompt(q, k, v):
    B, T = q.shape[:2]
    nb = -(-T // MOBA_BLOCK)
    pad = nb * MOBA_BLOCK - T

    def blocks(a):
        a = jnp.pad(a, ((0, 0), (0, pad), (0, 0), (0, 0)))
        return a.reshape(B, nb, MOBA_BLOCK, KVH_B, HD_B).transpose(0, 3, 1, 2, 4)

    kbt, vbt = blocks(k), blocks(v)
    hk = jnp.arange(NH_B) // G_B
    kmean_h = jnp.mean(kbt, axis=3, dtype=F32)[:, hk]
    topk = min(MOBA_TOPK, nb - 1)
    bi = jnp.arange(B)
    nq = T // MOBA_QCHUNK
    blk_ar = jnp.arange(MOBA_BLOCK)

    def one_chunk(args):
        ci, qi = args
        start = ci * MOBA_QCHUNK
        qpos = start + jnp.arange(MOBA_QCHUNK)
        own = start // MOBA_BLOCK
        k_own = lax.dynamic_index_in_dim(kbt, own, axis=2, keepdims=False)
        v_own = lax.dynamic_index_in_dim(vbt, own, axis=2, keepdims=False)
        pos_own = own * MOBA_BLOCK + blk_ar
        if topk == 0:
            return moba_attend(qi, qpos, None, None, None, None, k_own, v_own, pos_own)
        gate = jnp.einsum('bqhd,bhnd->bqhn', qi, kmean_h, preferred_element_type=F32)
        gate = jnp.where(jnp.arange(nb) < own, gate, -jnp.inf)
        _, idx = lax.top_k(gate, topk)
        ix = (bi[:, None, None, None], hk[None, None, :, None], idx)
        k_sel = kbt[ix].reshape(B, MOBA_QCHUNK, NH_B, topk * MOBA_BLOCK, HD_B)
        v_sel = vbt[ix].reshape(B, MOBA_QCHUNK, NH_B, topk * MOBA_BLOCK, HD_B)
        pos_sel = (idx[..., None] * MOBA_BLOCK + blk_ar).reshape(B, MOBA_QCHUNK, NH_B, topk * MOBA_BLOCK)
        ok_sel = jnp.repeat(jnp.arange(topk) < own, MOBA_BLOCK)
        return moba_attend(qi, qpos, k_sel, v_sel, pos_sel, ok_sel, k_own, v_own, pos_own)

    qc = jnp.moveaxis(q.reshape(B, nq, MOBA_QCHUNK, NH_B, HD_B), 1, 0)
    o = lax.map(one_chunk, (jnp.arange(nq), qc))
    return jnp.moveaxis(o, 0, 1).reshape(B, T, NH_B * HD_B)


def moba_sample(q, k_new, v_new, cache_k, cache_v, slot, page_table):
    DB, S = q.shape[:2]
    n_pages = page_table.shape[1]
    past = n_pages * PAGE_SIZE
    ppb = MOBA_BLOCK // PAGE_SIZE
    n_full = past // MOBA_BLOCK
    own_rows = (n_pages - n_full * ppb) * PAGE_SIZE
    hk = jnp.arange(NH_B) // G_B
    bi = jnp.arange(DB)
    blk_ar = jnp.arange(MOBA_BLOCK)

    def own(cache, new):
        rows = cache[slot, page_table[:, n_full * ppb:]].reshape(DB, own_rows, KVH_B, HD_B)
        return jnp.concatenate([rows.astype(new.dtype), new], axis=1).transpose(0, 2, 1, 3)

    k_own, v_own = own(cache_k, k_new), own(cache_v, v_new)
    pos_own = n_full * MOBA_BLOCK + jnp.arange(own_rows + S)
    qpos = past + jnp.arange(S)
    topk = min(MOBA_TOPK, n_full)
    if topk == 0:
        o = moba_attend(q, qpos, None, None, None, None, k_own, v_own, pos_own)
        return o.reshape(DB, S, NH_B * HD_B)
    k_past = cache_k[slot, page_table[:, :n_full * ppb]].reshape(DB, n_full, MOBA_BLOCK, KVH_B, HD_B)
    kmean = jnp.mean(k_past, axis=2, dtype=F32)
    gate = jnp.einsum('bqhd,bnhd->bqhn', q, kmean[:, :, hk], preferred_element_type=F32)
    _, idx = lax.top_k(gate, topk)

    def one_token(args):
        j, q_j, idx_j = args
        pg = page_table[bi[:, None, None, None], idx_j[..., None] * ppb + jnp.arange(ppb)]
        k_sel = cache_k[slot, pg, :, hk[None, :, None, None], :].reshape(DB, 1, NH_B, topk * MOBA_BLOCK, HD_B)
        v_sel = cache_v[slot, pg, :, hk[None, :, None, None], :].reshape(DB, 1, NH_B, topk * MOBA_BLOCK, HD_B)
        pos_sel = (idx_j[..., None] * MOBA_BLOCK + blk_ar).reshape(DB, 1, NH_B, topk * MOBA_BLOCK)
        o = moba_attend(q_j[:, None], (past + j)[None], k_sel, v_sel, pos_sel, True, k_own, v_own, pos_own)
        return o[:, 0]

    o = lax.map(one_token, (jnp.arange(S), jnp.moveaxis(q, 1, 0), jnp.moveaxis(idx, 1, 0)))
    return jnp.moveaxis(o, 0, 1).reshape(DB, S, NH_B * HD_B)


def swa_attend(q, qpos, k, v, kpos, sinks):
    B, N, Q = q.shape[:3]
    K = k.shape[2]
    slopes = alibi_slopes(NH_C)[:, None]
    qg = q.reshape(B, N, Q, KVH_C, G_C, HD_C)
    s = jnp.einsum('bnqkgd,bnskd->bnqkgs', qg, k, preferred_element_type=F32).reshape(B, N, Q, NH_C, K)
    d = qpos[:, :, None] - kpos[:, None, :]
    ok = (d >= 0) & (d <= WINDOW) & (kpos[:, None, :] >= 0)
    s = jnp.where(ok[:, :, None, :], s - slopes * d[:, :, None, :].astype(F32), -jnp.inf)
    sink = jnp.broadcast_to(sinks.astype(F32)[:, None], (B, N, Q, NH_C, 1))
    p = jax.nn.softmax(jnp.concatenate([s, sink], axis=-1), axis=-1)[..., :K]
    o = jnp.einsum('bnqkgs,bnskd->bnqkgd', p.reshape(B, N, Q, KVH_C, G_C, K), v, preferred_element_type=F32)
    return o.reshape(B, N, Q, NH_C * HD_C)


def swa_prompt(q, k, v, sinks):
    B, T = q.shape[:2]
    nblk = T // WINDOW
    qb = q.reshape(B, nblk, WINDOW, NH_C, HD_C)

    def band(a):
        ab = a.reshape(B, nblk, WINDOW, KVH_C, HD_C)
        prev = jnp.pad(ab, ((0, 0), (1, 0), (0, 0), (0, 0), (0, 0)))[:, :-1]
        return jnp.concatenate([prev, ab], axis=2)

    pos = jnp.arange(T).reshape(nblk, WINDOW)
    kpos = jnp.concatenate([pos - WINDOW, pos], axis=1)
    o = swa_attend(qb, pos, band(k), band(v), kpos, sinks)
    return o.reshape(B, T, NH_C * HD_C)


def swa_sample(q, k_new, v_new, cache_k, cache_v, sinks, past):
    W = cache_k.shape[1]
    S = q.shape[1]
    k_all = jnp.concatenate([cache_k.astype(k_new.dtype), k_new], axis=1)
    v_all = jnp.concatenate([cache_v.astype(v_new.dtype), v_new], axis=1)
    kpos = past - W + jnp.arange(W + S)
    qpos = past + jnp.arange(S)
    o = swa_attend(q[:, None], qpos[None], k_all[:, None], v_all[:, None], kpos[None], sinks)[:, 0]
    return o, k_all[:, -W:], v_all[:, -W:]


def hier_moe(x, w_group, b_group, w_router, b_router, w_gate, w_up, w_down):
    shp = x.shape
    xt = x.reshape(-1, D_MODEL)
    T = xt.shape[0]
    lg = jnp.einsum('td,dg->tg', xt, w_group, preferred_element_type=F32) + b_group
    pg = jax.nn.softmax(lg, axis=-1)
    pg1, g1 = lax.top_k(pg, 1)
    le = (jnp.einsum('td,de->te', xt, w_router, preferred_element_type=F32) + b_router).reshape(T, N_GROUPS, N_EXP)
    le_g = jnp.take_along_axis(le, g1[:, :, None], axis=1)[:, 0]
    pe = jax.nn.softmax(le_g, axis=-1)
    w2, e2 = lax.top_k(pe, MOE_TOPK)
    w2 = w2 / jnp.sum(w2, axis=-1, keepdims=True) * pg1
    combine = jnp.einsum('tk,tke->te', w2, jax.nn.one_hot(g1 * N_EXP + e2, N_EXPERTS, dtype=F32))
    hg = jnp.einsum('td,efd->tef', xt, jnp.swapaxes(w_gate, 1, 2))
    hu = jnp.einsum('td,edf->tef', xt, w_up)
    h = jax.nn.silu(hg) * hu * combine[..., None].astype(x.dtype)
    y = jnp.einsum('tef,efd->td', h, w_down)
    return y.reshape(shp)


def setup_inputs(seed: int = 0) -> dict:
    key = jax.random.key(seed)
    ks = jax.random.split(key, 32)

    def nrm(i, shape, scale):
        return jax.random.normal(ks[i], shape, F32) * scale

    n_pages = PAST_LEN // PAGE_SIZE
    n_used = DEC_BATCH * n_pages
    n_pool = n_used + max(1, n_used // 4)
    perm = jax.random.permutation(ks[0], n_pool).astype(jnp.int32)
    page_table = perm[:n_used].reshape(DEC_BATCH, n_pages)
    win = min(WINDOW, PAST_LEN)
    f_bias = jnp.linspace(3.0, 6.0, NH_M, dtype=F32)
    b_gate = jnp.concatenate([nrm(11, (N_A, NH_M), 0.1), f_bias + nrm(12, (N_A, NH_M), 0.1)], axis=-1)
    return {
        'x_prompt': nrm(1, (BATCH, SEQ, D_MODEL), 1.0),
        'x_sample': nrm(2, (DEC_BATCH, DEC_SEQ, D_MODEL), 1.0),
        'state_mlstm_C': nrm(3, (N_A, DEC_BATCH, NH_M, DK_M, DV_M), 1.0),
        'state_mlstm_n': nrm(4, (N_A, DEC_BATCH, NH_M, DK_M), 1.0),
        'state_mlstm_m': nrm(5, (N_A, DEC_BATCH, NH_M), 1.0),
        'cache_moba_k': nrm(6, (N_B, n_pool, PAGE_SIZE, KVH_B, HD_B), 1.0),
        'cache_moba_v': nrm(7, (N_B, n_pool, PAGE_SIZE, KVH_B, HD_B), 1.0),
        'cache_swa_k': nrm(8, (N_C, DEC_BATCH, win, KVH_C, HD_C), 1.0),
        'cache_swa_v': nrm(9, (N_C, DEC_BATCH, win, KVH_C, HD_C), 1.0),
        'page_table': page_table,
        'mlstm_w_in': nrm(10, (N_A, D_MODEL, M_IN), D_MODEL ** -0.5),
        'mlstm_b_gate': b_gate,
        'mlstm_norm_g': 1.0 + nrm(13, (N_A, NH_M * DV_M), 0.02),
        'mlstm_w_out': nrm(14, (N_A, NH_M * DV_M, D_MODEL), (NH_M * DV_M) ** -0.5 * BETA),
        'moba_w_in': nrm(15, (N_B, D_MODEL, B_IN), D_MODEL ** -0.5),
        'moba_w_out': nrm(16, (N_B, NH_B * HD_B, D_MODEL), (NH_B * HD_B) ** -0.5 * BETA),
        'swa_w_in': nrm(17, (N_C, D_MODEL, C_IN), D_MODEL ** -0.5),
        'swa_sinks': nrm(18, (N_C, NH_C), 0.5),
        'swa_w_out': nrm(19, (N_C, NH_C * HD_C, D_MODEL), (NH_C * HD_C) ** -0.5 * BETA),
        'ln_mix_g': 1.0 + nrm(20, (DEPTH, D_MODEL), 0.02),
        'ln_mix_b': nrm(21, (DEPTH, D_MODEL), 0.02),
        'ln_ffn_g': 1.0 + nrm(22, (DEPTH, D_MODEL), 0.02),
        'ln_ffn_b': nrm(23, (DEPTH, D_MODEL), 0.02),
        'moe_w_group': nrm(24, (DEPTH, D_MODEL, N_GROUPS), D_MODEL ** -0.5),
        'moe_b_group': nrm(25, (DEPTH, N_GROUPS), 0.01),
        'moe_w_router': nrm(26, (DEPTH, D_MODEL, N_EXPERTS), D_MODEL ** -0.5),
        'moe_b_router': nrm(27, (DEPTH, N_EXPERTS), 0.01),
        'moe_w_gate': nrm(28, (DEPTH, N_EXPERTS, D_MODEL, D_FF), D_MODEL ** -0.5),
        'moe_w_up': nrm(29, (DEPTH, N_EXPERTS, D_MODEL, D_FF), D_MODEL ** -0.5),
        'moe_w_down': nrm(30, (DEPTH, N_EXPERTS, D_FF, D_MODEL), D_FF ** -0.5 * BETA),
    }


def reference(x_prompt, x_sample, state_mlstm_C, state_mlstm_n, state_mlstm_m, cache_moba_k, cache_moba_v,
              cache_swa_k, cache_swa_v, page_table, mlstm_w_in, mlstm_b_gate, mlstm_norm_g, mlstm_w_out,
              moba_w_in, moba_w_out, swa_w_in, swa_sinks, swa_w_out, ln_mix_g, ln_mix_b, ln_ffn_g, ln_ffn_b,
              moe_w_group, moe_b_group, moe_w_router, moe_b_router, moe_w_gate, moe_w_up, moe_w_down):
    B, T, _ = x_prompt.shape
    DB, S, _ = x_sample.shape
    past = page_table.shape[1] * PAGE_SIZE
    xp, xs = x_prompt, x_sample
    mc_p, mn_p, mm_p, mc_s, mn_s, mm_s = [], [], [], [], [], []
    bk_p, bv_p, bk_s, bv_s = [], [], [], []
    sk_p, sv_p, sk_s, sv_s = [], [], [], []
    for layer in range(DEPTH):
        kind = layer % N_MIXERS
        slot = layer // N_MIXERS
        if kind == 0:
            zC = jnp.zeros((B, NH_M, DK_M, DV_M), F32)
            zn = jnp.zeros((B, NH_M, DK_M), F32)
            zm = jnp.zeros((B, NH_M), F32)
            yp, c, n, m = mlstm_mixer(xp, mlstm_w_in[slot], mlstm_b_gate[slot], mlstm_norm_g[slot],
                                      mlstm_w_out[slot], zC, zn, zm, M_CHUNK)
            mc_p.append(c); mn_p.append(n); mm_p.append(m)
            ys, c, n, m = mlstm_mixer(xs, mlstm_w_in[slot], mlstm_b_gate[slot], mlstm_norm_g[slot],
                                      mlstm_w_out[slot], state_mlstm_C[slot], state_mlstm_n[slot],
                                      state_mlstm_m[slot], S)
            mc_s.append(c); mn_s.append(n); mm_s.append(m)
        elif kind == 1:
            q, k, v = qkv_project(xp, moba_w_in[slot], NH_B, KVH_B, HD_B)
            yp = moba_prompt(q, k, v).astype(xp.dtype) @ moba_w_out[slot]
            bk_p.append(k); bv_p.append(v)
            q, k, v = qkv_project(xs, moba_w_in[slot], NH_B, KVH_B, HD_B)
            ys = moba_sample(q, k, v, cache_moba_k, cache_moba_v, slot, page_table).astype(xs.dtype) @ moba_w_out[slot]
            bk_s.append(k); bv_s.append(v)
        else:
            q, k, v = qkv_project(xp, swa_w_in[slot], NH_C, KVH_C, HD_C)
            yp = swa_prompt(q, k, v, swa_sinks[slot]).astype(xp.dtype) @ swa_w_out[slot]
            wp = min(WINDOW, T)
            sk_p.append(k[:, T - wp:]); sv_p.append(v[:, T - wp:])
            q, k, v = qkv_project(xs, swa_w_in[slot], NH_C, KVH_C, HD_C)
            o, kw, vw = swa_sample(q, k, v, cache_swa_k[slot], cache_swa_v[slot], swa_sinks[slot], past)
            ys = o.astype(xs.dtype) @ swa_w_out[slot]
            sk_s.append(kw); sv_s.append(vw)
        xp = layer_norm(ALPHA * xp + yp, ln_mix_g[layer], ln_mix_b[layer])
        xs = layer_norm(ALPHA * xs + ys, ln_mix_g[layer], ln_mix_b[layer])
        moe_args = (moe_w_group[layer], moe_b_group[layer], moe_w_router[layer], moe_b_router[layer],
                    moe_w_gate[layer], moe_w_up[layer], moe_w_down[layer])
        xp = layer_norm(ALPHA * xp + hier_moe(xp, *moe_args), ln_ffn_g[layer], ln_ffn_b[layer])
        xs = layer_norm(ALPHA * xs + hier_moe(xs, *moe_args), ln_ffn_g[layer], ln_ffn_b[layer])
    return (xp, xs,
            jnp.stack(mc_p), jnp.stack(mn_p), jnp.stack(mm_p),
            jnp.stack(mc_s), jnp.stack(mn_s), jnp.stack(mm_s),
            jnp.stack(bk_p), jnp.stack(bv_p), jnp.stack(bk_s), jnp.stack(bv_s),
            jnp.stack(sk_p), jnp.stack(sv_p), jnp.stack(sk_s), jnp.stack(sv_s))
```

```python
import functools

import jax
import jax.numpy as jnp
from jax import lax
from jax.experimental import pallas as pl
from jax.experimental.pallas import tpu as pltpu

F32 = jnp.float32
BF16 = jnp.bfloat16
HIGHEST = lax.Precision.HIGHEST

D_MODEL = 1024
BATCH = 2
SEQ = 8192
DEPTH = 4
DEC_BATCH = 128
DEC_SEQ = 4
PAGE_SIZE = 128
N_PAGES = 64
PAST = N_PAGES * PAGE_SIZE
NP = BATCH * SEQ
NS = DEC_BATCH * DEC_SEQ
NT = NP + NS

NH_M = 4
DK_M = 128
DV_M = 256
M_CHUNK = 128
M_QK = NH_M * DK_M
M_V = NH_M * DV_M
M_IN = 2 * M_QK + 2 * M_V + 2 * NH_M
M_IN_PAD = 3200
M_GATE_COL = 2 * M_QK + 2 * M_V

NH_B = 8
KVH_B = 4
G_B = 2
HD_B = 128
MOBA_BLOCK = 256
MOBA_TOPK = 3
NB_B = SEQ // MOBA_BLOCK
NB_S = PAST // MOBA_BLOCK
B_IN = (NH_B + 2 * KVH_B) * HD_B
SAMPLE_PAGES_PER_STEP = 16

NH_C = 16
KVH_C = 2
G_C = 8
HD_C = 64
WINDOW = 128
C_IN = (NH_C + 2 * KVH_C) * HD_C
SWA_SEQ_PER_STEP = 8

N_GROUPS = 4
N_EXP = 8
N_EXPERTS = 32
D_FF = 256
ROUTER_LANES = 128

ALPHA = (2.0 * DEPTH) ** 0.25
EPS = 1e-5
NEG_INF = float("-inf")
VMEM_LIMIT = 56 * 1024 * 1024


def _params(*sem):
    return pltpu.CompilerParams(dimension_semantics=sem, vmem_limit_bytes=VMEM_LIMIT)


def _dot(a, b):
    return jnp.dot(a, b, preferred_element_type=F32)


def _dot_nt(a, b, precision=None):
    return lax.dot_general(a, b, (((1,), (1,)), ((), ())), precision=precision, preferred_element_type=F32)


def _dot_tn(a, b):
    return lax.dot_general(a, b, (((0,), (0,)), ((), ())), preferred_element_type=F32)


def _layer_norm(z, g, b):
    mu = jnp.mean(z, axis=-1, keepdims=True)
    zc = z - mu
    var = jnp.mean(zc * zc, axis=-1, keepdims=True)
    return zc * lax.rsqrt(var + EPS) * g + b


def _mm_kernel(x_ref, w_ref, o_ref):
    o_ref[...] = _dot(x_ref[...].astype(BF16), w_ref[...])


def matmul(x, w_bf16, tm=512):
    m, k = x.shape
    n = w_bf16.shape[1]
    return pl.pallas_call(
        _mm_kernel,
        grid=(m // tm,),
        in_specs=[pl.BlockSpec((tm, k), lambda i: (i, 0)), pl.BlockSpec((k, n), lambda i: (0, 0))],
        out_specs=pl.BlockSpec((tm, n), lambda i: (i, 0)),
        out_shape=jax.ShapeDtypeStruct((m, n), F32),
        compiler_params=_params("arbitrary"),
        name="in_proj",
    )(x, w_bf16)


def _outproj_ln_kernel(a_ref, w_ref, r_ref, g_ref, b_ref, o_ref):
    y = _dot(a_ref[...].astype(BF16), w_ref[...])
    o_ref[...] = _layer_norm(ALPHA * r_ref[...] + y, g_ref[...], b_ref[...])


def outproj_ln(a, w_bf16, resid, g, b, tm=512):
    m, k = a.shape
    n = w_bf16.shape[1]
    row = lambda i: (i, 0)
    fixed = lambda i: (0, 0)
    return pl.pallas_call(
        _outproj_ln_kernel,
        grid=(m // tm,),
        in_specs=[pl.BlockSpec((tm, k), row), pl.BlockSpec((k, n), fixed), pl.BlockSpec((tm, n), row),
                  pl.BlockSpec((1, n), fixed), pl.BlockSpec((1, n), fixed)],
        out_specs=pl.BlockSpec((tm, n), row),
        out_shape=jax.ShapeDtypeStruct((m, n), F32),
        compiler_params=_params("arbitrary"),
        name="out_proj_ln",
    )(a, w_bf16, resid, g.reshape(1, n), b.reshape(1, n))


def _log_sigmoid(x):
    return jnp.minimum(x, 0.0) - jnp.log1p(jnp.exp(-jnp.abs(x)))


def _mlstm_kernel(p_ref, c0_ref, n0_ref, m0_ref, bg_ref, ng_ref,
                  hg_ref, co_ref, no_ref, mo_ref, c_s, n_s, m_s, *, L, L_real):
    c = pl.program_id(1)

    @pl.when(c == 0)
    def _():
        c_s[...] = c0_ref[...]
        n_s[...] = n0_ref[...]
        m_s[...] = m0_ref[...]

    gates = p_ref[:, M_GATE_COL:M_IN_PAD] + bg_ref[...]
    lane = lax.broadcasted_iota(jnp.int32, (L, 128), 1)
    x = jnp.where(lane < NH_M, gates, _log_sigmoid(gates))
    if L_real < L:
        tok = lax.broadcasted_iota(jnp.int32, (L, 128), 0)
        x = jnp.where(tok < L_real, x, jnp.where(lane < NH_M, -1e30, 0.0))
    r = lax.broadcasted_iota(jnp.int32, (L, L), 0)
    s = lax.broadcasted_iota(jnp.int32, (L, L), 1)
    causal = r >= s
    tri = causal.astype(F32)
    bcum = jnp.dot(tri, x, precision=HIGHEST, preferred_element_type=F32)
    eye8 = (lax.broadcasted_iota(jnp.int32, (8, 128), 0) == lax.broadcasted_iota(jnp.int32, (8, 128), 1)).astype(F32)
    x_rows = _dot_nt(eye8, x, HIGHEST)
    b_rows = _dot_nt(eye8, bcum, HIGHEST)

    for h in range(NH_M):
        li_row = x_rows[h:h + 1, :]
        li_col = x[:, h:h + 1]
        b_row = b_rows[NH_M + h:NH_M + h + 1, :]
        b_col = bcum[:, NH_M + h:NH_M + h + 1]
        m_old = m_s[:, h:h + 1]
        dm = jnp.where(causal, b_col - b_row + li_row, NEG_INF)
        g_col = b_col + m_old
        mq = jnp.maximum(g_col, jnp.max(dm, axis=1, keepdims=True))
        w_intra = jnp.exp(dm - mq)
        w_inter = jnp.exp(g_col - mq)
        q = p_ref[:, h * DK_M:(h + 1) * DK_M] * (DK_M ** -0.5)
        k = p_ref[:, M_QK + h * DK_M:M_QK + (h + 1) * DK_M]
        v = p_ref[:, 2 * M_QK + h * DV_M:2 * M_QK + (h + 1) * DV_M]
        qb = q.astype(BF16)
        kb = k.astype(BF16)
        cmat = c_s[h]
        nrow = n_s[h:h + 1, :]
        sc = _dot_nt(qb, kb) * w_intra
        num = _dot(sc.astype(BF16), v.astype(BF16)) + w_inter * _dot(qb, cmat.astype(BF16))
        den = jnp.sum(sc, axis=1, keepdims=True) + w_inter * jnp.sum(q * nrow, axis=1, keepdims=True)
        hh = num / jnp.maximum(jnp.abs(den), jnp.exp(-mq))
        bl = b_col[L - 1:L, :]
        lw = bl - b_col + li_col
        m_new = jnp.maximum(bl + m_old, jnp.max(lw, axis=0, keepdims=True))
        wl = jnp.exp(lw - m_new)
        dec = jnp.exp(bl + m_old - m_new)
        c_s[h] = dec * cmat + _dot_tn(kb, (wl * v).astype(BF16))
        n_s[h:h + 1, :] = dec * nrow + jnp.sum(wl * k, axis=0, keepdims=True)
        m_s[:, h:h + 1] = m_new
        mu = jnp.mean(hh, axis=1, keepdims=True)
        hc = hh - mu
        var = jnp.mean(hc * hc, axis=1, keepdims=True)
        hn = hc * lax.rsqrt(var + EPS) * ng_ref[:, h * DV_M:(h + 1) * DV_M]
        o = p_ref[:, 2 * M_QK + M_V + h * DV_M:2 * M_QK + M_V + (h + 1) * DV_M]
        hg_ref[:, h * DV_M:(h + 1) * DV_M] = hn / (1.0 + jnp.exp(-o))

    @pl.when(c == pl.num_programs(1) - 1)
    def _():
        co_ref[...] = c_s[...]
        no_ref[...] = n_s[...]
        mo_ref[...] = m_s[...]


def mlstm(p, c0, n0, m0, b_gate_row, norm_g_row, nb, nc, L, L_real):
    kern = functools.partial(_mlstm_kernel, L=L, L_real=L_real)
    per_b = lambda b, c: (b, 0, 0)
    return pl.pallas_call(
        kern,
        grid=(nb, nc),
        in_specs=[pl.BlockSpec((L, M_IN_PAD), lambda b, c: (b * nc + c, 0)),
                  pl.BlockSpec((None, NH_M, DK_M, DV_M), lambda b, c: (b, 0, 0, 0)),
                  pl.BlockSpec((None, NH_M, DK_M), per_b),
                  pl.BlockSpec((None, 1, NH_M), per_b),
                  pl.BlockSpec((1, 128), lambda b, c: (0, 0)),
                  pl.BlockSpec((1, M_V), lambda b, c: (0, 0))],
        out_specs=[pl.BlockSpec((L, M_V), lambda b, c: (b * nc + c, 0)),
                   pl.BlockSpec((None, NH_M, DK_M, DV_M), lambda b, c: (b, 0, 0, 0)),
                   pl.BlockSpec((None, NH_M, DK_M), per_b),
                   pl.BlockSpec((None, 1, NH_M), per_b)],
        out_shape=[jax.ShapeDtypeStruct((nb * nc * L, M_V), F32),
                   jax.ShapeDtypeStruct((nb, NH_M, DK_M, DV_M), F32),
                   jax.ShapeDtypeStruct((nb, NH_M, DK_M), F32),
                   jax.ShapeDtypeStruct((nb, 1, NH_M), F32)],
        scratch_shapes=[pltpu.VMEM((NH_M, DK_M, DV_M), F32), pltpu.VMEM((NH_M, DK_M), F32),
                        pltpu.VMEM((1, NH_M), F32)],
        compiler_params=_params("arbitrary", "arbitrary"),
        name=f"mlstm_L{L}",
    )(p, c0, n0, m0.reshape(nb, 1, NH_M), b_gate_row, norm_g_row)


def _topk_mask_lanes(g, k):
    lane = lax.broadcasted_iota(jnp.int32, g.shape, 1)
    n = g.shape[1]
    sel = jnp.zeros(g.shape, F32)
    for _ in range(k):
        mx = jnp.max(g, axis=1, keepdims=True)
        idx = jnp.min(jnp.where(g == mx, lane, n), axis=1, keepdims=True)
        hit = (lane == idx) & (mx > NEG_INF)
        sel = jnp.where(hit, 1.0, sel)
        g = jnp.where(lane == idx, NEG_INF, g)
    return sel


def _moba_select_kernel(q_ref, k_ref, sel_ref, kmean_s, *, tq):
    i = pl.program_id(2)

    @pl.when(i == 0)
    def _():
        kb = k_ref[...].reshape(NB_B, MOBA_BLOCK, HD_B)
        kmean_s[...] = jnp.sum(kb, axis=1) * (1.0 / MOBA_BLOCK)

    km = kmean_s[...]
    blk = lax.broadcasted_iota(jnp.int32, (tq, NB_B), 1)
    own = (i * tq + lax.broadcasted_iota(jnp.int32, (tq, NB_B), 0)) // MOBA_BLOCK
    for g in range(G_B):
        q = q_ref[:, g * HD_B:(g + 1) * HD_B] * (HD_B ** -0.5)
        gate = _dot_nt(q, km, HIGHEST)
        gate = jnp.where(blk < own, gate, NEG_INF)
        sel_ref[g] = _topk_mask_lanes(gate, MOBA_TOPK)


def moba_select(p, tq=1024):
    kern = functools.partial(_moba_select_kernel, tq=tq)
    nq = SEQ // tq
    return pl.pallas_call(
        kern,
        grid=(BATCH, KVH_B, nq),
        in_specs=[pl.BlockSpec((tq, G_B * HD_B), lambda b, c, i: (b * nq + i, c)),
                  pl.BlockSpec((SEQ, HD_B), lambda b, c, i: (b, NH_B + c))],
        out_specs=pl.BlockSpec((None, G_B, tq, NB_B), lambda b, c, i: (b, c, i, 0)),
        out_shape=jax.ShapeDtypeStruct((BATCH, NH_B, SEQ, NB_B), F32),
        scratch_shapes=[pltpu.VMEM((NB_B, HD_B), F32)],
        compiler_params=_params("arbitrary", "arbitrary", "arbitrary"),
        name="moba_select",
    )(p, p)


def _moba_flash_kernel(q_ref, k_ref, v_ref, sel_ref, slope_ref, o_ref):
    i = pl.program_id(2)
    bq = MOBA_BLOCK
    qb = (q_ref[...] * (HD_B ** -0.5)).astype(BF16)
    slope = slope_ref[:, 0:1]
    r = lax.broadcasted_iota(jnp.int32, (bq, bq), 0)
    cc = lax.broadcasted_iota(jnp.int32, (bq, bq), 1)
    rel = (r - cc).astype(F32)
    sel = sel_ref[...]
    blk_row = lax.broadcasted_iota(jnp.int32, (NB_B, bq), 0)

    def step(j, s_mask_fn, carry):
        m, l, acc = carry
        off = pl.multiple_of(j * bq, bq)
        kj = k_ref[pl.ds(off, bq), :].astype(BF16)
        vj = v_ref[pl.ds(off, bq), :].astype(BF16)
        d = rel + ((i - j) * bq).astype(F32)
        sc = _dot_nt(qb, kj) - slope * d
        sc = s_mask_fn(sc, j)
        m_new = jnp.maximum(m, jnp.max(sc, axis=1, keepdims=True))
        pr = jnp.exp(sc - m_new)
        a = jnp.exp(m - m_new)
        l = a * l + jnp.sum(pr, axis=1, keepdims=True)
        acc = a * acc + _dot(pr.astype(BF16), vj)
        return m_new, l, acc

    def past_mask(sc, j):
        chosen = _dot(sel, (blk_row == j).astype(F32))
        return jnp.where(chosen > 0.5, sc, NEG_INF)

    def own_mask(sc, j):
        return jnp.where(r >= cc, sc, NEG_INF)

    init = (jnp.full((bq, 1), -1e30, F32), jnp.zeros((bq, 1), F32), jnp.zeros((bq, HD_B), F32))
    carry = lax.fori_loop(0, i, lambda j, cr: step(j, past_mask, cr), init)
    m, l, acc = step(i, own_mask, carry)
    o_ref[...] = acc / l


def moba_flash(p, sel, slopes):
    nq = SEQ // MOBA_BLOCK
    return pl.pallas_call(
        _moba_flash_kernel,
        grid=(BATCH, NH_B, nq),
        in_specs=[pl.BlockSpec((MOBA_BLOCK, HD_B), lambda b, h, i: (b * nq + i, h)),
                  pl.BlockSpec((SEQ, HD_B), lambda b, h, i: (b, NH_B + h // G_B)),
                  pl.BlockSpec((SEQ, HD_B), lambda b, h, i: (b, NH_B + KVH_B + h // G_B)),
                  pl.BlockSpec((None, None, MOBA_BLOCK, NB_B), lambda b, h, i: (b, h, i, 0)),
                  pl.BlockSpec((None, 1, 128), lambda b, h, i: (h, 0, 0))],
        out_specs=pl.BlockSpec((MOBA_BLOCK, HD_B), lambda b, h, i: (b * nq + i, h)),
        out_shape=jax.ShapeDtypeStruct((NP, NH_B * HD_B), F32),
        compiler_params=_params("arbitrary", "arbitrary", "arbitrary"),
        name="moba_flash",
    )(p, p, p, sel, slopes)


def _moba_sample_kernel(pt_ref, qa_ref, kn_ref, vn_ref, slope_ref, qs_ref, ck_ref, cv_ref,
                        o_ref, buf, sem, s_s, p_s, pown_s, ksum_s, acc_s):
    npg = SAMPLE_PAGES_PER_STEP
    b = pl.program_id(0)
    t = pl.program_id(1)
    nb = pl.num_programs(0)
    g = b * 8 + t
    slot = g % 2

    def page_copy(src_ref, page, sl, pg):
        return pltpu.make_async_copy(src_ref.at[page], buf.at[sl, pg], sem.at[sl])

    def start_fetch(bn, tn, sl):
        qn = tn % 4

        @pl.when(tn < 4)
        def _():
            for pg in range(npg):
                page_copy(ck_ref, pt_ref[bn, qn * npg + pg], sl, pg).start()

        @pl.when(tn >= 4)
        def _():
            for pg in range(npg):
                page_copy(cv_ref, pt_ref[bn, qn * npg + pg], sl, pg).start()

    @pl.when(g == 0)
    def _():
        start_fetch(b, t, slot)

    @pl.when(g + 1 < nb * 8)
    def _():
        tn = (t + 1) % 8
        bn = b + (t + 1) // 8
        start_fetch(bn, tn, 1 - slot)

    for pg in range(npg):
        page_copy(ck_ref, 0, slot, pg).wait()

    qa = qa_ref[...]
    qab = qa.astype(BF16)
    slope = slope_ref[...]
    qs = qs_ref[...]

    @pl.when(t < 4)
    def _():
        for pg in range(npg):
            page = buf[slot, pg]
            row0 = pl.multiple_of((t * npg + pg) * PAGE_SIZE, PAGE_SIZE)
            s_s[pl.ds(row0, PAGE_SIZE), :] = _dot(page.astype(BF16), qab)
            csum = jnp.sum(page, axis=0, keepdims=True)
            if pg % 2 == 0:
                prev = csum
            else:
                ksum_s[pl.ds(t * (npg // 2) + pg // 2, 1), :] = prev + csum

    @pl.when(t == 3)
    def _():
        kmean = ksum_s[...] * (1.0 / MOBA_BLOCK)
        gate = jnp.dot(kmean, qa, precision=HIGHEST, preferred_element_type=F32)
        rowi = lax.broadcasted_iota(jnp.int32, gate.shape, 0)
        sel = jnp.zeros(gate.shape, F32)
        for _ in range(MOBA_TOPK):
            mx = jnp.max(gate, axis=0, keepdims=True)
            idx = jnp.min(jnp.where(gate == mx, rowi, NB_S), axis=0, keepdims=True)
            sel = jnp.where(rowi == idx, 1.0, sel)
            gate = jnp.where(rowi == idx, NEG_INF, gate)
        sc = s_s[...].reshape(NB_S, MOBA_BLOCK, 32)
        kpos = (lax.broadcasted_iota(jnp.int32, sc.shape, 0) * MOBA_BLOCK
                + lax.broadcasted_iota(jnp.int32, sc.shape, 1)).astype(F32)
        d = (float(PAST) + qs)[None] - kpos
        sc = jnp.where(sel[:, None, :] > 0.5, sc - slope[None] * d, NEG_INF)
        so = _dot(kn_ref[...].astype(BF16), qab)
        tok = lax.broadcasted_iota(jnp.int32, so.shape, 0).astype(F32)
        so = jnp.where((tok <= qs) & (tok < float(DEC_SEQ)), so - slope * (qs - tok), NEG_INF)
        mx = jnp.maximum(jnp.max(jnp.max(sc, axis=1), axis=0, keepdims=True), jnp.max(so, axis=0, keepdims=True))
        pr = jnp.exp(sc - mx[None])
        po = jnp.exp(so - mx)
        den = jnp.sum(jnp.sum(pr, axis=1), axis=0, keepdims=True) + jnp.sum(po, axis=0, keepdims=True)
        inv = 1.0 / den
        p_s[...] = (pr * inv[None]).reshape(PAST, 32)
        pown_s[...] = po * inv

    @pl.when(t == 4)
    def _():
        acc_s[...] = _dot_tn(vn_ref[...].astype(BF16), pown_s[...].astype(BF16))

    @pl.when(t >= 4)
    def _():
        acc = acc_s[...]
        for pg in range(npg):
            page = buf[slot, pg]
            row0 = pl.multiple_of(((t - 4) * npg + pg) * PAGE_SIZE, PAGE_SIZE)
            pt = p_s[pl.ds(row0, PAGE_SIZE), :]
            acc = acc + _dot_tn(page.astype(BF16), pt.astype(BF16))
        acc_s[...] = acc

    @pl.when(t == 7)
    def _():
        o_ref[...] = acc_s[...]


def moba_sample(page_table, q_aug, k_new, v_new, slope_col, qs_col, cache_k, cache_v):
    w = KVH_B * HD_B
    per_b3 = lambda b, t, pt: (b, 0, 0)
    fixed = lambda b, t, pt: (0, 0)
    grid_spec = pltpu.PrefetchScalarGridSpec(
        num_scalar_prefetch=1,
        grid=(DEC_BATCH, 8),
        in_specs=[pl.BlockSpec((None, w, 32), per_b3),
                  pl.BlockSpec((None, 8, w), per_b3),
                  pl.BlockSpec((None, 8, w), per_b3),
                  pl.BlockSpec((1, 32), fixed),
                  pl.BlockSpec((1, 32), fixed),
                  pl.BlockSpec(memory_space=pl.ANY),
                  pl.BlockSpec(memory_space=pl.ANY)],
        out_specs=pl.BlockSpec((None, w, 32), per_b3),
        scratch_shapes=[pltpu.VMEM((2, SAMPLE_PAGES_PER_STEP, PAGE_SIZE, w), F32),
                        pltpu.SemaphoreType.DMA((2,)),
                        pltpu.VMEM((PAST, 32), F32),
                        pltpu.VMEM((PAST, 32), F32),
                        pltpu.VMEM((8, 32), F32),
                        pltpu.VMEM((NB_S, w), F32),
                        pltpu.VMEM((w, 32), F32)],
    )
    return pl.pallas_call(
        _moba_sample_kernel,
        grid_spec=grid_spec,
        out_shape=jax.ShapeDtypeStruct((DEC_BATCH, w, 32), F32),
        compiler_params=_params("arbitrary", "arbitrary"),
        name="moba_sample",
    )(page_table, q_aug, k_new, v_new, slope_col, qs_col, cache_k, cache_v)


def _swa_slope(h):
    return 2.0 ** (-8.0 * (h + 1) / NH_C)


def _swa_heads(q, k_all, v_all, ok, d, sink_row):
    outs = []
    for c in range(KVH_C):
        kc = k_all[:, c * HD_C:(c + 1) * HD_C].astype(BF16)
        vc = v_all[:, c * HD_C:(c + 1) * HD_C].astype(BF16)
        for gi in range(G_C):
            h = c * G_C + gi
            qh = q[:, h * HD_C:(h + 1) * HD_C].astype(BF16)
            sc = jnp.where(ok, _dot_nt(qh, kc) - _swa_slope(h) * d, NEG_INF)
            sink = sink_row[:, h:h + 1]
            mx = jnp.maximum(jnp.max(sc, axis=1, keepdims=True), sink)
            pr = jnp.exp(sc - mx)
            den = jnp.sum(pr, axis=1, keepdims=True) + jnp.exp(sink - mx)
            outs.append(_dot(pr.astype(BF16), vc) / den)
    return outs


def _store_heads(o_ref, outs, lead=()):
    for j in range(0, NH_C, 2):
        o_ref[lead + (slice(None), slice(j * HD_C, (j + 2) * HD_C))] = jnp.concatenate(outs[j:j + 2], axis=1)


def _swa_prompt_kernel(q_ref, kvp_ref, kvc_ref, sink_ref, o_ref):
    i = pl.program_id(1)
    w = WINDOW
    kw = KVH_C * HD_C
    q = q_ref[...] * (HD_C ** -0.5)
    k_all = jnp.concatenate([kvp_ref[:, 0:kw], kvc_ref[:, 0:kw]], axis=0)
    v_all = jnp.concatenate([kvp_ref[:, kw:2 * kw], kvc_ref[:, kw:2 * kw]], axis=0)
    r = lax.broadcasted_iota(jnp.int32, (w, 2 * w), 0)
    cc = lax.broadcasted_iota(jnp.int32, (w, 2 * w), 1)
    di = r - cc + w
    ok = (di >= 0) & (di <= w) & ((cc >= w) | (i > 0))
    _store_heads(o_ref, _swa_heads(q, k_all, v_all, ok, di.astype(F32), sink_ref[...]))


def swa_prompt(p, sink_row):
    nblk = SEQ // WINDOW
    kvb = NH_C * HD_C // (2 * KVH_C * HD_C)
    return pl.pallas_call(
        _swa_prompt_kernel,
        grid=(BATCH, nblk),
        in_specs=[pl.BlockSpec((WINDOW, NH_C * HD_C), lambda b, i: (b * nblk + i, 0)),
                  pl.BlockSpec((WINDOW, 2 * KVH_C * HD_C), lambda b, i: (b * nblk + jnp.maximum(i - 1, 0), kvb)),
                  pl.BlockSpec((WINDOW, 2 * KVH_C * HD_C), lambda b, i: (b * nblk + i, kvb)),
                  pl.BlockSpec((1, 128), lambda b, i: (0, 0))],
        out_specs=pl.BlockSpec((WINDOW, NH_C * HD_C), lambda b, i: (b * nblk + i, 0)),
        out_shape=jax.ShapeDtypeStruct((NP, NH_C * HD_C), F32),
        compiler_params=_params("arbitrary", "arbitrary"),
        name="swa_prompt",
    )(p, p, p, sink_row)


def _swa_sample_kernel(q_ref, kn_ref, vn_ref, ck_ref, cv_ref, sink_ref, o_ref):
    w = WINDOW
    nk = w + 8
    r = lax.broadcasted_iota(jnp.int32, (8, nk), 0)
    cc = lax.broadcasted_iota(jnp.int32, (8, nk), 1)
    di = w + r - cc
    ok = (di >= 0) & (di <= w) & (cc < w + DEC_SEQ)
    d = di.astype(F32)
    sink = sink_ref[...]
    for sq in range(SWA_SEQ_PER_STEP):
        q = q_ref[sq] * (HD_C ** -0.5)
        k_all = jnp.concatenate([ck_ref[sq], kn_ref[sq]], axis=0)
        v_all = jnp.concatenate([cv_ref[sq], vn_ref[sq]], axis=0)
        _store_heads(o_ref, _swa_heads(q, k_all, v_all, ok, d, sink), lead=(sq,))


def swa_sample(q, k_new, v_new, cache_k, cache_v, sink_row):
    n = SWA_SEQ_PER_STEP
    kw = KVH_C * HD_C
    blk = lambda i: (i, 0, 0)
    return pl.pallas_call(
        _swa_sample_kernel,
        grid=(DEC_BATCH // n,),
        in_specs=[pl.BlockSpec((n, 8, NH_C * HD_C), blk), pl.BlockSpec((n, 8, kw), blk), pl.BlockSpec((n, 8, kw), blk),
                  pl.BlockSpec((n, WINDOW, kw), blk), pl.BlockSpec((n, WINDOW, kw), blk),
                  pl.BlockSpec((1, 128), lambda i: (0, 0))],
        out_specs=pl.BlockSpec((n, 8, NH_C * HD_C), blk),
        out_shape=jax.ShapeDtypeStruct((DEC_BATCH, 8, NH_C * HD_C), F32),
        compiler_params=_params("arbitrary"),
        name="swa_sample",
    )(q, k_new, v_new, cache_k, cache_v, sink_row)


def _router_kernel(x_ref, w_ref, b_ref, comb_ref):
    logits = jnp.dot(x_ref[...], w_ref[...], precision=HIGHEST, preferred_element_type=F32) + b_ref[...]
    lane = lax.broadcasted_iota(jnp.int32, logits.shape, 1)
    big = ROUTER_LANES

    lg = jnp.where(lane < N_GROUPS, logits, NEG_INF)
    mg = jnp.max(lg, axis=1, keepdims=True)
    eg = jnp.exp(lg - mg)
    pg = eg / jnp.sum(eg, axis=1, keepdims=True)
    pg1 = jnp.max(pg, axis=1, keepdims=True)
    g1 = jnp.min(jnp.where(pg == pg1, lane, big), axis=1, keepdims=True)

    e = lane - N_GROUPS
    ingroup = (e >= g1 * N_EXP) & (e < (g1 + 1) * N_EXP)
    le = jnp.where(ingroup, logits, NEG_INF)
    me = jnp.max(le, axis=1, keepdims=True)
    ee = jnp.exp(le - me)
    pe = ee / jnp.sum(ee, axis=1, keepdims=True)
    pe = jnp.where(ingroup, pe, NEG_INF)
    p1 = jnp.max(pe, axis=1, keepdims=True)
    i1 = jnp.min(jnp.where(pe == p1, lane, big), axis=1, keepdims=True)
    pe2 = jnp.where(lane == i1, NEG_INF, pe)
    p2 = jnp.max(pe2, axis=1, keepdims=True)
    i2 = jnp.min(jnp.where(pe2 == p2, lane, big), axis=1, keepdims=True)
    tot = p1 + p2
    w1 = p1 / tot * pg1
    w2 = p2 / tot * pg1
    comb_ref[...] = jnp.where(lane == i1, w1, 0.0) + jnp.where(lane == i2, w2, 0.0)


def moe_router(x, w_router, b_router, tm=512):
    m = x.shape[0]
    return pl.pallas_call(
        _router_kernel,
        grid=(m // tm,),
        in_specs=[pl.BlockSpec((tm, D_MODEL), lambda i: (i, 0)),
                  pl.BlockSpec((D_MODEL, ROUTER_LANES), lambda i: (0, 0)),
                  pl.BlockSpec((1, ROUTER_LANES), lambda i: (0, 0))],
        out_specs=pl.BlockSpec((tm, ROUTER_LANES), lambda i: (i, 0)),
        out_shape=jax.ShapeDtypeStruct((m, ROUTER_LANES), F32),
        compiler_params=_params("arbitrary"),
        name="moe_router",
    )(x, w_router, b_router)


def _moe_kernel(x_ref, comb_ref, wg_ref, wu_ref, wd_ref, g_ref, b_ref, o_ref, acc_s, xb_s):
    e = pl.program_id(1)

    @pl.when(e == 0)
    def _():
        acc_s[...] = jnp.zeros_like(acc_s)
        xb_s[...] = x_ref[...].astype(BF16)

    xb = xb_s[...]
    hg = _dot(xb, wg_ref[...])
    hu = _dot(xb, wu_ref[...])
    comb = comb_ref[...]
    lane = lax.broadcasted_iota(jnp.int32, comb.shape, 1)
    wcol = jnp.sum(jnp.where(lane == e + N_GROUPS, comb, 0.0), axis=1, keepdims=True)
    hh = hg / (1.0 + jnp.exp(-hg)) * hu * wcol
    acc_s[...] += _dot(hh.astype(BF16), wd_ref[...])

    @pl.when(e == N_EXPERTS - 1)
    def _():
        o_ref[...] = _layer_norm(ALPHA * x_ref[...] + acc_s[...], g_ref[...], b_ref[...])


def moe_ffn_ln(x, comb, wg, wu, wd, g, b, tm=512):
    m = x.shape[0]
    row = lambda i, e: (i, 0)
    fixed = lambda i, e: (0, 0)
    exp = lambda i, e: (e, 0, 0)
    return pl.pallas_call(
        _moe_kernel,
        grid=(m // tm, N_EXPERTS),
        in_specs=[pl.BlockSpec((tm, D_MODEL), row), pl.BlockSpec((tm, ROUTER_LANES), row),
                  pl.BlockSpec((None, D_MODEL, D_FF), exp), pl.BlockSpec((None, D_MODEL, D_FF), exp),
                  pl.BlockSpec((None, D_FF, D_MODEL), exp),
                  pl.BlockSpec((1, D_MODEL), fixed), pl.BlockSpec((1, D_MODEL), fixed)],
        out_specs=pl.BlockSpec((tm, D_MODEL), row),
        out_shape=jax.ShapeDtypeStruct((m, D_MODEL), F32),
        scratch_shapes=[pltpu.VMEM((tm, D_MODEL), F32), pltpu.VMEM((tm, D_MODEL), BF16)],
        compiler_params=_params("arbitrary", "arbitrary"),
        name="moe_ffn_ln",
    )(x, comb, wg, wu, wd, g.reshape(1, D_MODEL), b.reshape(1, D_MODEL))


def _pad_rows(a, rows):
    return jnp.pad(a, ((0, 0), (0, rows - a.shape[1]), (0, 0)))


def _mlstm_layer(x, w_in, b_gate, norm_g, w_out, c0, n0, m0):
    w = jnp.pad(w_in, ((0, 0), (0, M_IN_PAD - M_IN))).astype(BF16)
    p = matmul(x, w)
    bg = jnp.pad(b_gate, (0, 128 - 2 * NH_M)).reshape(1, 128)
    ng = norm_g.reshape(1, M_V)
    nc = SEQ // M_CHUNK
    hp, cp, np_, mp = mlstm(p, jnp.zeros((BATCH, NH_M, DK_M, DV_M), F32), jnp.zeros((BATCH, NH_M, DK_M), F32),
                            jnp.zeros((BATCH, NH_M), F32), bg, ng, BATCH, nc, M_CHUNK, M_CHUNK)
    ps = _pad_rows(p[NP:].reshape(DEC_BATCH, DEC_SEQ, M_IN_PAD), 8).reshape(DEC_BATCH * 8, M_IN_PAD)
    hs, cs, ns, ms = mlstm(ps, c0, n0, m0, bg, ng, DEC_BATCH, 1, 8, DEC_SEQ)
    hs = hs.reshape(DEC_BATCH, 8, M_V)[:, :DEC_SEQ].reshape(NS, M_V)
    a = jnp.concatenate([hp, hs], axis=0)
    states = (cp, np_, mp.reshape(BATCH, NH_M), cs, ns, ms.reshape(DEC_BATCH, NH_M))
    return a, w_out.astype(BF16), states


def _moba_sample_part(ps, cache_k, cache_v, page_table):
    hq = NH_B * HD_B
    hk = KVH_B * HD_B
    slopes_h = jnp.exp2(-8.0 * jnp.arange(1, NH_B + 1, dtype=F32) / NH_B)
    q = (ps[:, :hq] * (HD_B ** -0.5)).reshape(DEC_BATCH, DEC_SEQ, KVH_B, G_B, HD_B)
    eye = jnp.eye(KVH_B, dtype=F32)
    q_aug = jnp.einsum("bscgd,ce->bcdesg", q, eye).reshape(DEC_BATCH, hk, KVH_B * DEC_SEQ * G_B)
    k_new = ps[:, hq:hq + hk].reshape(DEC_BATCH, DEC_SEQ, hk)
    v_new = ps[:, hq + hk:].reshape(DEC_BATCH, DEC_SEQ, hk)
    col = jnp.arange(KVH_B * DEC_SEQ * G_B)
    col_c, col_s, col_g = col // (DEC_SEQ * G_B), (col // G_B) % DEC_SEQ, col % G_B
    slope_col = slopes_h[col_c * G_B + col_g].reshape(1, -1)
    qs_col = col_s.astype(F32).reshape(1, -1)
    n_pool = cache_k.shape[0]
    ot = moba_sample(page_table, q_aug, _pad_rows(k_new, 8), _pad_rows(v_new, 8), slope_col, qs_col,
                     cache_k.reshape(n_pool, PAGE_SIZE, hk), cache_v.reshape(n_pool, PAGE_SIZE, hk))
    ot = ot.reshape(DEC_BATCH, KVH_B, HD_B, KVH_B, DEC_SEQ, G_B)
    os_ = jnp.einsum("bcdesg,ce->bscgd", ot, eye).reshape(NS, hq)
    return os_, k_new, v_new


def _moba_layer(x, w_in, w_out, cache_k, cache_v, page_table):
    p = matmul(x, w_in.astype(BF16))
    hq = NH_B * HD_B
    hk = KVH_B * HD_B
    slopes_h = jnp.exp2(-8.0 * jnp.arange(1, NH_B + 1, dtype=F32) / NH_B)
    sel = moba_select(p)
    op = moba_flash(p, sel, jnp.broadcast_to(slopes_h[:, None, None], (NH_B, 1, 128)))
    os_, k_new, v_new = _moba_sample_part(p[NP:], cache_k, cache_v, page_table)
    a = jnp.concatenate([op, os_], axis=0)
    kv = (p[:NP, hq:hq + hk].reshape(BATCH, SEQ, KVH_B, HD_B), p[:NP, hq + hk:].reshape(BATCH, SEQ, KVH_B, HD_B),
          k_new.reshape(DEC_BATCH, DEC_SEQ, KVH_B, HD_B), v_new.reshape(DEC_BATCH, DEC_SEQ, KVH_B, HD_B))
    return a, w_out.astype(BF16), kv


def _swa_layer(x, w_in, sinks, w_out, cache_k, cache_v):
    p = matmul(x, w_in.astype(BF16))
    hq = NH_C * HD_C
    kw = KVH_C * HD_C
    sink_row = jnp.pad(sinks, (0, 128 - NH_C)).reshape(1, 128)
    op = swa_prompt(p, sink_row)
    ps = p[NP:]
    q = _pad_rows(ps[:, :hq].reshape(DEC_BATCH, DEC_SEQ, hq), 8)
    k_new = ps[:, hq:hq + kw].reshape(DEC_BATCH, DEC_SEQ, kw)
    v_new = ps[:, hq + kw:].reshape(DEC_BATCH, DEC_SEQ, kw)
    ck = cache_k.reshape(DEC_BATCH, WINDOW, kw)
    cv = cache_v.reshape(DEC_BATCH, WINDOW, kw)
    os_ = swa_sample(q, _pad_rows(k_new, 8), _pad_rows(v_new, 8), ck, cv, sink_row)[:, :DEC_SEQ].reshape(NS, hq)
    a = jnp.concatenate([op, os_], axis=0)
    pp = p[:NP].reshape(BATCH, SEQ, C_IN)[:, SEQ - WINDOW:]
    kv = (pp[..., hq:hq + kw].reshape(BATCH, WINDOW, KVH_C, HD_C), pp[..., hq + kw:].reshape(BATCH, WINDOW, KVH_C, HD_C),
          jnp.concatenate([ck, k_new], axis=1)[:, DEC_SEQ:].reshape(DEC_BATCH, WINDOW, KVH_C, HD_C),
          jnp.concatenate([cv, v_new], axis=1)[:, DEC_SEQ:].reshape(DEC_BATCH, WINDOW, KVH_C, HD_C))
    return a, w_out.astype(BF16), kv


def _moe_layer(x, w_group, b_group, w_router, b_router, w_gate, w_up, w_down, g, b):
    pad = ROUTER_LANES - N_GROUPS - N_EXPERTS
    wr = jnp.concatenate([w_group, w_router, jnp.zeros((D_MODEL, pad), F32)], axis=1)
    br = jnp.concatenate([b_group, b_router, jnp.zeros((pad,), F32)]).reshape(1, ROUTER_LANES)
    comb = moe_router(x, wr, br)
    return moe_ffn_ln(x, comb, w_gate.astype(BF16), w_up.astype(BF16), w_down.astype(BF16), g, b)


def kernel(x_prompt, x_sample, state_mlstm_C, state_mlstm_n, state_mlstm_m, cache_moba_k, cache_moba_v, cache_swa_k, cache_swa_v, page_table, mlstm_w_in, mlstm_b_gate, mlstm_norm_g, mlstm_w_out, moba_w_in, moba_w_out, swa_w_in, swa_sinks, swa_w_out, ln_mix_g, ln_mix_b, ln_ffn_g, ln_ffn_b, moe_w_group, moe_b_group, moe_w_router, moe_b_router, moe_w_gate, moe_w_up, moe_w_down):
    x = jnp.concatenate([x_prompt.reshape(NP, D_MODEL), x_sample.reshape(NS, D_MODEL)], axis=0)
    m_states, b_kv, c_kv = [], [], []
    for layer in range(DEPTH):
        kind, slot = layer % 3, layer // 3
        if kind == 0:
            a, w_out, st = _mlstm_layer(x, mlstm_w_in[slot], mlstm_b_gate[slot], mlstm_norm_g[slot], mlstm_w_out[slot],
                                        state_mlstm_C[slot], state_mlstm_n[slot], state_mlstm_m[slot])
            m_states.append(st)
        elif kind == 1:
            a, w_out, kv = _moba_layer(x, moba_w_in[slot], moba_w_out[slot], cache_moba_k[slot], cache_moba_v[slot],
                                       page_table)
            b_kv.append(kv)
        else:
            a, w_out, kv = _swa_layer(x, swa_w_in[slot], swa_sinks[slot], swa_w_out[slot], cache_swa_k[slot],
                                      cache_swa_v[slot])
            c_kv.append(kv)
        x = outproj_ln(a, w_out, x, ln_mix_g[layer], ln_mix_b[layer])
        x = _moe_layer(x, moe_w_group[layer], moe_b_group[layer], moe_w_router[layer], moe_b_router[layer],
                       moe_w_gate[layer], moe_w_up[layer], moe_w_down[layer], ln_ffn_g[layer], ln_ffn_b[layer])
    stack = lambda items, j: jnp.stack([it[j] for it in items])
    return (x[:NP].reshape(BATCH, SEQ, D_MODEL), x[NP:].reshape(DEC_BATCH, DEC_SEQ, D_MODEL),
            stack(m_states, 0), stack(m_states, 1), stack(m_states, 2),
            stack(m_states, 3), stack(m_states, 4), stack(m_states, 5),
            stack(b_kv, 0), stack(b_kv, 1), stack(b_kv, 2), stack(b_kv, 3),
            stack(c_kv, 0), stack(c_kv, 1), stack(c_kv, 2), stack(c_kv, 3))
```

```python
import functools

import jax
import jax.numpy as jnp
import numpy as np
from jax import lax
from jax.experimental import pallas as pl
from jax.experimental.pallas import tpu as pltpu

F32 = jnp.float32
BF16 = jnp.bfloat16
I32 = jnp.int32
HIGHEST = lax.Precision.HIGHEST

D_MODEL = 1024
BATCH = 2
SEQ = 8192
DEPTH = 4
DEC_BATCH = 128
DEC_SEQ = 4
PAGE_SIZE = 128
N_PAGES = 64
PAST = N_PAGES * PAGE_SIZE
NP = BATCH * SEQ
NS = DEC_BATCH * DEC_SEQ
NT = NP + NS

NH_M = 4
DK_M = 128
DV_M = 256
M_CHUNK = 128
M_QK = NH_M * DK_M
M_V = NH_M * DV_M
M_IN = 2 * M_QK + 2 * M_V + 2 * NH_M
M_IN_PAD = 3200
M_GATE_COL = 2 * M_QK + 2 * M_V

NH_B = 8
KVH_B = 4
G_B = 2
HD_B = 128
MOBA_BLOCK = 256
MOBA_TOPK = 3
NB_B = SEQ // MOBA_BLOCK
NB_S = PAST // MOBA_BLOCK
B_IN = (NH_B + 2 * KVH_B) * HD_B
SAMPLE_PAGES_PER_STEP = 16
AUG = 2 * HD_B
POS_LANE = NB_B

NH_C = 16
KVH_C = 2
G_C = 8
HD_C = 64
WINDOW = 128
C_IN = (NH_C + 2 * KVH_C) * HD_C
SWA_SEQ_PER_STEP = 8

N_GROUPS = 4
N_EXP = 8
N_EXPERTS = 32
D_FF = 256
ROUTER_ROWS = 128
MOE_TM = 1536
MOE_RC = 128
MOE_ROWS = 2 * MOE_TM + N_EXPERTS * 8 + MOE_RC

ALPHA = (2.0 * DEPTH) ** 0.25
EPS = 1e-5
LOG2E = 1.4426950408889634
NEG_INF = float("-inf")
MASKED = -1e30
VMEM_LIMIT = 56 * 1024 * 1024


def _params(*sem):
    return pltpu.CompilerParams(dimension_semantics=sem, vmem_limit_bytes=VMEM_LIMIT)


def _dot(a, b):
    return jnp.dot(a, b, preferred_element_type=F32)


def _dot_nt(a, b, precision=None):
    return lax.dot_general(a, b, (((1,), (1,)), ((), ())), precision=precision, preferred_element_type=F32)


def _dot_tn(a, b):
    return lax.dot_general(a, b, (((0,), (0,)), ((), ())), preferred_element_type=F32)


def _layer_norm(z, g, b):
    mu = jnp.mean(z, axis=-1, keepdims=True)
    zc = z - mu
    var = jnp.mean(zc * zc, axis=-1, keepdims=True)
    return zc * lax.rsqrt(var + EPS) * g + b


def _bf16_parts(x, n=3):
    parts = []
    for _ in range(n):
        bits = int(np.float32(x).view(np.uint32))
        rounded = ((bits + 0x7FFF + ((bits >> 16) & 1)) >> 16) << 16
        p = float(np.uint32(rounded & 0xFFFFFFFF).view(np.float32))
        parts.append(p)
        x = x - p
    return parts


def _mm_kernel(x_ref, w_ref, o_ref):
    o_ref[...] = _dot(x_ref[...].astype(BF16), w_ref[...])


def matmul(x, w_bf16, tm=512):
    m, k = x.shape
    n = w_bf16.shape[1]
    return pl.pallas_call(
        _mm_kernel,
        grid=(m // tm,),
        in_specs=[pl.BlockSpec((tm, k), lambda i: (i, 0)), pl.BlockSpec((k, n), lambda i: (0, 0))],
        out_specs=pl.BlockSpec((tm, n), lambda i: (i, 0)),
        out_shape=jax.ShapeDtypeStruct((m, n), F32),
        compiler_params=_params("arbitrary"),
        name="in_proj",
    )(x, w_bf16)


def _outproj_ln_kernel(a_ref, w_ref, r_ref, g_ref, b_ref, o_ref):
    y = _dot(a_ref[...].astype(BF16), w_ref[...])
    o_ref[...] = _layer_norm(ALPHA * r_ref[...] + y, g_ref[...], b_ref[...])


def outproj_ln(a, w_bf16, resid, g, b, tm=512):
    m, k = a.shape
    n = w_bf16.shape[1]
    row = lambda i: (i, 0)
    fixed = lambda i: (0, 0)
    return pl.pallas_call(
        _outproj_ln_kernel,
        grid=(m // tm,),
        in_specs=[pl.BlockSpec((tm, k), row), pl.BlockSpec((k, n), fixed), pl.BlockSpec((tm, n), row),
                  pl.BlockSpec((1, n), fixed), pl.BlockSpec((1, n), fixed)],
        out_specs=pl.BlockSpec((tm, n), row),
        out_shape=jax.ShapeDtypeStruct((m, n), F32),
        compiler_params=_params("arbitrary"),
        name="out_proj_ln",
    )(a, w_bf16, resid, g.reshape(1, n), b.reshape(1, n))


def _log_sigmoid(x):
    return jnp.minimum(x, 0.0) - jnp.log1p(jnp.exp(-jnp.abs(x)))


def _mlstm_kernel(p_ref, c0_ref, n0_ref, m0_ref, bg_ref, ng_ref,
                  hg_ref, co_ref, no_ref, mo_ref, c_s, n_s, m_s, *, L, L_real):
    c = pl.program_id(1)

    @pl.when(c == 0)
    def _():
        c_s[...] = c0_ref[...]
        n_s[...] = n0_ref[...]
        m_s[...] = m0_ref[...]

    gates = p_ref[:, M_GATE_COL:M_IN_PAD] + bg_ref[...]
    lane = lax.broadcasted_iota(jnp.int32, (L, 128), 1)
    x = jnp.where(lane < NH_M, gates, _log_sigmoid(gates))
    if L_real < L:
        tok = lax.broadcasted_iota(jnp.int32, (L, 128), 0)
        x = jnp.where(tok < L_real, x, jnp.where(lane < NH_M, -1e30, 0.0))
    r = lax.broadcasted_iota(jnp.int32, (L, L), 0)
    s = lax.broadcasted_iota(jnp.int32, (L, L), 1)
    causal = r >= s
    tri = causal.astype(F32)
    bcum = jnp.dot(tri, x, precision=HIGHEST, preferred_element_type=F32)
    eye8 = (lax.broadcasted_iota(jnp.int32, (8, 128), 0) == lax.broadcasted_iota(jnp.int32, (8, 128), 1)).astype(F32)
    x_rows = _dot_nt(eye8, x, HIGHEST)
    b_rows = _dot_nt(eye8, bcum, HIGHEST)

    for h in range(NH_M):
        li_row = x_rows[h:h + 1, :]
        li_col = x[:, h:h + 1]
        b_row = b_rows[NH_M + h:NH_M + h + 1, :]
        b_col = bcum[:, NH_M + h:NH_M + h + 1]
        m_old = m_s[:, h:h + 1]
        dm = jnp.where(causal, b_col - b_row + li_row, NEG_INF)
        g_col = b_col + m_old
        mq = jnp.maximum(g_col, jnp.max(dm, axis=1, keepdims=True))
        w_intra = jnp.exp(dm - mq)
        w_inter = jnp.exp(g_col - mq)
        q = p_ref[:, h * DK_M:(h + 1) * DK_M] * (DK_M ** -0.5)
        k = p_ref[:, M_QK + h * DK_M:M_QK + (h + 1) * DK_M]
        v = p_ref[:, 2 * M_QK + h * DV_M:2 * M_QK + (h + 1) * DV_M]
        qb = q.astype(BF16)
        kb = k.astype(BF16)
        cmat = c_s[h]
        nrow = n_s[h:h + 1, :]
        sc = _dot_nt(qb, kb) * w_intra
        num = _dot(sc.astype(BF16), v.astype(BF16)) + w_inter * _dot(qb, cmat.astype(BF16))
        den = jnp.sum(sc, axis=1, keepdims=True) + w_inter * jnp.sum(q * nrow, axis=1, keepdims=True)
        hh = num / jnp.maximum(jnp.abs(den), jnp.exp(-mq))
        bl = b_col[L - 1:L, :]
        lw = bl - b_col + li_col
        m_new = jnp.maximum(bl + m_old, jnp.max(lw, axis=0, keepdims=True))
        wl = jnp.exp(lw - m_new)
        dec = jnp.exp(bl + m_old - m_new)
        c_s[h] = dec * cmat + _dot_tn(kb, (wl * v).astype(BF16))
        n_s[h:h + 1, :] = dec * nrow + jnp.sum(wl * k, axis=0, keepdims=True)
        m_s[:, h:h + 1] = m_new
        mu = jnp.mean(hh, axis=1, keepdims=True)
        hc = hh - mu
        var = jnp.mean(hc * hc, axis=1, keepdims=True)
        hn = hc * lax.rsqrt(var + EPS) * ng_ref[:, h * DV_M:(h + 1) * DV_M]
        o = p_ref[:, 2 * M_QK + M_V + h * DV_M:2 * M_QK + M_V + (h + 1) * DV_M]
        hg_ref[:, h * DV_M:(h + 1) * DV_M] = hn / (1.0 + jnp.exp(-o))

    @pl.when(c == pl.num_programs(1) - 1)
    def _():
        co_ref[...] = c_s[...]
        no_ref[...] = n_s[...]
        mo_ref[...] = m_s[...]


def mlstm(p, c0, n0, m0, b_gate_row, norm_g_row, nb, nc, L, L_real):
    kern = functools.partial(_mlstm_kernel, L=L, L_real=L_real)
    per_b = lambda b, c: (b, 0, 0)
    return pl.pallas_call(
        kern,
        grid=(nb, nc),
        in_specs=[pl.BlockSpec((L, M_IN_PAD), lambda b, c: (b * nc + c, 0)),
                  pl.BlockSpec((None, NH_M, DK_M, DV_M), lambda b, c: (b, 0, 0, 0)),
                  pl.BlockSpec((None, NH_M, DK_M), per_b),
                  pl.BlockSpec((None, 1, NH_M), per_b),
                  pl.BlockSpec((1, 128), lambda b, c: (0, 0)),
                  pl.BlockSpec((1, M_V), lambda b, c: (0, 0))],
        out_specs=[pl.BlockSpec((L, M_V), lambda b, c: (b * nc + c, 0)),
                   pl.BlockSpec((None, NH_M, DK_M, DV_M), lambda b, c: (b, 0, 0, 0)),
                   pl.BlockSpec((None, NH_M, DK_M), per_b),
                   pl.BlockSpec((None, 1, NH_M), per_b)],
        out_shape=[jax.ShapeDtypeStruct((nb * nc * L, M_V), F32),
                   jax.ShapeDtypeStruct((nb, NH_M, DK_M, DV_M), F32),
                   jax.ShapeDtypeStruct((nb, NH_M, DK_M), F32),
                   jax.ShapeDtypeStruct((nb, 1, NH_M), F32)],
        scratch_shapes=[pltpu.VMEM((NH_M, DK_M, DV_M), F32), pltpu.VMEM((NH_M, DK_M), F32),
                        pltpu.VMEM((1, NH_M), F32)],
        compiler_params=_params("arbitrary", "arbitrary"),
        name=f"mlstm_L{L}",
    )(p, c0, n0, m0.reshape(nb, 1, NH_M), b_gate_row, norm_g_row)


def _topk_mask_lanes(g, k):
    lane = lax.broadcasted_iota(jnp.int32, g.shape, 1)
    n = g.shape[1]
    sel = jnp.zeros(g.shape, F32)
    for _ in range(k):
        mx = jnp.max(g, axis=1, keepdims=True)
        idx = jnp.min(jnp.where(g == mx, lane, n), axis=1, keepdims=True)
        hit = (lane == idx) & (mx > NEG_INF)
        sel = jnp.where(hit, 1.0, sel)
        g = jnp.where(lane == idx, NEG_INF, g)
    return sel


def _moba_select_kernel(q_ref, k_ref, v_ref, ac_ref, qa_ref, ka_ref, vt_ref, kmean_s, *, tq):
    i = pl.program_id(2)

    @pl.when(i == 0)
    def _():
        kmean_s[...] = jnp.zeros_like(kmean_s)
        kmean_s[0:NB_B, :] = jnp.sum(k_ref[...].reshape(NB_B, MOBA_BLOCK, HD_B), axis=1) * (1.0 / MOBA_BLOCK)
        rows = 1024
        lane = lax.broadcasted_iota(jnp.int32, (rows, HD_B), 1)
        rloc = lax.broadcasted_iota(jnp.int32, (rows, HD_B), 0)
        for ch in range(SEQ // rows):
            vt_ref[:, ch * rows:(ch + 1) * rows] = v_ref[ch * rows:(ch + 1) * rows, :].T.astype(BF16)
            row = rloc + ch * rows
            onehot = (lane == row // MOBA_BLOCK).astype(F32)
            offs = (row % MOBA_BLOCK).astype(F32)
            blks = (row // MOBA_BLOCK).astype(F32)
            pat = jnp.where(lane < NB_B, onehot,
                            jnp.where(lane < POS_LANE + 3, offs, jnp.where(lane < POS_LANE + 6, blks, 0.0)))
            ka_ref[ch * rows:(ch + 1) * rows, 0:HD_B] = k_ref[ch * rows:(ch + 1) * rows, :].astype(BF16)
            ka_ref[ch * rows:(ch + 1) * rows, HD_B:AUG] = pat.astype(BF16)

    km = kmean_s[...]
    lane = lax.broadcasted_iota(jnp.int32, (tq, HD_B), 1)
    own = (i * tq + lax.broadcasted_iota(jnp.int32, (tq, HD_B), 0)) // MOBA_BLOCK
    for g in range(G_B):
        q = q_ref[:, g * HD_B:(g + 1) * HD_B]
        gate = _dot_nt(q * (HD_B ** -0.5), km, HIGHEST)
        gate = jnp.where(lane < own, gate, NEG_INF)
        sel = _topk_mask_lanes(gate, MOBA_TOPK)
        sel = jnp.where(lane == own, 1.0, sel)
        aug = jnp.where(lane < NB_B, jnp.where(sel > 0.5, 0.0, MASKED), ac_ref[g, 0:1, :])
        qa_ref[:, g * AUG:g * AUG + HD_B] = (q * (HD_B ** -0.5 * LOG2E)).astype(BF16)
        qa_ref[:, g * AUG + HD_B:(g + 1) * AUG] = aug.astype(BF16)


def moba_select(p, aug_const, tq=1024):
    kern = functools.partial(_moba_select_kernel, tq=tq)
    nq = SEQ // tq
    per_bc = lambda b, c, i: (b, c, 0, 0)
    return pl.pallas_call(
        kern,
        grid=(BATCH, KVH_B, nq),
        in_specs=[pl.BlockSpec((tq, G_B * HD_B), lambda b, c, i: (b * nq + i, c)),
                  pl.BlockSpec((SEQ, HD_B), lambda b, c, i: (b, NH_B + c)),
                  pl.BlockSpec((SEQ, HD_B), lambda b, c, i: (b, NH_B + KVH_B + c)),
                  pl.BlockSpec((G_B, 8, 128), lambda b, c, i: (c, 0, 0))],
        out_specs=[pl.BlockSpec((tq, G_B * AUG), lambda b, c, i: (b * nq + i, c)),
                   pl.BlockSpec((None, None, SEQ, AUG), per_bc),
                   pl.BlockSpec((None, None, HD_B, SEQ), per_bc)],
        out_shape=[jax.ShapeDtypeStruct((NP, NH_B * AUG), BF16),
                   jax.ShapeDtypeStruct((BATCH, KVH_B, SEQ, AUG), BF16),
                   jax.ShapeDtypeStruct((BATCH, KVH_B, HD_B, SEQ), BF16)],
        scratch_shapes=[pltpu.VMEM((128, HD_B), F32)],
        compiler_params=_params("arbitrary", "arbitrary", "arbitrary"),
        name="moba_select",
    )(p, p, p, aug_const)


def _moba_flash_kernel(qa_ref, ka_ref, vt_ref, o_ref, s_buf):
    i = pl.program_id(2)
    bq = MOBA_BLOCK
    tk = 2 * MOBA_BLOCK
    nl = G_B * bq
    qa = jnp.concatenate([qa_ref[:, g * AUG:(g + 1) * AUG] for g in range(G_B)], axis=0)

    def scores(t, slot):
        off = pl.multiple_of(t * tk, tk)
        s_buf[slot] = _dot_nt(ka_ref[pl.ds(off, tk), :], qa)

    def softmax_pv(t, slot, last, carry):
        m, l, acc = carry
        st = s_buf[slot]
        if last:
            kpos = t * tk + lax.broadcasted_iota(jnp.int32, (tk, nl), 0)
            qpos = i * bq + lax.broadcasted_iota(jnp.int32, (tk, nl), 1) % bq
            st = jnp.where(kpos <= qpos, st, NEG_INF)
        m_new = jnp.maximum(m, jnp.max(st, axis=0, keepdims=True))
        pt = jnp.exp2(st - m_new)
        a = jnp.exp2(m - m_new)
        l = a * l + jnp.sum(pt, axis=0, keepdims=True)
        off = pl.multiple_of(t * tk, tk)
        acc = a * acc + _dot(vt_ref[:, pl.ds(off, tk)], pt.astype(BF16))
        return m_new, l, acc

    def pair(u, carry):
        scores(2 * u + 1, 1)
        carry = softmax_pv(2 * u, 0, False, carry)
        scores(2 * u + 2, 0)
        return softmax_pv(2 * u + 1, 1, False, carry)

    def tail_two(carry):
        t = 2 * npairs
        scores(t + 1, 1)
        return softmax_pv(t + 1, 1, True, softmax_pv(t, 0, False, carry))

    def tail_one(carry):
        return softmax_pv(2 * npairs, 0, True, carry)

    n_full = i // 2
    npairs = n_full // 2
    init = (jnp.full((1, nl), MASKED, F32), jnp.zeros((1, nl), F32), jnp.zeros((HD_B, nl), F32))
    scores(0, 0)
    carry = lax.fori_loop(0, npairs, pair, init)
    m, l, acc = lax.cond(n_full % 2 == 1, tail_two, tail_one, carry)
    out = acc / l
    for g in range(G_B):
        o_ref[:, g * HD_B:(g + 1) * HD_B] = out[:, g * bq:(g + 1) * bq].T


def moba_flash(q_aug, k_aug, v_t):
    nq = SEQ // MOBA_BLOCK
    per_bc = lambda b, c, i: (b, c, 0, 0)
    return pl.pallas_call(
        _moba_flash_kernel,
        grid=(BATCH, KVH_B, nq),
        in_specs=[pl.BlockSpec((MOBA_BLOCK, G_B * AUG), lambda b, c, i: (b * nq + i, c)),
                  pl.BlockSpec((None, None, SEQ, AUG), per_bc),
                  pl.BlockSpec((None, None, HD_B, SEQ), per_bc)],
        out_specs=pl.BlockSpec((MOBA_BLOCK, G_B * HD_B), lambda b, c, i: (b * nq + i, c)),
        out_shape=jax.ShapeDtypeStruct((NP, NH_B * HD_B), F32),
        scratch_shapes=[pltpu.VMEM((2, 2 * MOBA_BLOCK, G_B * MOBA_BLOCK), F32)],
        compiler_params=_params("arbitrary", "arbitrary", "arbitrary"),
        name="moba_flash",
    )(q_aug, k_aug, v_t)


def _moba_aug_const():
    c = np.zeros((NH_B, 8, 128), np.float32)
    for h in range(NH_B):
        beta = 2.0 ** (-8.0 * (h + 1) / NH_B) * LOG2E
        c[h, 0, POS_LANE:POS_LANE + 3] = _bf16_parts(beta)
        c[h, 0, POS_LANE + 3:POS_LANE + 6] = _bf16_parts(beta * MOBA_BLOCK)
    return jnp.asarray(c)


def _moba_sample_kernel(pt_ref, qa_ref, kn_ref, vn_ref, bias_ref, slope_ref, qs_ref, ck_ref, cv_ref,
                        o_ref, buf, sem, s_s, p_s, pown_s, ksum_s, acc_s):
    npg = SAMPLE_PAGES_PER_STEP
    b = pl.program_id(0)
    t = pl.program_id(1)
    nb = pl.num_programs(0)
    g = b * 8 + t
    slot = g % 2

    def page_copy(src_ref, page, sl, pg):
        return pltpu.make_async_copy(src_ref.at[page], buf.at[sl, pg], sem.at[sl])

    def start_fetch(bn, tn, sl):
        qn = tn % 4

        @pl.when(tn < 4)
        def _():
            for pg in range(npg):
                page_copy(ck_ref, pt_ref[bn, qn * npg + pg], sl, pg).start()

        @pl.when(tn >= 4)
        def _():
            for pg in range(npg):
                page_copy(cv_ref, pt_ref[bn, qn * npg + pg], sl, pg).start()

    @pl.when(g == 0)
    def _():
        start_fetch(b, t, slot)

    @pl.when(g + 1 < nb * 8)
    def _():
        tn = (t + 1) % 8
        bn = b + (t + 1) // 8
        start_fetch(bn, tn, 1 - slot)

    for pg in range(npg):
        page_copy(ck_ref, 0, slot, pg).wait()

    qa = qa_ref[...]
    qab = (qa * LOG2E).astype(BF16)
    slope = slope_ref[...]
    qs = qs_ref[...]

    @pl.when(t < 4)
    def _():
        for pg in range(npg):
            page = buf[slot, pg]
            row0 = pl.multiple_of((t * npg + pg) * PAGE_SIZE, PAGE_SIZE)
            s_s[pl.ds(row0, PAGE_SIZE), :] = _dot(page.astype(BF16), qab)
            csum = jnp.sum(page, axis=0, keepdims=True)
            if pg % 2 == 0:
                prev = csum
            else:
                ksum_s[pl.ds(t * (npg // 2) + pg // 2, 1), :] = prev + csum

    @pl.when(t == 3)
    def _():
        kmean = ksum_s[...] * (1.0 / MOBA_BLOCK)
        gate = jnp.dot(kmean, qa, precision=HIGHEST, preferred_element_type=F32)
        rowi = lax.broadcasted_iota(jnp.int32, gate.shape, 0)
        sel = jnp.zeros(gate.shape, F32)
        for _ in range(MOBA_TOPK):
            mx = jnp.max(gate, axis=0, keepdims=True)
            idx = jnp.min(jnp.where(gate == mx, rowi, NB_S), axis=0, keepdims=True)
            sel = jnp.where(rowi == idx, 1.0, sel)
            gate = jnp.where(rowi == idx, NEG_INF, gate)
        sc = (s_s[...] + bias_ref[...]).reshape(NB_S, MOBA_BLOCK, 32)
        sc = jnp.where(sel[:, None, :] > 0.5, sc, NEG_INF)
        so = _dot(kn_ref[...].astype(BF16), qab)
        tok = lax.broadcasted_iota(jnp.int32, so.shape, 0).astype(F32)
        so = jnp.where((tok <= qs) & (tok < float(DEC_SEQ)), so - slope * (qs - tok), NEG_INF)
        mx = jnp.maximum(jnp.max(jnp.max(sc, axis=1), axis=0, keepdims=True), jnp.max(so, axis=0, keepdims=True))
        pr = jnp.exp2(sc - mx[None])
        po = jnp.exp2(so - mx)
        den = jnp.sum(jnp.sum(pr, axis=1), axis=0, keepdims=True) + jnp.sum(po, axis=0, keepdims=True)
        inv = 1.0 / den
        p_s[...] = (pr * inv[None]).reshape(PAST, 32)
        pown_s[...] = po * inv

    @pl.when(t == 4)
    def _():
        acc_s[...] = _dot_tn(vn_ref[...].astype(BF16), pown_s[...].astype(BF16))

    @pl.when(t >= 4)
    def _():
        acc = acc_s[...]
        for pg in range(npg):
            page = buf[slot, pg]
            row0 = pl.multiple_of(((t - 4) * npg + pg) * PAGE_SIZE, PAGE_SIZE)
            pt = p_s[pl.ds(row0, PAGE_SIZE), :]
            acc = acc + _dot_tn(page.astype(BF16), pt.astype(BF16))
        acc_s[...] = acc

    @pl.when(t == 7)
    def _():
        o_ref[...] = acc_s[...]


def moba_sample(page_table, q_aug, k_new, v_new, bias, slope_col, qs_col, cache_k, cache_v):
    w = KVH_B * HD_B
    per_b3 = lambda b, t, pt: (b, 0, 0)
    fixed = lambda b, t, pt: (0, 0)
    grid_spec = pltpu.PrefetchScalarGridSpec(
        num_scalar_prefetch=1,
        grid=(DEC_BATCH, 8),
        in_specs=[pl.BlockSpec((None, w, 32), per_b3),
                  pl.BlockSpec((None, 8, w), per_b3),
                  pl.BlockSpec((None, 8, w), per_b3),
                  pl.BlockSpec((PAST, 32), fixed),
                  pl.BlockSpec((1, 32), fixed),
                  pl.BlockSpec((1, 32), fixed),
                  pl.BlockSpec(memory_space=pl.ANY),
                  pl.BlockSpec(memory_space=pl.ANY)],
        out_specs=pl.BlockSpec((None, w, 32), per_b3),
        scratch_shapes=[pltpu.VMEM((2, SAMPLE_PAGES_PER_STEP, PAGE_SIZE, w), F32),
                        pltpu.SemaphoreType.DMA((2,)),
                        pltpu.VMEM((PAST, 32), F32),
                        pltpu.VMEM((PAST, 32), F32),
                        pltpu.VMEM((8, 32), F32),
                        pltpu.VMEM((NB_S, w), F32),
                        pltpu.VMEM((w, 32), F32)],
    )
    return pl.pallas_call(
        _moba_sample_kernel,
        grid_spec=grid_spec,
        out_shape=jax.ShapeDtypeStruct((DEC_BATCH, w, 32), F32),
        compiler_params=_params("arbitrary", "arbitrary"),
        name="moba_sample",
    )(page_table, q_aug, k_new, v_new, bias, slope_col, qs_col, cache_k, cache_v)


def _swa_slope(h):
    return 2.0 ** (-8.0 * (h + 1) / NH_C)


def _by_head(gidx, values):
    out = values[-1]
    for g in range(len(values) - 2, -1, -1):
        out = jnp.where(gidx == g, values[g], out)
    return out


def _swa_kv_aug(k_all, v_all, c):
    nk = k_all.shape[0]
    lane = lax.broadcasted_iota(jnp.int32, (nk, HD_C), 1)
    kidx = lax.broadcasted_iota(jnp.int32, (nk, HD_C), 0).astype(F32)
    kpat = jnp.where(lane < 3, kidx, 0.0)
    ka = jnp.concatenate([k_all[:, c * HD_C:(c + 1) * HD_C], kpat], axis=1).astype(BF16)
    return ka, v_all[:, c * HD_C:(c + 1) * HD_C].astype(BF16)


def _swa_softmax_pv(qa, ka, vc, maskt, sink2):
    s2 = _dot_nt(qa, ka) + maskt
    mx = jnp.maximum(jnp.max(s2, axis=1, keepdims=True), sink2)
    pr = jnp.exp2(s2 - mx)
    den = jnp.sum(pr, axis=1, keepdims=True) + jnp.exp2(sink2 - mx)
    return _dot(pr.astype(BF16), vc) / den


def _store_heads(o_ref, outs, lead=()):
    for j in range(0, NH_C, 2):
        o_ref[lead + (slice(None), slice(j * HD_C, (j + 2) * HD_C))] = jnp.concatenate(outs[j:j + 2], axis=1)


def _swa_prompt_kernel(q_ref, kvp_ref, kvc_ref, sink_ref, o_ref):
    i = pl.program_id(1)
    w = WINDOW
    kw = KVH_C * HD_C
    k_all = jnp.concatenate([kvp_ref[:, 0:kw], kvc_ref[:, 0:kw]], axis=0)
    v_all = jnp.concatenate([kvp_ref[:, kw:2 * kw], kvc_ref[:, kw:2 * kw]], axis=0)
    kk = lax.broadcasted_iota(jnp.int32, (2 * w, w), 0)
    qq = lax.broadcasted_iota(jnp.int32, (2 * w, w), 1)
    di = qq - kk + w
    ok = (di >= 0) & (di <= w) & ((kk >= w) | (i > 0))
    mask1 = jnp.where(ok, 0.0, NEG_INF)
    maskt = jnp.concatenate([mask1] * G_C, axis=1)
    qdist = (lax.broadcasted_iota(jnp.int32, (1, w), 1) + w).astype(F32)
    qlane = lax.broadcasted_iota(jnp.int32, (w, HD_C), 1)
    sink = sink_ref[...]
    for c in range(KVH_C):
        ka, vc = _swa_kv_aug(k_all, v_all, c)
        qas, sinks = [], []
        for gi in range(G_C):
            h = c * G_C + gi
            b1, b2, b3 = _bf16_parts(_swa_slope(h) * LOG2E)
            qpat = jnp.where(qlane == 0, b1, jnp.where(qlane == 1, b2, jnp.where(qlane == 2, b3, 0.0)))
            qa = jnp.concatenate([q_ref[:, h * HD_C:(h + 1) * HD_C] * (HD_C ** -0.5 * LOG2E), qpat], axis=1)
            qas.append(qa.astype(BF16))
            sinks.append((sink[:, h:h + 1] + _swa_slope(h) * qdist) * LOG2E)
        sink2 = jnp.concatenate(sinks, axis=1)
        st = _dot_nt(ka, jnp.concatenate(qas, axis=0)) + maskt
        mx = jnp.maximum(jnp.max(st, axis=0, keepdims=True), sink2)
        pt = jnp.exp2(st - mx)
        den = jnp.sum(pt, axis=0, keepdims=True) + jnp.exp2(sink2 - mx)
        ot = _dot_tn(vc, pt.astype(BF16)) / den
        for gi in range(0, G_C, 2):
            h = c * G_C + gi
            two = jnp.concatenate([ot[:, gi * w:(gi + 1) * w], ot[:, (gi + 1) * w:(gi + 2) * w]], axis=0)
            o_ref[:, h * HD_C:(h + 2) * HD_C] = two.T


def swa_prompt(p, sink_row):
    nblk = SEQ // WINDOW
    kvb = NH_C * HD_C // (2 * KVH_C * HD_C)
    return pl.pallas_call(
        _swa_prompt_kernel,
        grid=(BATCH, nblk),
        in_specs=[pl.BlockSpec((WINDOW, NH_C * HD_C), lambda b, i: (b * nblk + i, 0)),
                  pl.BlockSpec((WINDOW, 2 * KVH_C * HD_C), lambda b, i: (b * nblk + jnp.maximum(i - 1, 0), kvb)),
                  pl.BlockSpec((WINDOW, 2 * KVH_C * HD_C), lambda b, i: (b * nblk + i, kvb)),
                  pl.BlockSpec((1, 128), lambda b, i: (0, 0))],
        out_specs=pl.BlockSpec((WINDOW, NH_C * HD_C), lambda b, i: (b * nblk + i, 0)),
        out_shape=jax.ShapeDtypeStruct((NP, NH_C * HD_C), F32),
        compiler_params=_params("arbitrary", "arbitrary"),
        name="swa_prompt",
    )(p, p, p, sink_row)


def _swa_sample_kernel(q_ref, kn_ref, vn_ref, ck_ref, cv_ref, sink_ref, o_ref):
    w = WINDOW
    nk = w + 8
    nr = G_C * 8
    r = lax.broadcasted_iota(jnp.int32, (nr, nk), 0)
    cc = lax.broadcasted_iota(jnp.int32, (nr, nk), 1)
    di = w + r % 8 - cc
    ok = (di >= 0) & (di <= w) & (cc < w + DEC_SEQ)
    maskt = jnp.where(ok, 0.0, NEG_INF)
    rcol = lax.broadcasted_iota(jnp.int32, (nr, 1), 0)
    gcol = rcol // 8
    qdist = (w + rcol % 8).astype(F32)
    qlane = lax.broadcasted_iota(jnp.int32, (nr, HD_C), 1)
    sink = sink_ref[...]
    qpats, sink2s = [], []
    for c in range(KVH_C):
        heads = [c * G_C + gi for gi in range(G_C)]
        parts = [_bf16_parts(_swa_slope(h) * LOG2E) for h in heads]
        b = [_by_head(gcol, [pp[j] for pp in parts]) for j in range(3)]
        qpats.append(jnp.where(qlane == 0, b[0], jnp.where(qlane == 1, b[1], jnp.where(qlane == 2, b[2], 0.0))))
        slope = _by_head(gcol, [_swa_slope(h) for h in heads])
        sk = _by_head(gcol, [sink[:, h:h + 1] for h in heads])
        sink2s.append((sk + slope * qdist) * LOG2E)
    for sq in range(SWA_SEQ_PER_STEP):
        q = q_ref[sq] * (HD_C ** -0.5 * LOG2E)
        k_all = jnp.concatenate([ck_ref[sq], kn_ref[sq]], axis=0)
        v_all = jnp.concatenate([cv_ref[sq], vn_ref[sq]], axis=0)
        outs = []
        for c in range(KVH_C):
            ka, vc = _swa_kv_aug(k_all, v_all, c)
            qc = jnp.concatenate([q[:, (c * G_C + gi) * HD_C:(c * G_C + gi + 1) * HD_C] for gi in range(G_C)], axis=0)
            qa = jnp.concatenate([qc, qpats[c]], axis=1).astype(BF16)
            oc = _swa_softmax_pv(qa, ka, vc, maskt, sink2s[c])
            outs.extend(oc[gi * 8:(gi + 1) * 8, :] for gi in range(G_C))
        _store_heads(o_ref, outs, lead=(sq,))


def swa_sample(q, k_new, v_new, cache_k, cache_v, sink_row):
    n = SWA_SEQ_PER_STEP
    kw = KVH_C * HD_C
    blk = lambda i: (i, 0, 0)
    return pl.pallas_call(
        _swa_sample_kernel,
        grid=(DEC_BATCH // n,),
        in_specs=[pl.BlockSpec((n, 8, NH_C * HD_C), blk), pl.BlockSpec((n, 8, kw), blk), pl.BlockSpec((n, 8, kw), blk),
                  pl.BlockSpec((n, WINDOW, kw), blk), pl.BlockSpec((n, WINDOW, kw), blk),
                  pl.BlockSpec((1, 128), lambda i: (0, 0))],
        out_specs=pl.BlockSpec((n, 8, NH_C * HD_C), blk),
        out_shape=jax.ShapeDtypeStruct((DEC_BATCH, 8, NH_C * HD_C), F32),
        compiler_params=_params("arbitrary"),
        name="swa_sample",
    )(q, k_new, v_new, cache_k, cache_v, sink_row)


def _router_kernel(x_ref, wt_ref, bt_ref, meta_ref, wts_ref, seg_ref, tri_s, *, tm):
    @pl.when(pl.program_id(0) == 0)
    def _():
        rr = lax.broadcasted_iota(jnp.int32, (128, tm), 0)
        ccn = lax.broadcasted_iota(jnp.int32, (128, tm), 1)
        for ch in range(tm // 128):
            tri_s[ch * 128:(ch + 1) * 128, :] = (rr + ch * 128 < ccn).astype(BF16)

    lt = _dot_nt(wt_ref[...], x_ref[...], HIGHEST) + bt_ref[...]
    row = lax.broadcasted_iota(jnp.int32, lt.shape, 0)
    big = ROUTER_ROWS

    lg = jnp.where(row < N_GROUPS, lt, NEG_INF)
    mg = jnp.max(lg, axis=0, keepdims=True)
    eg = jnp.exp(lg - mg)
    pg = eg / jnp.sum(eg, axis=0, keepdims=True)
    pg1 = jnp.max(pg, axis=0, keepdims=True)
    g1 = jnp.min(jnp.where(pg == pg1, row, big), axis=0, keepdims=True)

    e = row - N_GROUPS
    ingroup = (e >= g1 * N_EXP) & (e < (g1 + 1) * N_EXP)
    le = jnp.where(ingroup, lt, NEG_INF)
    me = jnp.max(le, axis=0, keepdims=True)
    ee = jnp.exp(le - me)
    pe = ee / jnp.sum(ee, axis=0, keepdims=True)
    pe = jnp.where(ingroup, pe, NEG_INF)
    p1 = jnp.max(pe, axis=0, keepdims=True)
    i1 = jnp.min(jnp.where(pe == p1, row, big), axis=0, keepdims=True)
    pe2 = jnp.where(row == i1, NEG_INF, pe)
    p2 = jnp.max(pe2, axis=0, keepdims=True)
    i2 = jnp.min(jnp.where(pe2 == p2, row, big), axis=0, keepdims=True)
    tot = p1 + p2
    w1 = p1 / tot * pg1
    w2 = p2 / tot * pg1

    hit1 = row == i1
    hit2 = row == i2
    oh = jnp.where(hit1 | hit2, 1.0, 0.0)
    ohb = oh.astype(BF16)
    before = _dot(ohb, tri_s[...])
    cnt_col = jnp.sum(oh, axis=1, keepdims=True)
    pad_col = jnp.floor((cnt_col + 7.0) * 0.125) * 8.0
    r128 = lax.broadcasted_iota(jnp.int32, (128, 128), 0)
    c128 = lax.broadcasted_iota(jnp.int32, (128, 128), 1)
    off_col = jnp.dot((c128 < r128).astype(F32), pad_col + jnp.zeros((128, 128), F32),
                      precision=HIGHEST, preferred_element_type=F32)[:, 0:1]
    place = before + off_col
    pos1 = jnp.sum(jnp.where(hit1, place, 0.0), axis=0, keepdims=True)
    pos2 = jnp.sum(jnp.where(hit2, place, 0.0), axis=0, keepdims=True)
    cnt_row = _dot_nt(jnp.ones((8, tm), BF16), ohb)
    pad_row = jnp.floor((cnt_row + 7.0) * 0.125) * 8.0
    off_row = jnp.dot(pad_row, (r128 < c128).astype(F32), precision=HIGHEST, preferred_element_type=F32)

    r8 = lax.broadcasted_iota(jnp.int32, (8, tm), 0)
    meta_ref[...] = jnp.where(r8 == 0, pos1, jnp.where(r8 == 1, pos2, 0.0)).astype(I32)
    wts_ref[...] = jnp.where(r8 == 0, w1, jnp.where(r8 == 1, w2, 0.0))
    s8 = lax.broadcasted_iota(jnp.int32, (8, 128), 0)
    seg_ref[...] = jnp.where(s8 == 0, off_row, jnp.where(s8 == 1, cnt_row, 0.0)).astype(I32)


def moe_router(x, wt_router, bt_router, tm=MOE_TM):
    m = x.shape[0]
    nt = m // tm
    blk = lambda i: (i, 0, 0)
    return pl.pallas_call(
        functools.partial(_router_kernel, tm=tm),
        grid=(nt,),
        in_specs=[pl.BlockSpec((tm, D_MODEL), lambda i: (i, 0)),
                  pl.BlockSpec((ROUTER_ROWS, D_MODEL), lambda i: (0, 0)),
                  pl.BlockSpec((ROUTER_ROWS, 1), lambda i: (0, 0))],
        out_specs=[pl.BlockSpec((None, 8, tm), blk), pl.BlockSpec((None, 8, tm), blk),
                   pl.BlockSpec((None, 8, 128), blk)],
        out_shape=[jax.ShapeDtypeStruct((nt, 8, tm), I32), jax.ShapeDtypeStruct((nt, 8, tm), F32),
                   jax.ShapeDtypeStruct((nt, 8, 128), I32)],
        scratch_shapes=[pltpu.VMEM((tm, tm), BF16)],
        compiler_params=_params("arbitrary"),
        name="moe_router",
    )(x, wt_router, bt_router)


def _moe_kernel(x_ref, meta_hbm, wts_hbm, seg_hbm, wg_hbm, wu_hbm, wd_hbm, g_ref, b_ref, o_ref,
                xs, wbg, wbu, wbd, wsem, meta_s, wts_s, seg_s, msem, *, layer, tm):
    i = pl.program_id(0)
    nt = pl.num_programs(0)
    rc = MOE_RC

    def weight_copies(e, sl):
        return (pltpu.make_async_copy(wg_hbm.at[layer, e], wbg.at[sl], wsem.at[sl, 0]),
                pltpu.make_async_copy(wu_hbm.at[layer, e], wbu.at[sl], wsem.at[sl, 1]),
                pltpu.make_async_copy(wd_hbm.at[layer, e], wbd.at[sl], wsem.at[sl, 2]))

    meta_copies = (pltpu.make_async_copy(meta_hbm.at[i], meta_s, msem.at[0]),
                   pltpu.make_async_copy(wts_hbm.at[i], wts_s, msem.at[1]),
                   pltpu.make_async_copy(seg_hbm.at[i], seg_s, msem.at[2]))
    for cp in meta_copies:
        cp.start()

    @pl.when(i == 0)
    def _():
        for cp in weight_copies(0, 0):
            cp.start()
        xs[...] = jnp.zeros_like(xs)

    for cp in meta_copies:
        cp.wait()

    def gather(t, carry):
        rowv = x_ref[pl.ds(t, 1), :]
        xs[pl.ds(meta_s[0, t], 1), :] = rowv
        xs[pl.ds(meta_s[1, t], 1), :] = rowv
        return carry

    lax.fori_loop(0, tm, gather, 0, unroll=8)

    def expert(e, carry):
        sl = e % 2
        for cp in weight_copies(e, sl):
            cp.wait()

        @pl.when(e + 1 < N_EXPERTS)
        def _():
            for cp in weight_copies(e + 1, 1 - sl):
                cp.start()

        @pl.when((e + 1 == N_EXPERTS) & (i + 1 < nt))
        def _():
            for cp in weight_copies(0, 1 - sl):
                cp.start()

        off = seg_s[0, N_GROUPS + e]
        cnt = seg_s[1, N_GROUPS + e]
        wgb = wbg[sl].astype(BF16)
        wub = wbu[sl].astype(BF16)
        wdb = wbd[sl].astype(BF16)
        ridx = lax.broadcasted_iota(jnp.int32, (rc, 1), 0)

        def chunk(k, c2):
            row0 = pl.multiple_of(off + k * rc, 8)
            lhs = xs[pl.ds(row0, rc), :]
            lb = lhs.astype(BF16)
            hg = _dot(lb, wgb)
            hu = _dot(lb, wub)
            hh = hg / (1.0 + jnp.exp(-hg)) * hu
            out = _dot(hh.astype(BF16), wdb)
            xs[pl.ds(row0, rc), :] = jnp.where(ridx < cnt - k * rc, out, lhs)
            return c2

        lax.fori_loop(0, (cnt + rc - 1) // rc, chunk, 0)
        return carry

    lax.fori_loop(0, N_EXPERTS, expert, 0)

    def combine(t, carry):
        y = wts_s[0, t] * xs[pl.ds(meta_s[0, t], 1), :] + wts_s[1, t] * xs[pl.ds(meta_s[1, t], 1), :]
        o_ref[pl.ds(t, 1), :] = y
        return carry

    lax.fori_loop(0, tm, combine, 0, unroll=8)
    o_ref[...] = _layer_norm(ALPHA * x_ref[...] + o_ref[...], g_ref[...], b_ref[...])


def moe_ffn_ln(x, meta, wts, seg, wg, wu, wd, g, b, layer, tm=MOE_TM):
    m = x.shape[0]
    row = lambda i: (i, 0)
    fixed = lambda i: (0, 0)
    hbm = pl.BlockSpec(memory_space=pl.ANY)
    return pl.pallas_call(
        functools.partial(_moe_kernel, layer=layer, tm=tm),
        grid=(m // tm,),
        in_specs=[pl.BlockSpec((tm, D_MODEL), row), hbm, hbm, hbm, hbm, hbm, hbm,
                  pl.BlockSpec((1, D_MODEL), fixed), pl.BlockSpec((1, D_MODEL), fixed)],
        out_specs=pl.BlockSpec((tm, D_MODEL), row),
        out_shape=jax.ShapeDtypeStruct((m, D_MODEL), F32),
        scratch_shapes=[pltpu.VMEM((MOE_ROWS, D_MODEL), F32),
                        pltpu.VMEM((2, D_MODEL, D_FF), F32), pltpu.VMEM((2, D_MODEL, D_FF), F32),
                        pltpu.VMEM((2, D_FF, D_MODEL), F32), pltpu.SemaphoreType.DMA((2, 3)),
                        pltpu.SMEM((8, tm), I32), pltpu.SMEM((8, tm), F32), pltpu.SMEM((8, 128), I32),
                        pltpu.SemaphoreType.DMA((3,))],
        compiler_params=_params("arbitrary"),
        name="moe_ffn_ln",
    )(x, meta, wts, seg, wg, wu, wd, g.reshape(1, D_MODEL), b.reshape(1, D_MODEL))


def _pad_rows(a, rows):
    return jnp.pad(a, ((0, 0), (0, rows - a.shape[1]), (0, 0)))


def _mlstm_layer(x, w_in, b_gate, norm_g, w_out, c0, n0, m0):
    w = jnp.pad(w_in, ((0, 0), (0, M_IN_PAD - M_IN))).astype(BF16)
    p = matmul(x, w)
    bg = jnp.pad(b_gate, (0, 128 - 2 * NH_M)).reshape(1, 128)
    ng = norm_g.reshape(1, M_V)
    nc = SEQ // M_CHUNK
    hp, cp, np_, mp = mlstm(p, jnp.zeros((BATCH, NH_M, DK_M, DV_M), F32), jnp.zeros((BATCH, NH_M, DK_M), F32),
                            jnp.zeros((BATCH, NH_M), F32), bg, ng, BATCH, nc, M_CHUNK, M_CHUNK)
    ps = _pad_rows(p[NP:].reshape(DEC_BATCH, DEC_SEQ, M_IN_PAD), 8).reshape(DEC_BATCH * 8, M_IN_PAD)
    hs, cs, ns, ms = mlstm(ps, c0, n0, m0, bg, ng, DEC_BATCH, 1, 8, DEC_SEQ)
    hs = hs.reshape(DEC_BATCH, 8, M_V)[:, :DEC_SEQ].reshape(NS, M_V)
    a = jnp.concatenate([hp, hs], axis=0)
    states = (cp, np_, mp.reshape(BATCH, NH_M), cs, ns, ms.reshape(DEC_BATCH, NH_M))
    return a, w_out.astype(BF16), states


def _moba_sample_part(ps, cache_k, cache_v, page_rows):
    hq = NH_B * HD_B
    hk = KVH_B * HD_B
    slopes_h = jnp.exp2(-8.0 * jnp.arange(1, NH_B + 1, dtype=F32) / NH_B)
    q = (ps[:, :hq] * (HD_B ** -0.5)).reshape(DEC_BATCH, DEC_SEQ, KVH_B, G_B, HD_B)
    eye = jnp.eye(KVH_B, dtype=F32)
    q_aug = jnp.einsum("bscgd,ce->bcdesg", q, eye).reshape(DEC_BATCH, hk, KVH_B * DEC_SEQ * G_B)
    k_new = ps[:, hq:hq + hk].reshape(DEC_BATCH, DEC_SEQ, hk)
    v_new = ps[:, hq + hk:].reshape(DEC_BATCH, DEC_SEQ, hk)
    col = jnp.arange(KVH_B * DEC_SEQ * G_B)
    col_c, col_s, col_g = col // (DEC_SEQ * G_B), (col // G_B) % DEC_SEQ, col % G_B
    slope_col = (slopes_h[col_c * G_B + col_g] * LOG2E).reshape(1, -1)
    qs_col = col_s.astype(F32).reshape(1, -1)
    kpos = jnp.arange(PAST, dtype=F32)[:, None]
    bias = -slope_col * (float(PAST) + qs_col - kpos)
    ot = moba_sample(page_rows, q_aug, _pad_rows(k_new, 8), _pad_rows(v_new, 8), bias, slope_col, qs_col,
                     cache_k, cache_v)
    ot = ot.reshape(DEC_BATCH, KVH_B, HD_B, KVH_B, DEC_SEQ, G_B)
    os_ = jnp.einsum("bcdesg,ce->bscgd", ot, eye).reshape(NS, hq)
    return os_, k_new, v_new


def _moba_layer(x, w_in, w_out, cache_k, cache_v, page_rows):
    p = matmul(x, w_in.astype(BF16))
    hq = NH_B * HD_B
    hk = KVH_B * HD_B
    ac = _moba_aug_const()
    q_aug, k_aug, v_t = moba_select(p, ac)
    op = moba_flash(q_aug, k_aug, v_t)
    os_, k_new, v_new = _moba_sample_part(p[NP:], cache_k, cache_v, page_rows)
    a = jnp.concatenate([op, os_], axis=0)
    kv = (p[:NP, hq:hq + hk].reshape(BATCH, SEQ, KVH_B, HD_B), p[:NP, hq + hk:].reshape(BATCH, SEQ, KVH_B, HD_B),
          k_new.reshape(DEC_BATCH, DEC_SEQ, KVH_B, HD_B), v_new.reshape(DEC_BATCH, DEC_SEQ, KVH_B, HD_B))
    return a, w_out.astype(BF16), kv


def _swa_layer(x, w_in, sinks, w_out, cache_k, cache_v):
    p = matmul(x, w_in.astype(BF16))
    hq = NH_C * HD_C
    kw = KVH_C * HD_C
    sink_row = jnp.pad(sinks, (0, 128 - NH_C)).reshape(1, 128)
    op = swa_prompt(p, sink_row)
    ps = p[NP:]
    q = _pad_rows(ps[:, :hq].reshape(DEC_BATCH, DEC_SEQ, hq), 8)
    k_new = ps[:, hq:hq + kw].reshape(DEC_BATCH, DEC_SEQ, kw)
    v_new = ps[:, hq + kw:].reshape(DEC_BATCH, DEC_SEQ, kw)
    ck = cache_k.reshape(DEC_BATCH, WINDOW, kw)
    cv = cache_v.reshape(DEC_BATCH, WINDOW, kw)
    os_ = swa_sample(q, _pad_rows(k_new, 8), _pad_rows(v_new, 8), ck, cv, sink_row)[:, :DEC_SEQ].reshape(NS, hq)
    a = jnp.concatenate([op, os_], axis=0)
    pp = p[:NP].reshape(BATCH, SEQ, C_IN)[:, SEQ - WINDOW:]
    kv = (pp[..., hq:hq + kw].reshape(BATCH, WINDOW, KVH_C, HD_C), pp[..., hq + kw:].reshape(BATCH, WINDOW, KVH_C, HD_C),
          jnp.concatenate([ck, k_new], axis=1)[:, DEC_SEQ:].reshape(DEC_BATCH, WINDOW, KVH_C, HD_C),
          jnp.concatenate([cv, v_new], axis=1)[:, DEC_SEQ:].reshape(DEC_BATCH, WINDOW, KVH_C, HD_C))
    return a, w_out.astype(BF16), kv


def _moe_layer(x, w_group, b_group, w_router, b_router, w_gate, w_up, w_down, g, b, layer):
    pad = ROUTER_ROWS - N_GROUPS - N_EXPERTS
    wt = jnp.concatenate([w_group.T, w_router.T, jnp.zeros((pad, D_MODEL), F32)], axis=0)
    bt = jnp.concatenate([b_group, b_router, jnp.zeros((pad,), F32)]).reshape(ROUTER_ROWS, 1)
    meta, wts, seg = moe_router(x, wt, bt)
    return moe_ffn_ln(x, meta, wts, seg, w_gate, w_up, w_down, g, b, layer)


def kernel(x_prompt, x_sample, state_mlstm_C, state_mlstm_n, state_mlstm_m, cache_moba_k, cache_moba_v, cache_swa_k, cache_swa_v, page_table, mlstm_w_in, mlstm_b_gate, mlstm_norm_g, mlstm_w_out, moba_w_in, moba_w_out, swa_w_in, swa_sinks, swa_w_out, ln_mix_g, ln_mix_b, ln_ffn_g, ln_ffn_b, moe_w_group, moe_b_group, moe_w_router, moe_b_router, moe_w_gate, moe_w_up, moe_w_down):
    x = jnp.concatenate([x_prompt.reshape(NP, D_MODEL), x_sample.reshape(NS, D_MODEL)], axis=0)
    n_pool = cache_moba_k.shape[1]
    moba_k = cache_moba_k.reshape(-1, PAGE_SIZE, KVH_B * HD_B)
    moba_v = cache_moba_v.reshape(-1, PAGE_SIZE, KVH_B * HD_B)
    m_states, b_kv, c_kv = [], [], []
    for layer in range(DEPTH):
        kind, slot = layer % 3, layer // 3
        if kind == 0:
            a, w_out, st = _mlstm_layer(x, mlstm_w_in[slot], mlstm_b_gate[slot], mlstm_norm_g[slot], mlstm_w_out[slot],
                                        state_mlstm_C[slot], state_mlstm_n[slot], state_mlstm_m[slot])
            m_states.append(st)
        elif kind == 1:
            a, w_out, kv = _moba_layer(x, moba_w_in[slot], moba_w_out[slot], moba_k, moba_v,
                                       page_table + slot * n_pool)
            b_kv.append(kv)
        else:
            a, w_out, kv = _swa_layer(x, swa_w_in[slot], swa_sinks[slot], swa_w_out[slot], cache_swa_k[slot],
                                      cache_swa_v[slot])
            c_kv.append(kv)
        x = outproj_ln(a, w_out, x, ln_mix_g[layer], ln_mix_b[layer])
        x = _moe_layer(x, moe_w_group[layer], moe_b_group[layer], moe_w_router[layer], moe_b_router[layer],
                       moe_w_gate, moe_w_up, moe_w_down, ln_ffn_g[layer], ln_ffn_b[layer], layer)
    stack = lambda items, j: jnp.stack([it[j] for it in items])
    return (x[:NP].reshape(BATCH, SEQ, D_MODEL), x[NP:].reshape(DEC_BATCH, DEC_SEQ, D_MODEL),
            stack(m_states, 0), stack(m_states, 1), stack(m_states, 2),
            stack(m_states, 3), stack(m_states, 4), stack(m_states, 5),
            stack(b_kv, 0), stack(b_kv, 1), stack(b_kv, 2), stack(b_kv, 3),
            stack(c_kv, 0), stack(c_kv, 1), stack(c_kv, 2), stack(c_kv, 3))
```

```python
import functools

import jax
import jax.numpy as jnp
import numpy as np
from jax import lax
from jax.experimental import pallas as pl
from jax.experimental.pallas import tpu as pltpu

F32 = jnp.float32
BF16 = jnp.bfloat16
I32 = jnp.int32
HIGHEST = lax.Precision.HIGHEST

D_MODEL = 1024
BATCH = 2
SEQ = 8192
DEPTH = 4
DEC_BATCH = 128
DEC_SEQ = 4
PAGE_SIZE = 128
N_PAGES = 64
PAST = N_PAGES * PAGE_SIZE
NP = BATCH * SEQ
NS = DEC_BATCH * DEC_SEQ
NT = NP + NS

NH_M = 4
DK_M = 128
DV_M = 256
M_CHUNK = 128
M_QK = NH_M * DK_M
M_V = NH_M * DV_M
M_IN = 2 * M_QK + 2 * M_V + 2 * NH_M
M_IN_PAD = 3200
M_GATE_COL = 2 * M_QK + 2 * M_V

NH_B = 8
KVH_B = 4
G_B = 2
HD_B = 128
MOBA_BLOCK = 256
MOBA_TOPK = 3
NB_B = SEQ // MOBA_BLOCK
NB_S = PAST // MOBA_BLOCK
B_IN = (NH_B + 2 * KVH_B) * HD_B
SAMPLE_PAGES_PER_STEP = 16
AUG = 2 * HD_B
POS_LANE = NB_B

NH_C = 16
KVH_C = 2
G_C = 8
HD_C = 64
WINDOW = 128
C_IN = (NH_C + 2 * KVH_C) * HD_C
SWA_SEQ_PER_STEP = 8

N_GROUPS = 4
N_EXP = 8
N_EXPERTS = 32
D_FF = 256
ROUTER_ROWS = 128
MOE_TM = 1536
MOE_RC = 128
MOE_ROWS = 2 * MOE_TM + N_EXPERTS * 8 + MOE_RC

ALPHA = (2.0 * DEPTH) ** 0.25
EPS = 1e-5
LOG2E = 1.4426950408889634
NEG_INF = float("-inf")
MASKED = -1e30
VMEM_LIMIT = 56 * 1024 * 1024


def _params(*sem):
    return pltpu.CompilerParams(dimension_semantics=sem, vmem_limit_bytes=VMEM_LIMIT)


def _dot(a, b):
    return jnp.dot(a, b, preferred_element_type=F32)


def _dot_nt(a, b, precision=None):
    return lax.dot_general(a, b, (((1,), (1,)), ((), ())), precision=precision, preferred_element_type=F32)


def _dot_tn(a, b):
    return lax.dot_general(a, b, (((0,), (0,)), ((), ())), preferred_element_type=F32)


def _layer_norm(z, g, b):
    mu = jnp.mean(z, axis=-1, keepdims=True)
    zc = z - mu
    var = jnp.mean(zc * zc, axis=-1, keepdims=True)
    return zc * lax.rsqrt(var + EPS) * g + b


def _bf16_parts(x, n=3):
    parts = []
    for _ in range(n):
        bits = int(np.float32(x).view(np.uint32))
        rounded = ((bits + 0x7FFF + ((bits >> 16) & 1)) >> 16) << 16
        p = float(np.uint32(rounded & 0xFFFFFFFF).view(np.float32))
        parts.append(p)
        x = x - p
    return parts


def _mm_kernel(x_ref, w_ref, o_ref):
    o_ref[...] = _dot(x_ref[...].astype(BF16), w_ref[...])


def matmul(x, w_bf16, tm=512):
    m, k = x.shape
    n = w_bf16.shape[1]
    return pl.pallas_call(
        _mm_kernel,
        grid=(m // tm,),
        in_specs=[pl.BlockSpec((tm, k), lambda i: (i, 0)), pl.BlockSpec((k, n), lambda i: (0, 0))],
        out_specs=pl.BlockSpec((tm, n), lambda i: (i, 0)),
        out_shape=jax.ShapeDtypeStruct((m, n), F32),
        compiler_params=_params("arbitrary"),
        name="in_proj",
    )(x, w_bf16)


def _outproj_ln_kernel(a_ref, w_ref, r_ref, g_ref, b_ref, o_ref):
    y = _dot(a_ref[...].astype(BF16), w_ref[...])
    o_ref[...] = _layer_norm(ALPHA * r_ref[...] + y, g_ref[...], b_ref[...])


def outproj_ln(a, w_bf16, resid, g, b, tm=512):
    m, k = a.shape
    n = w_bf16.shape[1]
    row = lambda i: (i, 0)
    fixed = lambda i: (0, 0)
    return pl.pallas_call(
        _outproj_ln_kernel,
        grid=(m // tm,),
        in_specs=[pl.BlockSpec((tm, k), row), pl.BlockSpec((k, n), fixed), pl.BlockSpec((tm, n), row),
                  pl.BlockSpec((1, n), fixed), pl.BlockSpec((1, n), fixed)],
        out_specs=pl.BlockSpec((tm, n), row),
        out_shape=jax.ShapeDtypeStruct((m, n), F32),
        compiler_params=_params("arbitrary"),
        name="out_proj_ln",
    )(a, w_bf16, resid, g.reshape(1, n), b.reshape(1, n))


def _log_sigmoid(x):
    return jnp.minimum(x, 0.0) - jnp.log1p(jnp.exp(-jnp.abs(x)))


def _mlstm_kernel(p_ref, c0_ref, n0_ref, m0_ref, bg_ref, ng_ref,
                  hg_ref, co_ref, no_ref, mo_ref, c_s, n_s, m_s, *, L, L_real):
    c = pl.program_id(1)

    @pl.when(c == 0)
    def _():
        c_s[...] = c0_ref[...]
        n_s[...] = n0_ref[...]
        m_s[...] = m0_ref[...]

    gates = p_ref[:, M_GATE_COL:M_IN_PAD] + bg_ref[...]
    lane = lax.broadcasted_iota(jnp.int32, (L, 128), 1)
    x = jnp.where(lane < NH_M, gates, _log_sigmoid(gates))
    if L_real < L:
        tok = lax.broadcasted_iota(jnp.int32, (L, 128), 0)
        x = jnp.where(tok < L_real, x, jnp.where(lane < NH_M, -1e30, 0.0))
    r = lax.broadcasted_iota(jnp.int32, (L, L), 0)
    s = lax.broadcasted_iota(jnp.int32, (L, L), 1)
    causal = r >= s
    tri = causal.astype(F32)
    bcum = jnp.dot(tri, x, precision=HIGHEST, preferred_element_type=F32)
    eye8 = (lax.broadcasted_iota(jnp.int32, (8, 128), 0) == lax.broadcasted_iota(jnp.int32, (8, 128), 1)).astype(F32)
    x_rows = _dot_nt(eye8, x, HIGHEST)
    b_rows = _dot_nt(eye8, bcum, HIGHEST)

    for h in range(NH_M):
        li_row = x_rows[h:h + 1, :]
        li_col = x[:, h:h + 1]
        b_row = b_rows[NH_M + h:NH_M + h + 1, :]
        b_col = bcum[:, NH_M + h:NH_M + h + 1]
        m_old = m_s[:, h:h + 1]
        dm = jnp.where(causal, b_col - b_row + li_row, NEG_INF)
        g_col = b_col + m_old
        mq = jnp.maximum(g_col, jnp.max(dm, axis=1, keepdims=True))
        w_intra = jnp.exp(dm - mq)
        w_inter = jnp.exp(g_col - mq)
        q = p_ref[:, h * DK_M:(h + 1) * DK_M] * (DK_M ** -0.5)
        k = p_ref[:, M_QK + h * DK_M:M_QK + (h + 1) * DK_M]
        v = p_ref[:, 2 * M_QK + h * DV_M:2 * M_QK + (h + 1) * DV_M]
        qb = q.astype(BF16)
        kb = k.astype(BF16)
        cmat = c_s[h]
        nrow = n_s[h:h + 1, :]
        sc = _dot_nt(qb, kb) * w_intra
        num = _dot(sc.astype(BF16), v.astype(BF16)) + w_inter * _dot(qb, cmat.astype(BF16))
        den = jnp.sum(sc, axis=1, keepdims=True) + w_inter * jnp.sum(q * nrow, axis=1, keepdims=True)
        hh = num / jnp.maximum(jnp.abs(den), jnp.exp(-mq))
        bl = b_col[L - 1:L, :]
        lw = bl - b_col + li_col
        m_new = jnp.maximum(bl + m_old, jnp.max(lw, axis=0, keepdims=True))
        wl = jnp.exp(lw - m_new)
        dec = jnp.exp(bl + m_old - m_new)
        c_s[h] = dec * cmat + _dot_tn(kb, (wl * v).astype(BF16))
        n_s[h:h + 1, :] = dec * nrow + jnp.sum(wl * k, axis=0, keepdims=True)
        m_s[:, h:h + 1] = m_new
        mu = jnp.mean(hh, axis=1, keepdims=True)
        hc = hh - mu
        var = jnp.mean(hc * hc, axis=1, keepdims=True)
        hn = hc * lax.rsqrt(var + EPS) * ng_ref[:, h * DV_M:(h + 1) * DV_M]
        o = p_ref[:, 2 * M_QK + M_V + h * DV_M:2 * M_QK + M_V + (h + 1) * DV_M]
        hg_ref[:, h * DV_M:(h + 1) * DV_M] = hn / (1.0 + jnp.exp(-o))

    @pl.when(c == pl.num_programs(1) - 1)
    def _():
        co_ref[...] = c_s[...]
        no_ref[...] = n_s[...]
        mo_ref[...] = m_s[...]


def mlstm(p, c0, n0, m0, b_gate_row, norm_g_row, nb, nc, L, L_real):
    kern = functools.partial(_mlstm_kernel, L=L, L_real=L_real)
    per_b = lambda b, c: (b, 0, 0)
    return pl.pallas_call(
        kern,
        grid=(nb, nc),
        in_specs=[pl.BlockSpec((L, M_IN_PAD), lambda b, c: (b * nc + c, 0)),
                  pl.BlockSpec((None, NH_M, DK_M, DV_M), lambda b, c: (b, 0, 0, 0)),
                  pl.BlockSpec((None, NH_M, DK_M), per_b),
                  pl.BlockSpec((None, 1, NH_M), per_b),
                  pl.BlockSpec((1, 128), lambda b, c: (0, 0)),
                  pl.BlockSpec((1, M_V), lambda b, c: (0, 0))],
        out_specs=[pl.BlockSpec((L, M_V), lambda b, c: (b * nc + c, 0)),
                   pl.BlockSpec((None, NH_M, DK_M, DV_M), lambda b, c: (b, 0, 0, 0)),
                   pl.BlockSpec((None, NH_M, DK_M), per_b),
                   pl.BlockSpec((None, 1, NH_M), per_b)],
        out_shape=[jax.ShapeDtypeStruct((nb * nc * L, M_V), F32),
                   jax.ShapeDtypeStruct((nb, NH_M, DK_M, DV_M), F32),
                   jax.ShapeDtypeStruct((nb, NH_M, DK_M), F32),
                   jax.ShapeDtypeStruct((nb, 1, NH_M), F32)],
        scratch_shapes=[pltpu.VMEM((NH_M, DK_M, DV_M), F32), pltpu.VMEM((NH_M, DK_M), F32),
                        pltpu.VMEM((1, NH_M), F32)],
        compiler_params=_params("arbitrary", "arbitrary"),
        name=f"mlstm_L{L}",
    )(p, c0, n0, m0.reshape(nb, 1, NH_M), b_gate_row, norm_g_row)


def _topk_mask_lanes(g, k):
    lane = lax.broadcasted_iota(jnp.int32, g.shape, 1)
    n = g.shape[1]
    sel = jnp.zeros(g.shape, F32)
    for _ in range(k):
        mx = jnp.max(g, axis=1, keepdims=True)
        idx = jnp.min(jnp.where(g == mx, lane, n), axis=1, keepdims=True)
        hit = (lane == idx) & (mx > NEG_INF)
        sel = jnp.where(hit, 1.0, sel)
        g = jnp.where(lane == idx, NEG_INF, g)
    return sel


def _moba_select_kernel(q_ref, k_ref, v_ref, ac_ref, qa_ref, ka_ref, vt_ref, kmean_s, *, tq):
    i = pl.program_id(2)

    @pl.when(i == 0)
    def _():
        kmean_s[...] = jnp.zeros_like(kmean_s)
        kmean_s[0:NB_B, :] = jnp.sum(k_ref[...].reshape(NB_B, MOBA_BLOCK, HD_B), axis=1) * (1.0 / MOBA_BLOCK)
        rows = 1024
        lane = lax.broadcasted_iota(jnp.int32, (rows, HD_B), 1)
        rloc = lax.broadcasted_iota(jnp.int32, (rows, HD_B), 0)
        for ch in range(SEQ // rows):
            vt_ref[:, ch * rows:(ch + 1) * rows] = v_ref[ch * rows:(ch + 1) * rows, :].T.astype(BF16)
            row = rloc + ch * rows
            onehot = (lane == row // MOBA_BLOCK).astype(F32)
            offs = (row % MOBA_BLOCK).astype(F32)
            blks = (row // MOBA_BLOCK).astype(F32)
            pat = jnp.where(lane < NB_B, onehot,
                            jnp.where(lane < POS_LANE + 3, offs, jnp.where(lane < POS_LANE + 6, blks, 0.0)))
            ka_ref[ch * rows:(ch + 1) * rows, 0:HD_B] = k_ref[ch * rows:(ch + 1) * rows, :].astype(BF16)
            ka_ref[ch * rows:(ch + 1) * rows, HD_B:AUG] = pat.astype(BF16)

    km = kmean_s[...]
    lane = lax.broadcasted_iota(jnp.int32, (tq, HD_B), 1)
    own = (i * tq + lax.broadcasted_iota(jnp.int32, (tq, HD_B), 0)) // MOBA_BLOCK
    for g in range(G_B):
        q = q_ref[:, g * HD_B:(g + 1) * HD_B]
        gate = _dot_nt(q * (HD_B ** -0.5), km, HIGHEST)
        gate = jnp.where(lane < own, gate, NEG_INF)
        sel = _topk_mask_lanes(gate, MOBA_TOPK)
        sel = jnp.where(lane == own, 1.0, sel)
        aug = jnp.where(lane < NB_B, jnp.where(sel > 0.5, 0.0, MASKED), ac_ref[g, 0:1, :])
        qa_ref[:, g * AUG:g * AUG + HD_B] = (q * (HD_B ** -0.5 * LOG2E)).astype(BF16)
        qa_ref[:, g * AUG + HD_B:(g + 1) * AUG] = aug.astype(BF16)


def moba_select(p, aug_const, tq=1024):
    kern = functools.partial(_moba_select_kernel, tq=tq)
    nq = SEQ // tq
    per_bc = lambda b, c, i: (b, c, 0, 0)
    return pl.pallas_call(
        kern,
        grid=(BATCH, KVH_B, nq),
        in_specs=[pl.BlockSpec((tq, G_B * HD_B), lambda b, c, i: (b * nq + i, c)),
                  pl.BlockSpec((SEQ, HD_B), lambda b, c, i: (b, NH_B + c)),
                  pl.BlockSpec((SEQ, HD_B), lambda b, c, i: (b, NH_B + KVH_B + c)),
                  pl.BlockSpec((G_B, 8, 128), lambda b, c, i: (c, 0, 0))],
        out_specs=[pl.BlockSpec((tq, G_B * AUG), lambda b, c, i: (b * nq + i, c)),
                   pl.BlockSpec((None, None, SEQ, AUG), per_bc),
                   pl.BlockSpec((None, None, HD_B, SEQ), per_bc)],
        out_shape=[jax.ShapeDtypeStruct((NP, NH_B * AUG), BF16),
                   jax.ShapeDtypeStruct((BATCH, KVH_B, SEQ, AUG), BF16),
                   jax.ShapeDtypeStruct((BATCH, KVH_B, HD_B, SEQ), BF16)],
        scratch_shapes=[pltpu.VMEM((128, HD_B), F32)],
        compiler_params=_params("arbitrary", "arbitrary", "arbitrary"),
        name="moba_select",
    )(p, p, p, aug_const)


def _moba_flash_kernel(qa_ref, ka_ref, vt_ref, o_ref, s_buf):
    i = pl.program_id(2)
    bq = MOBA_BLOCK
    tk = 2 * MOBA_BLOCK
    nl = G_B * bq
    qa = jnp.concatenate([qa_ref[:, g * AUG:(g + 1) * AUG] for g in range(G_B)], axis=0)

    def scores(t, slot):
        off = pl.multiple_of(t * tk, tk)
        s_buf[slot] = _dot_nt(ka_ref[pl.ds(off, tk), :], qa)

    def softmax_pv(t, slot, last, carry):
        m, l, acc = carry
        st = s_buf[slot]
        if last:
            kpos = t * tk + lax.broadcasted_iota(jnp.int32, (tk, nl), 0)
            qpos = i * bq + lax.broadcasted_iota(jnp.int32, (tk, nl), 1) % bq
            st = jnp.where(kpos <= qpos, st, NEG_INF)
        m_new = jnp.maximum(m, jnp.max(st, axis=0, keepdims=True))
        pt = jnp.exp2(st - m_new)
        a = jnp.exp2(m - m_new)
        l = a * l + jnp.sum(pt, axis=0, keepdims=True)
        off = pl.multiple_of(t * tk, tk)
        acc = a * acc + _dot(vt_ref[:, pl.ds(off, tk)], pt.astype(BF16))
        return m_new, l, acc

    def pair(u, carry):
        scores(2 * u + 1, 1)
        carry = softmax_pv(2 * u, 0, False, carry)
        scores(2 * u + 2, 0)
        return softmax_pv(2 * u + 1, 1, False, carry)

    def tail_two(carry):
        t = 2 * npairs
        scores(t + 1, 1)
        return softmax_pv(t + 1, 1, True, softmax_pv(t, 0, False, carry))

    def tail_one(carry):
        return softmax_pv(2 * npairs, 0, True, carry)

    n_full = i // 2
    npairs = n_full // 2
    init = (jnp.full((1, nl), MASKED, F32), jnp.zeros((1, nl), F32), jnp.zeros((HD_B, nl), F32))
    scores(0, 0)
    carry = lax.fori_loop(0, npairs, pair, init)
    m, l, acc = lax.cond(n_full % 2 == 1, tail_two, tail_one, carry)
    out = acc / l
    for g in range(G_B):
        o_ref[:, g * HD_B:(g + 1) * HD_B] = out[:, g * bq:(g + 1) * bq].T


def moba_flash(q_aug, k_aug, v_t):
    nq = SEQ // MOBA_BLOCK
    per_bc = lambda b, c, i: (b, c, 0, 0)
    return pl.pallas_call(
        _moba_flash_kernel,
        grid=(BATCH, KVH_B, nq),
        in_specs=[pl.BlockSpec((MOBA_BLOCK, G_B * AUG), lambda b, c, i: (b * nq + i, c)),
                  pl.BlockSpec((None, None, SEQ, AUG), per_bc),
                  pl.BlockSpec((None, None, HD_B, SEQ), per_bc)],
        out_specs=pl.BlockSpec((MOBA_BLOCK, G_B * HD_B), lambda b, c, i: (b * nq + i, c)),
        out_shape=jax.ShapeDtypeStruct((NP, NH_B * HD_B), F32),
        scratch_shapes=[pltpu.VMEM((2, 2 * MOBA_BLOCK, G_B * MOBA_BLOCK), F32)],
        compiler_params=_params("arbitrary", "arbitrary", "arbitrary"),
        name="moba_flash",
    )(q_aug, k_aug, v_t)


def _moba_aug_const():
    c = np.zeros((NH_B, 8, 128), np.float32)
    for h in range(NH_B):
        beta = 2.0 ** (-8.0 * (h + 1) / NH_B) * LOG2E
        c[h, 0, POS_LANE:POS_LANE + 3] = _bf16_parts(beta)
        c[h, 0, POS_LANE + 3:POS_LANE + 6] = _bf16_parts(beta * MOBA_BLOCK)
    return jnp.asarray(c)


def _moba_sample_kernel(pt_ref, qa_ref, qat_ref, kn_ref, vn_ref, bias_ref, ck_ref, cv_ref,
                        o_ref, buf, sem, s_s, p_s, ksum_s, acc_s, inv_s):
    npg = SAMPLE_PAGES_PER_STEP
    w = KVH_B * HD_B
    nrow = KVH_B * DEC_SEQ * G_B

    def load_page(sl, pg):
        return jnp.concatenate([buf[sl, pg, pl.ds(c, PAGE_SIZE, stride=KVH_B), :] for c in range(KVH_B)], axis=1)

    b = pl.program_id(0)
    t = pl.program_id(1)
    nb = pl.num_programs(0)
    g = b * 8 + t
    slot = g % 2

    def page_copy(src_ref, page, sl, pg):
        return pltpu.make_async_copy(src_ref.at[page], buf.at[sl, pg], sem.at[sl])

    def start_fetch(bn, tn, sl):
        qn = tn % 4

        @pl.when(tn < 4)
        def _():
            for pg in range(npg):
                page_copy(ck_ref, pt_ref[bn, qn * npg + pg], sl, pg).start()

        @pl.when(tn >= 4)
        def _():
            for pg in range(npg):
                page_copy(cv_ref, pt_ref[bn, qn * npg + pg], sl, pg).start()

    @pl.when(g == 0)
    def _():
        start_fetch(b, t, slot)

    @pl.when(g + 1 < nb * 8)
    def _():
        tn = (t + 1) % 8
        bn = b + (t + 1) // 8
        start_fetch(bn, tn, 1 - slot)

    for pg in range(npg):
        page_copy(ck_ref, 0, slot, pg).wait()

    qab = (qa_ref[...] * LOG2E).astype(BF16)

    @pl.when(g == 0)
    def _():
        p_s[...] = jnp.zeros_like(p_s)

    @pl.when(t < 4)
    def _():
        for pg in range(npg):
            page = load_page(slot, pg)
            col0 = pl.multiple_of((t * npg + pg) * PAGE_SIZE, PAGE_SIZE)
            s_s[:, pl.ds(col0, PAGE_SIZE)] = _dot(page.astype(BF16), qab).T
            csum = jnp.sum(page, axis=0, keepdims=True)
            if pg % 2 == 0:
                prev = csum
            else:
                ksum_s[pl.ds(t * (npg // 2) + pg // 2, 1), :] = prev + csum

    @pl.when(t == 3)
    def _():
        qat = qat_ref[...]
        kmean = ksum_s[...] * (1.0 / MOBA_BLOCK)
        sel = _topk_mask_lanes(_dot_nt(qat, kmean, HIGHEST), MOBA_TOPK)
        kn = jnp.concatenate([kn_ref[...], jnp.zeros((PAGE_SIZE - 8, w), F32)], axis=0)
        s_s[0:nrow, PAST:PAST + PAGE_SIZE] = _dot_nt((qat * LOG2E).astype(BF16), kn.astype(BF16))
        selm = jnp.where(sel > 0.5, 0.0, NEG_INF)

        def logits(bk):
            if bk == NB_S:
                return s_s[0:nrow, PAST:PAST + PAGE_SIZE] + bias_ref[:, PAST:PAST + PAGE_SIZE]
            lo = bk * MOBA_BLOCK
            return s_s[0:nrow, lo:lo + MOBA_BLOCK] + bias_ref[:, lo:lo + MOBA_BLOCK] + selm[:, bk:bk + 1]

        macc = logits(0)
        for bk in range(1, NB_S):
            macc = jnp.maximum(macc, logits(bk))
        mx = jnp.maximum(jnp.max(macc, axis=1, keepdims=True), jnp.max(logits(NB_S), axis=1, keepdims=True))
        dacc = jnp.zeros((nrow, MOBA_BLOCK), F32)
        for bk in range(NB_S):
            pr = jnp.exp2(logits(bk) - mx)
            p_s[0:nrow, bk * MOBA_BLOCK:(bk + 1) * MOBA_BLOCK] = pr
            dacc = dacc + pr
        pr = jnp.exp2(logits(NB_S) - mx)
        p_s[0:nrow, PAST:PAST + PAGE_SIZE] = pr
        den = jnp.sum(dacc, axis=1, keepdims=True) + jnp.sum(pr, axis=1, keepdims=True)
        inv_s[...] = jnp.zeros_like(inv_s)
        inv_s[0:nrow, :] = jnp.broadcast_to(1.0 / den, (nrow, 128))
        inv_s[...] = inv_s[...].T

    @pl.when(t == 4)
    def _():
        vn = jnp.concatenate([vn_ref[...], jnp.zeros((PAGE_SIZE - 8, w), F32)], axis=0)
        acc_s[...] = _dot_tn(vn.astype(BF16), p_s[:, PAST:PAST + PAGE_SIZE].T.astype(BF16))

    @pl.when(t >= 4)
    def _():
        acc = acc_s[...]
        for pg in range(npg):
            col0 = pl.multiple_of(((t - 4) * npg + pg) * PAGE_SIZE, PAGE_SIZE)
            pt = p_s[:, pl.ds(col0, PAGE_SIZE)].T
            acc = acc + _dot_tn(load_page(slot, pg).astype(BF16), pt.astype(BF16))
        acc_s[...] = acc

    @pl.when(t == 7)
    def _():
        o_ref[...] = acc_s[:, 0:nrow] * inv_s[0:1, 0:nrow]


def moba_sample(page_table, q_aug, q_aug_t, k_new, v_new, bias, cache_k, cache_v):
    w = KVH_B * HD_B
    nrow = KVH_B * DEC_SEQ * G_B
    nkeys = PAST + PAGE_SIZE
    per_b3 = lambda b, t, pt: (b, 0, 0)
    fixed = lambda b, t, pt: (0, 0)
    grid_spec = pltpu.PrefetchScalarGridSpec(
        num_scalar_prefetch=1,
        grid=(DEC_BATCH, 8),
        in_specs=[pl.BlockSpec((None, w, 128), per_b3),
                  pl.BlockSpec((None, nrow, w), per_b3),
                  pl.BlockSpec((None, 8, w), per_b3),
                  pl.BlockSpec((None, 8, w), per_b3),
                  pl.BlockSpec((nrow, nkeys), fixed),
                  pl.BlockSpec(memory_space=pl.ANY),
                  pl.BlockSpec(memory_space=pl.ANY)],
        out_specs=pl.BlockSpec((None, w, nrow), per_b3),
        scratch_shapes=[pltpu.VMEM((2, SAMPLE_PAGES_PER_STEP, PAGE_SIZE * KVH_B, HD_B), F32),
                        pltpu.SemaphoreType.DMA((2,)),
                        pltpu.VMEM((128, nkeys), F32),
                        pltpu.VMEM((128, nkeys), F32),
                        pltpu.VMEM((NB_S, w), F32),
                        pltpu.VMEM((w, 128), F32),
                        pltpu.VMEM((128, 128), F32)],
    )
    return pl.pallas_call(
        _moba_sample_kernel,
        grid_spec=grid_spec,
        out_shape=jax.ShapeDtypeStruct((DEC_BATCH, w, nrow), F32),
        compiler_params=_params("arbitrary", "arbitrary"),
        name="moba_sample",
    )(page_table, q_aug, q_aug_t, k_new, v_new, bias, cache_k, cache_v)


def _swa_slope(h):
    return 2.0 ** (-8.0 * (h + 1) / NH_C)


def _by_head(gidx, values):
    out = values[-1]
    for g in range(len(values) - 2, -1, -1):
        out = jnp.where(gidx == g, values[g], out)
    return out


def _swa_kv_aug(k_all, v_all, c):
    nk = k_all.shape[0]
    lane = lax.broadcasted_iota(jnp.int32, (nk, HD_C), 1)
    kidx = lax.broadcasted_iota(jnp.int32, (nk, HD_C), 0).astype(F32)
    kpat = jnp.where(lane < 3, kidx, 0.0)
    ka = jnp.concatenate([k_all[:, c * HD_C:(c + 1) * HD_C], kpat], axis=1).astype(BF16)
    return ka, v_all[:, c * HD_C:(c + 1) * HD_C].astype(BF16)


def _swa_softmax_pv(qa, ka, vc, maskt, sink2):
    s2 = _dot_nt(qa, ka) + maskt
    mx = jnp.maximum(jnp.max(s2, axis=1, keepdims=True), sink2)
    pr = jnp.exp2(s2 - mx)
    den = jnp.sum(pr, axis=1, keepdims=True) + jnp.exp2(sink2 - mx)
    return _dot(pr.astype(BF16), vc) / den


def _store_heads(o_ref, outs, lead=()):
    for j in range(0, NH_C, 2):
        o_ref[lead + (slice(None), slice(j * HD_C, (j + 2) * HD_C))] = jnp.concatenate(outs[j:j + 2], axis=1)


def _swa_prompt_kernel(q_ref, kvp_ref, kvc_ref, sink_ref, o_ref):
    i = pl.program_id(1)
    w = WINDOW
    kw = KVH_C * HD_C
    k_all = jnp.concatenate([kvp_ref[:, 0:kw], kvc_ref[:, 0:kw]], axis=0)
    v_all = jnp.concatenate([kvp_ref[:, kw:2 * kw], kvc_ref[:, kw:2 * kw]], axis=0)
    kk = lax.broadcasted_iota(jnp.int32, (2 * w, w), 0)
    qq = lax.broadcasted_iota(jnp.int32, (2 * w, w), 1)
    di = qq - kk + w
    ok = (di >= 0) & (di <= w) & ((kk >= w) | (i > 0))
    mask1 = jnp.where(ok, 0.0, NEG_INF)
    maskt = jnp.concatenate([mask1] * G_C, axis=1)
    qdist = (lax.broadcasted_iota(jnp.int32, (1, w), 1) + w).astype(F32)
    qlane = lax.broadcasted_iota(jnp.int32, (w, HD_C), 1)
    sink = sink_ref[...]
    for c in range(KVH_C):
        ka, vc = _swa_kv_aug(k_all, v_all, c)
        qas, sinks = [], []
        for gi in range(G_C):
            h = c * G_C + gi
            b1, b2, b3 = _bf16_parts(_swa_slope(h) * LOG2E)
            qpat = jnp.where(qlane == 0, b1, jnp.where(qlane == 1, b2, jnp.where(qlane == 2, b3, 0.0)))
            qa = jnp.concatenate([q_ref[:, h * HD_C:(h + 1) * HD_C] * (HD_C ** -0.5 * LOG2E), qpat], axis=1)
            qas.append(qa.astype(BF16))
            sinks.append((sink[:, h:h + 1] + _swa_slope(h) * qdist) * LOG2E)
        sink2 = jnp.concatenate(sinks, axis=1)
        st = _dot_nt(ka, jnp.concatenate(qas, axis=0)) + maskt
        mx = jnp.maximum(jnp.max(st, axis=0, keepdims=True), sink2)
        pt = jnp.exp2(st - mx)
        den = jnp.sum(pt, axis=0, keepdims=True) + jnp.exp2(sink2 - mx)
        ot = _dot_tn(vc, pt.astype(BF16)) / den
        for gi in range(0, G_C, 2):
            h = c * G_C + gi
            two = jnp.concatenate([ot[:, gi * w:(gi + 1) * w], ot[:, (gi + 1) * w:(gi + 2) * w]], axis=0)
            o_ref[:, h * HD_C:(h + 2) * HD_C] = two.T


def swa_prompt(p, sink_row):
    nblk = SEQ // WINDOW
    kvb = NH_C * HD_C // (2 * KVH_C * HD_C)
    return pl.pallas_call(
        _swa_prompt_kernel,
        grid=(BATCH, nblk),
        in_specs=[pl.BlockSpec((WINDOW, NH_C * HD_C), lambda b, i: (b * nblk + i, 0)),
                  pl.BlockSpec((WINDOW, 2 * KVH_C * HD_C), lambda b, i: (b * nblk + jnp.maximum(i - 1, 0), kvb)),
                  pl.BlockSpec((WINDOW, 2 * KVH_C * HD_C), lambda b, i: (b * nblk + i, kvb)),
                  pl.BlockSpec((1, 128), lambda b, i: (0, 0))],
        out_specs=pl.BlockSpec((WINDOW, NH_C * HD_C), lambda b, i: (b * nblk + i, 0)),
        out_shape=jax.ShapeDtypeStruct((NP, NH_C * HD_C), F32),
        compiler_params=_params("arbitrary", "arbitrary"),
        name="swa_prompt",
    )(p, p, p, sink_row)


def _swa_sample_kernel(q_ref, kn_ref, vn_ref, ck_ref, cv_ref, sink_ref, o_ref):
    w = WINDOW
    nk = w + 8
    nr = G_C * 8
    r = lax.broadcasted_iota(jnp.int32, (nr, nk), 0)
    cc = lax.broadcasted_iota(jnp.int32, (nr, nk), 1)
    di = w + r % 8 - cc
    ok = (di >= 0) & (di <= w) & (cc < w + DEC_SEQ)
    maskt = jnp.where(ok, 0.0, NEG_INF)
    rcol = lax.broadcasted_iota(jnp.int32, (nr, 1), 0)
    gcol = rcol // 8
    qdist = (w + rcol % 8).astype(F32)
    qlane = lax.broadcasted_iota(jnp.int32, (nr, HD_C), 1)
    sink = sink_ref[...]
    qpats, sink2s = [], []
    for c in range(KVH_C):
        heads = [c * G_C + gi for gi in range(G_C)]
        parts = [_bf16_parts(_swa_slope(h) * LOG2E) for h in heads]
        b = [_by_head(gcol, [pp[j] for pp in parts]) for j in range(3)]
        qpats.append(jnp.where(qlane == 0, b[0], jnp.where(qlane == 1, b[1], jnp.where(qlane == 2, b[2], 0.0))))
        slope = _by_head(gcol, [_swa_slope(h) for h in heads])
        sk = _by_head(gcol, [sink[:, h:h + 1] for h in heads])
        sink2s.append((sk + slope * qdist) * LOG2E)
    for sq in range(SWA_SEQ_PER_STEP):
        q = q_ref[sq] * (HD_C ** -0.5 * LOG2E)
        k_all = jnp.concatenate([ck_ref[sq], kn_ref[sq]], axis=0)
        v_all = jnp.concatenate([cv_ref[sq], vn_ref[sq]], axis=0)
        outs = []
        for c in range(KVH_C):
            ka, vc = _swa_kv_aug(k_all, v_all, c)
            qc = jnp.concatenate([q[:, (c * G_C + gi) * HD_C:(c * G_C + gi + 1) * HD_C] for gi in range(G_C)], axis=0)
            qa = jnp.concatenate([qc, qpats[c]], axis=1).astype(BF16)
            oc = _swa_softmax_pv(qa, ka, vc, maskt, sink2s[c])
            outs.extend(oc[gi * 8:(gi + 1) * 8, :] for gi in range(G_C))
        _store_heads(o_ref, outs, lead=(sq,))


def swa_sample(q, k_new, v_new, cache_k, cache_v, sink_row):
    n = SWA_SEQ_PER_STEP
    kw = KVH_C * HD_C
    blk = lambda i: (i, 0, 0)
    return pl.pallas_call(
        _swa_sample_kernel,
        grid=(DEC_BATCH // n,),
        in_specs=[pl.BlockSpec((n, 8, NH_C * HD_C), blk), pl.BlockSpec((n, 8, kw), blk), pl.BlockSpec((n, 8, kw), blk),
                  pl.BlockSpec((n, WINDOW, kw), blk), pl.BlockSpec((n, WINDOW, kw), blk),
                  pl.BlockSpec((1, 128), lambda i: (0, 0))],
        out_specs=pl.BlockSpec((n, 8, NH_C * HD_C), blk),
        out_shape=jax.ShapeDtypeStruct((DEC_BATCH, 8, NH_C * HD_C), F32),
        compiler_params=_params("arbitrary"),
        name="swa_sample",
    )(q, k_new, v_new, cache_k, cache_v, sink_row)


def _router_kernel(x_ref, wt_ref, bt_ref, meta_ref, wts_ref, seg_ref, tri_s, *, tm):
    @pl.when(pl.program_id(0) == 0)
    def _():
        rr = lax.broadcasted_iota(jnp.int32, (128, tm), 0)
        ccn = lax.broadcasted_iota(jnp.int32, (128, tm), 1)
        for ch in range(tm // 128):
            tri_s[ch * 128:(ch + 1) * 128, :] = (rr + ch * 128 < ccn).astype(BF16)

    lt = _dot_nt(wt_ref[...], x_ref[...], HIGHEST) + bt_ref[...]
    row = lax.broadcasted_iota(jnp.int32, lt.shape, 0)
    big = ROUTER_ROWS

    lg = jnp.where(row < N_GROUPS, lt, NEG_INF)
    mg = jnp.max(lg, axis=0, keepdims=True)
    eg = jnp.exp(lg - mg)
    pg = eg / jnp.sum(eg, axis=0, keepdims=True)
    pg1 = jnp.max(pg, axis=0, keepdims=True)
    g1 = jnp.min(jnp.where(pg == pg1, row, big), axis=0, keepdims=True)

    e = row - N_GROUPS
    ingroup = (e >= g1 * N_EXP) & (e < (g1 + 1) * N_EXP)
    le = jnp.where(ingroup, lt, NEG_INF)
    me = jnp.max(le, axis=0, keepdims=True)
    ee = jnp.exp(le - me)
    pe = ee / jnp.sum(ee, axis=0, keepdims=True)
    pe = jnp.where(ingroup, pe, NEG_INF)
    p1 = jnp.max(pe, axis=0, keepdims=True)
    i1 = jnp.min(jnp.where(pe == p1, row, big), axis=0, keepdims=True)
    pe2 = jnp.where(row == i1, NEG_INF, pe)
    p2 = jnp.max(pe2, axis=0, keepdims=True)
    i2 = jnp.min(jnp.where(pe2 == p2, row, big), axis=0, keepdims=True)
    tot = p1 + p2
    w1 = p1 / tot * pg1
    w2 = p2 / tot * pg1

    hit1 = row == i1
    hit2 = row == i2
    oh = jnp.where(hit1 | hit2, 1.0, 0.0)
    ohb = oh.astype(BF16)
    before = _dot(ohb, tri_s[...])
    cnt_col = jnp.sum(oh, axis=1, keepdims=True)
    pad_col = jnp.floor((cnt_col + 7.0) * 0.125) * 8.0
    r128 = lax.broadcasted_iota(jnp.int32, (128, 128), 0)
    c128 = lax.broadcasted_iota(jnp.int32, (128, 128), 1)
    off_col = jnp.dot((c128 < r128).astype(F32), pad_col + jnp.zeros((128, 128), F32),
                      precision=HIGHEST, preferred_element_type=F32)[:, 0:1]
    place = before + off_col
    pos1 = jnp.sum(jnp.where(hit1, place, 0.0), axis=0, keepdims=True)
    pos2 = jnp.sum(jnp.where(hit2, place, 0.0), axis=0, keepdims=True)
    cnt_row = _dot_nt(jnp.ones((8, tm), BF16), ohb)
    pad_row = jnp.floor((cnt_row + 7.0) * 0.125) * 8.0
    off_row = jnp.dot(pad_row, (r128 < c128).astype(F32), precision=HIGHEST, preferred_element_type=F32)

    r8 = lax.broadcasted_iota(jnp.int32, (8, tm), 0)
    meta_ref[...] = jnp.where(r8 == 0, pos1, jnp.where(r8 == 1, pos2, 0.0)).astype(I32)
    wts_ref[...] = jnp.where(r8 == 0, w1, jnp.where(r8 == 1, w2, 0.0))
    s8 = lax.broadcasted_iota(jnp.int32, (8, 128), 0)
    seg_ref[...] = jnp.where(s8 == 0, off_row, jnp.where(s8 == 1, cnt_row, 0.0)).astype(I32)


def moe_router(x, wt_router, bt_router, tm=MOE_TM):
    m = x.shape[0]
    nt = m // tm
    blk = lambda i: (i, 0, 0)
    return pl.pallas_call(
        functools.partial(_router_kernel, tm=tm),
        grid=(nt,),
        in_specs=[pl.BlockSpec((tm, D_MODEL), lambda i: (i, 0)),
                  pl.BlockSpec((ROUTER_ROWS, D_MODEL), lambda i: (0, 0)),
                  pl.BlockSpec((ROUTER_ROWS, 1), lambda i: (0, 0))],
        out_specs=[pl.BlockSpec((None, 8, tm), blk), pl.BlockSpec((None, 8, tm), blk),
                   pl.BlockSpec((None, 8, 128), blk)],
        out_shape=[jax.ShapeDtypeStruct((nt, 8, tm), I32), jax.ShapeDtypeStruct((nt, 8, tm), F32),
                   jax.ShapeDtypeStruct((nt, 8, 128), I32)],
        scratch_shapes=[pltpu.VMEM((tm, tm), BF16)],
        compiler_params=_params("arbitrary"),
        name="moe_router",
    )(x, wt_router, bt_router)


def _moe_kernel(x_ref, meta_hbm, wts_hbm, seg_hbm, wg_hbm, wu_hbm, wd_hbm, g_ref, b_ref, o_ref,
                xs, wbg, wbu, wbd, wsem, meta_s, wts_s, seg_s, msem, *, layer, tm):
    i = pl.program_id(0)
    nt = pl.num_programs(0)
    rc = MOE_RC

    def weight_copies(e, sl):
        return (pltpu.make_async_copy(wg_hbm.at[layer, e], wbg.at[sl], wsem.at[sl, 0]),
                pltpu.make_async_copy(wu_hbm.at[layer, e], wbu.at[sl], wsem.at[sl, 1]),
                pltpu.make_async_copy(wd_hbm.at[layer, e], wbd.at[sl], wsem.at[sl, 2]))

    meta_copies = (pltpu.make_async_copy(meta_hbm.at[i], meta_s, msem.at[0]),
                   pltpu.make_async_copy(wts_hbm.at[i], wts_s, msem.at[1]),
                   pltpu.make_async_copy(seg_hbm.at[i], seg_s, msem.at[2]))
    for cp in meta_copies:
        cp.start()

    @pl.when(i == 0)
    def _():
        for cp in weight_copies(0, 0):
            cp.start()
        xs[...] = jnp.zeros_like(xs)

    for cp in meta_copies:
        cp.wait()

    def gather(t, carry):
        rowv = x_ref[pl.ds(t, 1), :]
        xs[pl.ds(meta_s[0, t], 1), :] = rowv
        xs[pl.ds(meta_s[1, t], 1), :] = rowv
        return carry

    lax.fori_loop(0, tm, gather, 0, unroll=8)

    def expert(e, carry):
        sl = e % 2
        for cp in weight_copies(e, sl):
            cp.wait()

        @pl.when(e + 1 < N_EXPERTS)
        def _():
            for cp in weight_copies(e + 1, 1 - sl):
                cp.start()

        @pl.when((e + 1 == N_EXPERTS) & (i + 1 < nt))
        def _():
            for cp in weight_copies(0, 1 - sl):
                cp.start()

        off = seg_s[0, N_GROUPS + e]
        cnt = seg_s[1, N_GROUPS + e]
        wgb = wbg[sl]
        wub = wbu[sl]
        wdb = wbd[sl]
        ridx = lax.broadcasted_iota(jnp.int32, (rc, 1), 0)

        def chunk(k, c2):
            row0 = pl.multiple_of(off + k * rc, 8)
            lhs = xs[pl.ds(row0, rc), :]
            lb = lhs.astype(BF16)
            hg = _dot(lb, wgb)
            hu = _dot(lb, wub)
            hh = hg / (1.0 + jnp.exp(-hg)) * hu
            out = _dot(hh.astype(BF16), wdb)
            xs[pl.ds(row0, rc), :] = jnp.where(ridx < cnt - k * rc, out, lhs)
            return c2

        lax.fori_loop(0, (cnt + rc - 1) // rc, chunk, 0)
        return carry

    lax.fori_loop(0, N_EXPERTS, expert, 0)

    def combine(t, carry):
        y = wts_s[0, t] * xs[pl.ds(meta_s[0, t], 1), :] + wts_s[1, t] * xs[pl.ds(meta_s[1, t], 1), :]
        o_ref[pl.ds(t, 1), :] = y
        return carry

    lax.fori_loop(0, tm, combine, 0, unroll=8)
    o_ref[...] = _layer_norm(ALPHA * x_ref[...] + o_ref[...], g_ref[...], b_ref[...])


def moe_ffn_ln(x, meta, wts, seg, wg, wu, wd, g, b, layer, tm=MOE_TM):
    m = x.shape[0]
    row = lambda i: (i, 0)
    fixed = lambda i: (0, 0)
    hbm = pl.BlockSpec(memory_space=pl.ANY)
    return pl.pallas_call(
        functools.partial(_moe_kernel, layer=layer, tm=tm),
        grid=(m // tm,),
        in_specs=[pl.BlockSpec((tm, D_MODEL), row), hbm, hbm, hbm, hbm, hbm, hbm,
                  pl.BlockSpec((1, D_MODEL), fixed), pl.BlockSpec((1, D_MODEL), fixed)],
        out_specs=pl.BlockSpec((tm, D_MODEL), row),
        out_shape=jax.ShapeDtypeStruct((m, D_MODEL), F32),
        scratch_shapes=[pltpu.VMEM((MOE_ROWS, D_MODEL), F32),
                        pltpu.VMEM((2, D_MODEL, D_FF), BF16), pltpu.VMEM((2, D_MODEL, D_FF), BF16),
                        pltpu.VMEM((2, D_FF, D_MODEL), BF16), pltpu.SemaphoreType.DMA((2, 3)),
                        pltpu.SMEM((8, tm), I32), pltpu.SMEM((8, tm), F32), pltpu.SMEM((8, 128), I32),
                        pltpu.SemaphoreType.DMA((3,))],
        compiler_params=_params("arbitrary"),
        name="moe_ffn_ln",
    )(x, meta, wts, seg, wg, wu, wd, g.reshape(1, D_MODEL), b.reshape(1, D_MODEL))


def _pad_rows(a, rows):
    return jnp.pad(a, ((0, 0), (0, rows - a.shape[1]), (0, 0)))


def _mlstm_layer(x, w_in, b_gate, norm_g, w_out, c0, n0, m0):
    w = jnp.pad(w_in, ((0, 0), (0, M_IN_PAD - M_IN))).astype(BF16)
    p = matmul(x, w)
    bg = jnp.pad(b_gate, (0, 128 - 2 * NH_M)).reshape(1, 128)
    ng = norm_g.reshape(1, M_V)
    nc = SEQ // M_CHUNK
    hp, cp, np_, mp = mlstm(p, jnp.zeros((BATCH, NH_M, DK_M, DV_M), F32), jnp.zeros((BATCH, NH_M, DK_M), F32),
                            jnp.zeros((BATCH, NH_M), F32), bg, ng, BATCH, nc, M_CHUNK, M_CHUNK)
    ps = _pad_rows(p[NP:].reshape(DEC_BATCH, DEC_SEQ, M_IN_PAD), 8).reshape(DEC_BATCH * 8, M_IN_PAD)
    hs, cs, ns, ms = mlstm(ps, c0, n0, m0, bg, ng, DEC_BATCH, 1, 8, DEC_SEQ)
    hs = hs.reshape(DEC_BATCH, 8, M_V)[:, :DEC_SEQ].reshape(NS, M_V)
    a = jnp.concatenate([hp, hs], axis=0)
    states = (cp, np_, mp.reshape(BATCH, NH_M), cs, ns, ms.reshape(DEC_BATCH, NH_M))
    return a, w_out.astype(BF16), states


def _moba_sample_part(ps, cache_k, cache_v, page_rows):
    hq = NH_B * HD_B
    hk = KVH_B * HD_B
    slopes_h = jnp.exp2(-8.0 * jnp.arange(1, NH_B + 1, dtype=F32) / NH_B)
    q = (ps[:, :hq] * (HD_B ** -0.5)).reshape(DEC_BATCH, DEC_SEQ, KVH_B, G_B, HD_B)
    eye = jnp.eye(KVH_B, dtype=F32)
    nrow = KVH_B * DEC_SEQ * G_B
    q_aug_t = jnp.einsum("bscgd,ce->besgcd", q, eye).reshape(DEC_BATCH, nrow, hk)
    q_aug = jnp.pad(jnp.swapaxes(q_aug_t, 1, 2), ((0, 0), (0, 0), (0, 128 - nrow)))
    k_new = ps[:, hq:hq + hk].reshape(DEC_BATCH, DEC_SEQ, hk)
    v_new = ps[:, hq + hk:].reshape(DEC_BATCH, DEC_SEQ, hk)
    row = jnp.arange(nrow)
    row_c, row_s, row_g = row // (DEC_SEQ * G_B), (row // G_B) % DEC_SEQ, row % G_B
    slope_row = (slopes_h[row_c * G_B + row_g] * LOG2E)[:, None]
    qs_row = row_s.astype(F32)[:, None]
    past_bias = -slope_row * (float(PAST) + qs_row - jnp.arange(PAST, dtype=F32)[None, :])
    tok = jnp.arange(PAGE_SIZE, dtype=F32)[None, :]
    own_bias = jnp.where((tok <= qs_row) & (tok < float(DEC_SEQ)), -slope_row * (qs_row - tok), NEG_INF)
    bias = jnp.concatenate([past_bias, own_bias], axis=1)
    ot = moba_sample(page_rows, q_aug, q_aug_t, _pad_rows(k_new, 8), _pad_rows(v_new, 8), bias, cache_k, cache_v)
    ot = ot.reshape(DEC_BATCH, KVH_B, HD_B, KVH_B, DEC_SEQ, G_B)
    os_ = jnp.einsum("bcdesg,ce->bscgd", ot, eye).reshape(NS, hq)
    return os_, k_new, v_new


def _moba_layer(x, w_in, w_out, cache_k, cache_v, page_rows):
    p = matmul(x, w_in.astype(BF16))
    hq = NH_B * HD_B
    hk = KVH_B * HD_B
    ac = _moba_aug_const()
    q_aug, k_aug, v_t = moba_select(p, ac)
    op = moba_flash(q_aug, k_aug, v_t)
    os_, k_new, v_new = _moba_sample_part(p[NP:], cache_k, cache_v, page_rows)
    a = jnp.concatenate([op, os_], axis=0)
    kv = (p[:NP, hq:hq + hk].reshape(BATCH, SEQ, KVH_B, HD_B), p[:NP, hq + hk:].reshape(BATCH, SEQ, KVH_B, HD_B),
          k_new.reshape(DEC_BATCH, DEC_SEQ, KVH_B, HD_B), v_new.reshape(DEC_BATCH, DEC_SEQ, KVH_B, HD_B))
    return a, w_out.astype(BF16), kv


def _swa_layer(x, w_in, sinks, w_out, cache_k, cache_v):
    p = matmul(x, w_in.astype(BF16))
    hq = NH_C * HD_C
    kw = KVH_C * HD_C
    sink_row = jnp.pad(sinks, (0, 128 - NH_C)).reshape(1, 128)
    op = swa_prompt(p, sink_row)
    ps = p[NP:]
    q = _pad_rows(ps[:, :hq].reshape(DEC_BATCH, DEC_SEQ, hq), 8)
    k_new = ps[:, hq:hq + kw].reshape(DEC_BATCH, DEC_SEQ, kw)
    v_new = ps[:, hq + kw:].reshape(DEC_BATCH, DEC_SEQ, kw)
    ck = cache_k.reshape(DEC_BATCH, WINDOW, kw)
    cv = cache_v.reshape(DEC_BATCH, WINDOW, kw)
    os_ = swa_sample(q, _pad_rows(k_new, 8), _pad_rows(v_new, 8), ck, cv, sink_row)[:, :DEC_SEQ].reshape(NS, hq)
    a = jnp.concatenate([op, os_], axis=0)
    pp = p[:NP].reshape(BATCH, SEQ, C_IN)[:, SEQ - WINDOW:]
    kv = (pp[..., hq:hq + kw].reshape(BATCH, WINDOW, KVH_C, HD_C), pp[..., hq + kw:].reshape(BATCH, WINDOW, KVH_C, HD_C),
          jnp.concatenate([ck, k_new], axis=1)[:, DEC_SEQ:].reshape(DEC_BATCH, WINDOW, KVH_C, HD_C),
          jnp.concatenate([cv, v_new], axis=1)[:, DEC_SEQ:].reshape(DEC_BATCH, WINDOW, KVH_C, HD_C))
    return a, w_out.astype(BF16), kv


def _moe_layer(x, w_group, b_group, w_router, b_router, w_gate, w_up, w_down, g, b, layer):
    pad = ROUTER_ROWS - N_GROUPS - N_EXPERTS
    wt = jnp.concatenate([w_group.T, w_router.T, jnp.zeros((pad, D_MODEL), F32)], axis=0)
    bt = jnp.concatenate([b_group, b_router, jnp.zeros((pad,), F32)]).reshape(ROUTER_ROWS, 1)
    meta, wts, seg = moe_router(x, wt, bt)
    return moe_ffn_ln(x, meta, wts, seg, w_gate, w_up, w_down, g, b, layer)


def kernel(x_prompt, x_sample, state_mlstm_C, state_mlstm_n, state_mlstm_m, cache_moba_k, cache_moba_v, cache_swa_k, cache_swa_v, page_table, mlstm_w_in, mlstm_b_gate, mlstm_norm_g, mlstm_w_out, moba_w_in, moba_w_out, swa_w_in, swa_sinks, swa_w_out, ln_mix_g, ln_mix_b, ln_ffn_g, ln_ffn_b, moe_w_group, moe_b_group, moe_w_router, moe_b_router, moe_w_gate, moe_w_up, moe_w_down):
    x = jnp.concatenate([x_prompt.reshape(NP, D_MODEL), x_sample.reshape(NS, D_MODEL)], axis=0)
    n_pool = cache_moba_k.shape[1]
    moba_k = cache_moba_k.reshape(-1, PAGE_SIZE * KVH_B, HD_B)
    moba_v = cache_moba_v.reshape(-1, PAGE_SIZE * KVH_B, HD_B)
    wg_all, wu_all, wd_all = moe_w_gate.astype(BF16), moe_w_up.astype(BF16), moe_w_down.astype(BF16)
    m_states, b_kv, c_kv = [], [], []
    for layer in range(DEPTH):
        kind, slot = layer % 3, layer // 3
        if kind == 0:
            a, w_out, st = _mlstm_layer(x, mlstm_w_in[slot], mlstm_b_gate[slot], mlstm_norm_g[slot], mlstm_w_out[slot],
                                        state_mlstm_C[slot], state_mlstm_n[slot], state_mlstm_m[slot])
            m_states.append(st)
        elif kind == 1:
            a, w_out, kv = _moba_layer(x, moba_w_in[slot], moba_w_out[slot], moba_k, moba_v,
                                       page_table + slot * n_pool)
            b_kv.append(kv)
        else:
            a, w_out, kv = _swa_layer(x, swa_w_in[slot], swa_sinks[slot], swa_w_out[slot], cache_swa_k[slot],
                                      cache_swa_v[slot])
            c_kv.append(kv)
        x = outproj_ln(a, w_out, x, ln_mix_g[layer], ln_mix_b[layer])
        x = _moe_layer(x, moe_w_group[layer], moe_b_group[layer], moe_w_router[layer], moe_b_router[layer],
                       wg_all, wu_all, wd_all, ln_ffn_g[layer], ln_ffn_b[layer], layer)
    stack = lambda items, j: jnp.stack([it[j] for it in items])
    return (x[:NP].reshape(BATCH, SEQ, D_MODEL), x[NP:].reshape(DEC_BATCH, DEC_SEQ, D_MODEL),
            stack(m_states, 0), stack(m_states, 1), stack(m_states, 2),
            stack(m_states, 3), stack(m_states, 4), stack(m_states, 5),
            stack(b_kv, 0), stack(b_kv, 1), stack(b_kv, 2), stack(b_kv, 3),
            stack(c_kv, 0), stack(c_kv, 1), stack(c_kv, 2), stack(c_kv, 3))
```

```python
import functools

import jax
import jax.numpy as jnp
import numpy as np
from jax import lax
from jax.experimental import pallas as pl
from jax.experimental.pallas import tpu as pltpu

F32 = jnp.float32
BF16 = jnp.bfloat16
I32 = jnp.int32
HIGHEST = lax.Precision.HIGHEST

D_MODEL = 1024
BATCH = 2
SEQ = 8192
DEPTH = 4
DEC_BATCH = 128
DEC_SEQ = 4
PAGE_SIZE = 128
N_PAGES = 64
PAST = N_PAGES * PAGE_SIZE
NP = BATCH * SEQ
NS = DEC_BATCH * DEC_SEQ
NT = NP + NS

NH_M = 4
DK_M = 128
DV_M = 256
M_CHUNK = 128
M_QK = NH_M * DK_M
M_V = NH_M * DV_M
M_IN = 2 * M_QK + 2 * M_V + 2 * NH_M
M_IN_PAD = 3200
M_GATE_COL = 2 * M_QK + 2 * M_V

NH_B = 8
KVH_B = 4
G_B = 2
HD_B = 128
MOBA_BLOCK = 256
MOBA_TOPK = 3
NB_B = SEQ // MOBA_BLOCK
NB_S = PAST // MOBA_BLOCK
B_IN = (NH_B + 2 * KVH_B) * HD_B
SAMPLE_PAGES_PER_STEP = 16
AUG = 2 * HD_B
POS_LANE = NB_B

NH_C = 16
KVH_C = 2
G_C = 8
HD_C = 64
WINDOW = 128
C_IN = (NH_C + 2 * KVH_C) * HD_C
SWA_SEQ_PER_STEP = 8

N_GROUPS = 4
N_EXP = 8
N_EXPERTS = 32
D_FF = 256
ROUTER_ROWS = 128
MOE_TM = 1536
MOE_RC = 128
MOE_WSLOTS = 4
MOE_ROWS = 2 * MOE_TM + N_EXPERTS * 8 + MOE_RC

ALPHA = (2.0 * DEPTH) ** 0.25
EPS = 1e-5
LOG2E = 1.4426950408889634
NEG_INF = float("-inf")
MASKED = -1e30
VMEM_LIMIT = 56 * 1024 * 1024


def _params(*sem):
    return pltpu.CompilerParams(dimension_semantics=sem, vmem_limit_bytes=VMEM_LIMIT)


def _dot(a, b):
    return jnp.dot(a, b, preferred_element_type=F32)


def _dot_nt(a, b, precision=None):
    return lax.dot_general(a, b, (((1,), (1,)), ((), ())), precision=precision, preferred_element_type=F32)


def _dot_tn(a, b):
    return lax.dot_general(a, b, (((0,), (0,)), ((), ())), preferred_element_type=F32)


def _layer_norm(z, g, b):
    mu = jnp.mean(z, axis=-1, keepdims=True)
    zc = z - mu
    var = jnp.mean(zc * zc, axis=-1, keepdims=True)
    return zc * lax.rsqrt(var + EPS) * g + b


def _bf16_parts(x, n=3):
    parts = []
    for _ in range(n):
        bits = int(np.float32(x).view(np.uint32))
        rounded = ((bits + 0x7FFF + ((bits >> 16) & 1)) >> 16) << 16
        p = float(np.uint32(rounded & 0xFFFFFFFF).view(np.float32))
        parts.append(p)
        x = x - p
    return parts


def _mm_kernel(x_ref, w_ref, o_ref):
    o_ref[...] = _dot(x_ref[...].astype(BF16), w_ref[...])


def matmul(x, w_bf16, tm=512):
    m, k = x.shape
    n = w_bf16.shape[1]
    return pl.pallas_call(
        _mm_kernel,
        grid=(m // tm,),
        in_specs=[pl.BlockSpec((tm, k), lambda i: (i, 0)), pl.BlockSpec((k, n), lambda i: (0, 0))],
        out_specs=pl.BlockSpec((tm, n), lambda i: (i, 0)),
        out_shape=jax.ShapeDtypeStruct((m, n), F32),
        compiler_params=_params("arbitrary"),
        name="in_proj",
    )(x, w_bf16)


def _outproj_ln_kernel(ap_ref, as_ref, w_ref, r_ref, g_ref, b_ref, o_ref, *, n_p):
    a = jnp.where(pl.program_id(0) < n_p, ap_ref[...], as_ref[...])
    y = _dot(a.astype(BF16), w_ref[...])
    o_ref[...] = _layer_norm(ALPHA * r_ref[...] + y, g_ref[...], b_ref[...])


def outproj_ln(a_p, a_s, w_bf16, resid, g, b, tm=512):
    k = a_p.shape[1]
    n = w_bf16.shape[1]
    n_p = a_p.shape[0] // tm
    m = a_p.shape[0] + a_s.shape[0]
    row = lambda i: (i, 0)
    fixed = lambda i: (0, 0)
    return pl.pallas_call(
        functools.partial(_outproj_ln_kernel, n_p=n_p),
        grid=(m // tm,),
        in_specs=[pl.BlockSpec((tm, k), lambda i: (jnp.minimum(i, n_p - 1), 0)),
                  pl.BlockSpec((tm, k), lambda i: (jnp.maximum(i - n_p, 0), 0)),
                  pl.BlockSpec((k, n), fixed), pl.BlockSpec((tm, n), row),
                  pl.BlockSpec((1, n), fixed), pl.BlockSpec((1, n), fixed)],
        out_specs=pl.BlockSpec((tm, n), row),
        out_shape=jax.ShapeDtypeStruct((m, n), F32),
        compiler_params=_params("arbitrary"),
        name="out_proj_ln",
    )(a_p, a_s, w_bf16, resid, g.reshape(1, n), b.reshape(1, n))


def _log_sigmoid(x):
    return jnp.minimum(x, 0.0) - jnp.log1p(jnp.exp(-jnp.abs(x)))


def _mlstm_kernel(p_ref, c0_ref, n0_ref, m0_ref, bg_ref, ng_ref,
                  hg_ref, co_ref, no_ref, mo_ref, c_s, n_s, m_s, *, L, L_real):
    c = pl.program_id(1)

    @pl.when(c == 0)
    def _():
        c_s[...] = c0_ref[...]
        n_s[...] = n0_ref[...]
        m_s[...] = m0_ref[...]

    gates = p_ref[:, M_GATE_COL:M_IN_PAD] + bg_ref[...]
    lane = lax.broadcasted_iota(jnp.int32, (L, 128), 1)
    x = jnp.where(lane < NH_M, gates, _log_sigmoid(gates))
    if L_real < L:
        tok = lax.broadcasted_iota(jnp.int32, (L, 128), 0)
        x = jnp.where(tok < L_real, x, jnp.where(lane < NH_M, -1e30, 0.0))
    r = lax.broadcasted_iota(jnp.int32, (L, L), 0)
    s = lax.broadcasted_iota(jnp.int32, (L, L), 1)
    causal = r >= s
    tri = causal.astype(F32)
    bcum = jnp.dot(tri, x, precision=HIGHEST, preferred_element_type=F32)
    eye8 = (lax.broadcasted_iota(jnp.int32, (8, 128), 0) == lax.broadcasted_iota(jnp.int32, (8, 128), 1)).astype(F32)
    x_rows = _dot_nt(eye8, x, HIGHEST)
    b_rows = _dot_nt(eye8, bcum, HIGHEST)

    for h in range(NH_M):
        li_row = x_rows[h:h + 1, :]
        li_col = x[:, h:h + 1]
        b_row = b_rows[NH_M + h:NH_M + h + 1, :]
        b_col = bcum[:, NH_M + h:NH_M + h + 1]
        m_old = m_s[:, h:h + 1]
        dm = jnp.where(causal, b_col - b_row + li_row, NEG_INF)
        g_col = b_col + m_old
        mq = jnp.maximum(g_col, jnp.max(dm, axis=1, keepdims=True))
        w_intra = jnp.exp(dm - mq)
        w_inter = jnp.exp(g_col - mq)
        q = p_ref[:, h * DK_M:(h + 1) * DK_M] * (DK_M ** -0.5)
        k = p_ref[:, M_QK + h * DK_M:M_QK + (h + 1) * DK_M]
        v = p_ref[:, 2 * M_QK + h * DV_M:2 * M_QK + (h + 1) * DV_M]
        qb = q.astype(BF16)
        kb = k.astype(BF16)
        cmat = c_s[h]
        nrow = n_s[h:h + 1, :]
        sc = _dot_nt(qb, kb) * w_intra
        num = _dot(sc.astype(BF16), v.astype(BF16)) + w_inter * _dot(qb, cmat.astype(BF16))
        den = jnp.sum(sc, axis=1, keepdims=True) + w_inter * jnp.sum(q * nrow, axis=1, keepdims=True)
        hh = num / jnp.maximum(jnp.abs(den), jnp.exp(-mq))
        bl = b_col[L - 1:L, :]
        lw = bl - b_col + li_col
        m_new = jnp.maximum(bl + m_old, jnp.max(lw, axis=0, keepdims=True))
        wl = jnp.exp(lw - m_new)
        dec = jnp.exp(bl + m_old - m_new)
        c_s[h] = dec * cmat + _dot_tn(kb, (wl * v).astype(BF16))
        n_s[h:h + 1, :] = dec * nrow + jnp.sum(wl * k, axis=0, keepdims=True)
        m_s[:, h:h + 1] = m_new
        mu = jnp.mean(hh, axis=1, keepdims=True)
        hc = hh - mu
        var = jnp.mean(hc * hc, axis=1, keepdims=True)
        hn = hc * lax.rsqrt(var + EPS) * ng_ref[:, h * DV_M:(h + 1) * DV_M]
        o = p_ref[:, 2 * M_QK + M_V + h * DV_M:2 * M_QK + M_V + (h + 1) * DV_M]
        hg_ref[:, h * DV_M:(h + 1) * DV_M] = hn / (1.0 + jnp.exp(-o))

    @pl.when(c == pl.num_programs(1) - 1)
    def _():
        co_ref[...] = c_s[...]
        no_ref[...] = n_s[...]
        mo_ref[...] = m_s[...]


def mlstm(p, c0, n0, m0, b_gate_row, norm_g_row, nb, nc, L, L_real):
    kern = functools.partial(_mlstm_kernel, L=L, L_real=L_real)
    per_b = lambda b, c: (b, 0, 0)
    return pl.pallas_call(
        kern,
        grid=(nb, nc),
        in_specs=[pl.BlockSpec((L, M_IN_PAD), lambda b, c: (b * nc + c, 0)),
                  pl.BlockSpec((None, NH_M, DK_M, DV_M), lambda b, c: (b, 0, 0, 0)),
                  pl.BlockSpec((None, NH_M, DK_M), per_b),
                  pl.BlockSpec((None, 1, NH_M), per_b),
                  pl.BlockSpec((1, 128), lambda b, c: (0, 0)),
                  pl.BlockSpec((1, M_V), lambda b, c: (0, 0))],
        out_specs=[pl.BlockSpec((L, M_V), lambda b, c: (b * nc + c, 0)),
                   pl.BlockSpec((None, NH_M, DK_M, DV_M), lambda b, c: (b, 0, 0, 0)),
                   pl.BlockSpec((None, NH_M, DK_M), per_b),
                   pl.BlockSpec((None, 1, NH_M), per_b)],
        out_shape=[jax.ShapeDtypeStruct((nb * nc * L, M_V), F32),
                   jax.ShapeDtypeStruct((nb, NH_M, DK_M, DV_M), F32),
                   jax.ShapeDtypeStruct((nb, NH_M, DK_M), F32),
                   jax.ShapeDtypeStruct((nb, 1, NH_M), F32)],
        scratch_shapes=[pltpu.VMEM((NH_M, DK_M, DV_M), F32), pltpu.VMEM((NH_M, DK_M), F32),
                        pltpu.VMEM((1, NH_M), F32)],
        compiler_params=_params("arbitrary", "arbitrary"),
        name=f"mlstm_L{L}",
    )(p, c0, n0, m0.reshape(nb, 1, NH_M), b_gate_row, norm_g_row)


def _topk_mask_lanes(g, k):
    lane = lax.broadcasted_iota(jnp.int32, g.shape, 1)
    n = g.shape[1]
    sel = jnp.zeros(g.shape, F32)
    for _ in range(k):
        mx = jnp.max(g, axis=1, keepdims=True)
        idx = jnp.min(jnp.where(g == mx, lane, n), axis=1, keepdims=True)
        hit = (lane == idx) & (mx > NEG_INF)
        sel = jnp.where(hit, 1.0, sel)
        g = jnp.where(lane == idx, NEG_INF, g)
    return sel


def _moba_select_kernel(q_ref, k_ref, v_ref, ac_ref, qa_ref, ka_ref, vt_ref, kmean_s, *, tq):
    i = pl.program_id(2)

    @pl.when(i == 0)
    def _():
        kmean_s[...] = jnp.zeros_like(kmean_s)
        kmean_s[0:NB_B, :] = jnp.sum(k_ref[...].reshape(NB_B, MOBA_BLOCK, HD_B), axis=1) * (1.0 / MOBA_BLOCK)
        rows = 1024
        lane = lax.broadcasted_iota(jnp.int32, (rows, HD_B), 1)
        rloc = lax.broadcasted_iota(jnp.int32, (rows, HD_B), 0)
        for ch in range(SEQ // rows):
            vt_ref[:, ch * rows:(ch + 1) * rows] = v_ref[ch * rows:(ch + 1) * rows, :].T.astype(BF16)
            row = rloc + ch * rows
            onehot = (lane == row // MOBA_BLOCK).astype(F32)
            offs = (row % MOBA_BLOCK).astype(F32)
            blks = (row // MOBA_BLOCK).astype(F32)
            pat = jnp.where(lane < NB_B, onehot,
                            jnp.where(lane < POS_LANE + 3, offs, jnp.where(lane < POS_LANE + 6, blks, 0.0)))
            ka_ref[ch * rows:(ch + 1) * rows, 0:HD_B] = k_ref[ch * rows:(ch + 1) * rows, :].astype(BF16)
            ka_ref[ch * rows:(ch + 1) * rows, HD_B:AUG] = pat.astype(BF16)

    km = kmean_s[...]
    lane = lax.broadcasted_iota(jnp.int32, (tq, HD_B), 1)
    own = (i * tq + lax.broadcasted_iota(jnp.int32, (tq, HD_B), 0)) // MOBA_BLOCK
    for g in range(G_B):
        q = q_ref[:, g * HD_B:(g + 1) * HD_B]
        gate = _dot_nt(q * (HD_B ** -0.5), km, HIGHEST)
        gate = jnp.where(lane < own, gate, NEG_INF)
        sel = _topk_mask_lanes(gate, MOBA_TOPK)
        sel = jnp.where(lane == own, 1.0, sel)
        aug = jnp.where(lane < NB_B, jnp.where(sel > 0.5, 0.0, MASKED), ac_ref[g, 0:1, :])
        qa_ref[:, g * AUG:g * AUG + HD_B] = (q * (HD_B ** -0.5 * LOG2E)).astype(BF16)
        qa_ref[:, g * AUG + HD_B:(g + 1) * AUG] = aug.astype(BF16)


def moba_select(p, aug_const, tq=1024):
    kern = functools.partial(_moba_select_kernel, tq=tq)
    nq = SEQ // tq
    per_bc = lambda b, c, i: (b, c, 0, 0)
    return pl.pallas_call(
        kern,
        grid=(BATCH, KVH_B, nq),
        in_specs=[pl.BlockSpec((tq, G_B * HD_B), lambda b, c, i: (b * nq + i, c)),
                  pl.BlockSpec((SEQ, HD_B), lambda b, c, i: (b, NH_B + c)),
                  pl.BlockSpec((SEQ, HD_B), lambda b, c, i: (b, NH_B + KVH_B + c)),
                  pl.BlockSpec((G_B, 8, 128), lambda b, c, i: (c, 0, 0))],
        out_specs=[pl.BlockSpec((tq, G_B * AUG), lambda b, c, i: (b * nq + i, c)),
                   pl.BlockSpec((None, None, SEQ, AUG), per_bc),
                   pl.BlockSpec((None, None, HD_B, SEQ), per_bc)],
        out_shape=[jax.ShapeDtypeStruct((NP, NH_B * AUG), BF16),
                   jax.ShapeDtypeStruct((BATCH, KVH_B, SEQ, AUG), BF16),
                   jax.ShapeDtypeStruct((BATCH, KVH_B, HD_B, SEQ), BF16)],
        scratch_shapes=[pltpu.VMEM((128, HD_B), F32)],
        compiler_params=_params("arbitrary", "arbitrary", "arbitrary"),
        name="moba_select",
    )(p, p, p, aug_const)


def _moba_flash_kernel(qa_ref, ka_ref, vt_ref, o_ref, s_buf):
    i = pl.program_id(2)
    bq = MOBA_BLOCK
    tk = 2 * MOBA_BLOCK
    nl = G_B * bq
    qa = jnp.concatenate([qa_ref[:, g * AUG:(g + 1) * AUG] for g in range(G_B)], axis=0)

    def scores(t, slot):
        off = pl.multiple_of(t * tk, tk)
        s_buf[slot] = _dot_nt(ka_ref[pl.ds(off, tk), :], qa)

    def softmax_pv(t, slot, last, carry):
        m, l, acc = carry
        st = s_buf[slot]
        if last:
            kpos = t * tk + lax.broadcasted_iota(jnp.int32, (tk, nl), 0)
            qpos = i * bq + lax.broadcasted_iota(jnp.int32, (tk, nl), 1) % bq
            st = jnp.where(kpos <= qpos, st, NEG_INF)
        m_new = jnp.maximum(m, jnp.max(st, axis=0, keepdims=True))
        pt = jnp.exp2(st - m_new)
        a = jnp.exp2(m - m_new)
        l = a * l + jnp.sum(pt, axis=0, keepdims=True)
        off = pl.multiple_of(t * tk, tk)
        acc = a * acc + _dot(vt_ref[:, pl.ds(off, tk)], pt.astype(BF16))
        return m_new, l, acc

    def pair(u, carry):
        scores(2 * u + 1, 1)
        carry = softmax_pv(2 * u, 0, False, carry)
        scores(2 * u + 2, 0)
        return softmax_pv(2 * u + 1, 1, False, carry)

    def tail_two(carry):
        t = 2 * npairs
        scores(t + 1, 1)
        return softmax_pv(t + 1, 1, True, softmax_pv(t, 0, False, carry))

    def tail_one(carry):
        return softmax_pv(2 * npairs, 0, True, carry)

    n_full = i // 2
    npairs = n_full // 2
    init = (jnp.full((1, nl), MASKED, F32), jnp.zeros((1, nl), F32), jnp.zeros((HD_B, nl), F32))
    scores(0, 0)
    carry = lax.fori_loop(0, npairs, pair, init)
    m, l, acc = lax.cond(n_full % 2 == 1, tail_two, tail_one, carry)
    out = acc / l
    for g in range(G_B):
        o_ref[:, g * HD_B:(g + 1) * HD_B] = out[:, g * bq:(g + 1) * bq].T


def moba_flash(q_aug, k_aug, v_t):
    nq = SEQ // MOBA_BLOCK
    per_bc = lambda b, c, i: (b, c, 0, 0)
    return pl.pallas_call(
        _moba_flash_kernel,
        grid=(BATCH, KVH_B, nq),
        in_specs=[pl.BlockSpec((MOBA_BLOCK, G_B * AUG), lambda b, c, i: (b * nq + i, c)),
                  pl.BlockSpec((None, None, SEQ, AUG), per_bc),
                  pl.BlockSpec((None, None, HD_B, SEQ), per_bc)],
        out_specs=pl.BlockSpec((MOBA_BLOCK, G_B * HD_B), lambda b, c, i: (b * nq + i, c)),
        out_shape=jax.ShapeDtypeStruct((NP, NH_B * HD_B), F32),
        scratch_shapes=[pltpu.VMEM((2, 2 * MOBA_BLOCK, G_B * MOBA_BLOCK), F32)],
        compiler_params=_params("arbitrary", "arbitrary", "arbitrary"),
        name="moba_flash",
    )(q_aug, k_aug, v_t)


def _moba_aug_const():
    c = np.zeros((NH_B, 8, 128), np.float32)
    for h in range(NH_B):
        beta = 2.0 ** (-8.0 * (h + 1) / NH_B) * LOG2E
        c[h, 0, POS_LANE:POS_LANE + 3] = _bf16_parts(beta)
        c[h, 0, POS_LANE + 3:POS_LANE + 6] = _bf16_parts(beta * MOBA_BLOCK)
    return jnp.asarray(c)


def _moba_sample_kernel(pt_ref, qa_ref, qat_ref, kn_ref, vn_ref, bias_ref, ck_ref, cv_ref,
                        o_ref, buf, sem, s_s, p_s, ksum_s, acc_s, inv_s):
    npg = SAMPLE_PAGES_PER_STEP
    w = KVH_B * HD_B
    nrow = KVH_B * DEC_SEQ * G_B

    def load_page(sl, pg):
        return jnp.concatenate([buf[sl, pg, pl.ds(c, PAGE_SIZE, stride=KVH_B), :] for c in range(KVH_B)], axis=1)

    b = pl.program_id(0)
    t = pl.program_id(1)
    nb = pl.num_programs(0)
    g = b * 8 + t
    slot = g % 2

    def page_copy(src_ref, page, sl, pg):
        return pltpu.make_async_copy(src_ref.at[page], buf.at[sl, pg], sem.at[sl])

    def start_fetch(bn, tn, sl):
        qn = tn % 4

        @pl.when(tn < 4)
        def _():
            for pg in range(npg):
                page_copy(ck_ref, pt_ref[bn, qn * npg + pg], sl, pg).start()

        @pl.when(tn >= 4)
        def _():
            for pg in range(npg):
                page_copy(cv_ref, pt_ref[bn, qn * npg + pg], sl, pg).start()

    @pl.when(g == 0)
    def _():
        start_fetch(b, t, slot)

    @pl.when(g + 1 < nb * 8)
    def _():
        tn = (t + 1) % 8
        bn = b + (t + 1) // 8
        start_fetch(bn, tn, 1 - slot)

    for pg in range(npg):
        page_copy(ck_ref, 0, slot, pg).wait()

    qab = (qa_ref[...] * LOG2E).astype(BF16)

    @pl.when(g == 0)
    def _():
        p_s[...] = jnp.zeros_like(p_s)

    @pl.when(t < 4)
    def _():
        for pg in range(npg):
            page = load_page(slot, pg)
            col0 = pl.multiple_of((t * npg + pg) * PAGE_SIZE, PAGE_SIZE)
            s_s[:, pl.ds(col0, PAGE_SIZE)] = _dot(page.astype(BF16), qab).T
            csum = jnp.sum(page, axis=0, keepdims=True)
            if pg % 2 == 0:
                prev = csum
            else:
                ksum_s[pl.ds(t * (npg // 2) + pg // 2, 1), :] = prev + csum

    @pl.when(t == 3)
    def _():
        qat = qat_ref[...]
        kmean = ksum_s[...] * (1.0 / MOBA_BLOCK)
        sel = _topk_mask_lanes(_dot_nt(qat, kmean, HIGHEST), MOBA_TOPK)
        kn = jnp.concatenate([kn_ref[...], jnp.zeros((PAGE_SIZE - 8, w), F32)], axis=0)
        s_s[0:nrow, PAST:PAST + PAGE_SIZE] = _dot_nt((qat * LOG2E).astype(BF16), kn.astype(BF16))
        selm = jnp.where(sel > 0.5, 0.0, NEG_INF)

        def logits(bk):
            if bk == NB_S:
                return s_s[0:nrow, PAST:PAST + PAGE_SIZE] + bias_ref[:, PAST:PAST + PAGE_SIZE]
            lo = bk * MOBA_BLOCK
            return s_s[0:nrow, lo:lo + MOBA_BLOCK] + bias_ref[:, lo:lo + MOBA_BLOCK] + selm[:, bk:bk + 1]

        macc = logits(0)
        for bk in range(1, NB_S):
            macc = jnp.maximum(macc, logits(bk))
        mx = jnp.maximum(jnp.max(macc, axis=1, keepdims=True), jnp.max(logits(NB_S), axis=1, keepdims=True))
        dacc = jnp.zeros((nrow, MOBA_BLOCK), F32)
        for bk in range(NB_S):
            pr = jnp.exp2(logits(bk) - mx)
            p_s[0:nrow, bk * MOBA_BLOCK:(bk + 1) * MOBA_BLOCK] = pr
            dacc = dacc + pr
        pr = jnp.exp2(logits(NB_S) - mx)
        p_s[0:nrow, PAST:PAST + PAGE_SIZE] = pr
        den = jnp.sum(dacc, axis=1, keepdims=True) + jnp.sum(pr, axis=1, keepdims=True)
        inv_s[...] = jnp.zeros_like(inv_s)
        inv_s[0:nrow, :] = jnp.broadcast_to(1.0 / den, (nrow, 128))
        inv_s[...] = inv_s[...].T

    @pl.when(t == 4)
    def _():
        vn = jnp.concatenate([vn_ref[...], jnp.zeros((PAGE_SIZE - 8, w), F32)], axis=0)
        acc_s[...] = _dot_tn(vn.astype(BF16), p_s[:, PAST:PAST + PAGE_SIZE].T.astype(BF16))

    @pl.when(t >= 4)
    def _():
        acc = acc_s[...]
        for pg in range(npg):
            col0 = pl.multiple_of(((t - 4) * npg + pg) * PAGE_SIZE, PAGE_SIZE)
            pt = p_s[:, pl.ds(col0, PAGE_SIZE)].T
            acc = acc + _dot_tn(load_page(slot, pg).astype(BF16), pt.astype(BF16))
        acc_s[...] = acc

    @pl.when(t == 7)
    def _():
        o_ref[...] = acc_s[:, 0:nrow] * inv_s[0:1, 0:nrow]


def moba_sample(page_table, q_aug, q_aug_t, k_new, v_new, bias, cache_k, cache_v):
    w = KVH_B * HD_B
    nrow = KVH_B * DEC_SEQ * G_B
    nkeys = PAST + PAGE_SIZE
    per_b3 = lambda b, t, pt: (b, 0, 0)
    fixed = lambda b, t, pt: (0, 0)
    grid_spec = pltpu.PrefetchScalarGridSpec(
        num_scalar_prefetch=1,
        grid=(DEC_BATCH, 8),
        in_specs=[pl.BlockSpec((None, w, 128), per_b3),
                  pl.BlockSpec((None, nrow, w), per_b3),
                  pl.BlockSpec((None, 8, w), per_b3),
                  pl.BlockSpec((None, 8, w), per_b3),
                  pl.BlockSpec((nrow, nkeys), fixed),
                  pl.BlockSpec(memory_space=pl.ANY),
                  pl.BlockSpec(memory_space=pl.ANY)],
        out_specs=pl.BlockSpec((None, w, nrow), per_b3),
        scratch_shapes=[pltpu.VMEM((2, SAMPLE_PAGES_PER_STEP, PAGE_SIZE * KVH_B, HD_B), F32),
                        pltpu.SemaphoreType.DMA((2,)),
                        pltpu.VMEM((128, nkeys), F32),
                        pltpu.VMEM((128, nkeys), F32),
                        pltpu.VMEM((NB_S, w), F32),
                        pltpu.VMEM((w, 128), F32),
                        pltpu.VMEM((128, 128), F32)],
    )
    return pl.pallas_call(
        _moba_sample_kernel,
        grid_spec=grid_spec,
        out_shape=jax.ShapeDtypeStruct((DEC_BATCH, w, nrow), F32),
        compiler_params=_params("arbitrary", "arbitrary"),
        name="moba_sample",
    )(page_table, q_aug, q_aug_t, k_new, v_new, bias, cache_k, cache_v)


def _swa_slope(h):
    return 2.0 ** (-8.0 * (h + 1) / NH_C)


def _by_head(gidx, values):
    out = values[-1]
    for g in range(len(values) - 2, -1, -1):
        out = jnp.where(gidx == g, values[g], out)
    return out


def _swa_kv_aug(k_all, v_all, c):
    nk = k_all.shape[0]
    lane = lax.broadcasted_iota(jnp.int32, (nk, HD_C), 1)
    kidx = lax.broadcasted_iota(jnp.int32, (nk, HD_C), 0).astype(F32)
    kpat = jnp.where(lane < 3, kidx, 0.0)
    ka = jnp.concatenate([k_all[:, c * HD_C:(c + 1) * HD_C], kpat], axis=1).astype(BF16)
    return ka, v_all[:, c * HD_C:(c + 1) * HD_C].astype(BF16)


def _swa_softmax_pv(qa, ka, vc, maskt, sink2):
    s2 = _dot_nt(qa, ka) + maskt
    mx = jnp.maximum(jnp.max(s2, axis=1, keepdims=True), sink2)
    pr = jnp.exp2(s2 - mx)
    den = jnp.sum(pr, axis=1, keepdims=True) + jnp.exp2(sink2 - mx)
    return _dot(pr.astype(BF16), vc) / den


def _store_heads(o_ref, outs, lead=()):
    for j in range(0, NH_C, 2):
        o_ref[lead + (slice(None), slice(j * HD_C, (j + 2) * HD_C))] = jnp.concatenate(outs[j:j + 2], axis=1)


def _swa_prompt_kernel(q_ref, kvp_ref, kvc_ref, sink_ref, o_ref):
    i = pl.program_id(1)
    w = WINDOW
    kw = KVH_C * HD_C
    k_all = jnp.concatenate([kvp_ref[:, 0:kw], kvc_ref[:, 0:kw]], axis=0)
    v_all = jnp.concatenate([kvp_ref[:, kw:2 * kw], kvc_ref[:, kw:2 * kw]], axis=0)
    kk = lax.broadcasted_iota(jnp.int32, (2 * w, w), 0)
    qq = lax.broadcasted_iota(jnp.int32, (2 * w, w), 1)
    di = qq - kk + w
    ok = (di >= 0) & (di <= w) & ((kk >= w) | (i > 0))
    mask1 = jnp.where(ok, 0.0, NEG_INF)
    maskt = jnp.concatenate([mask1] * G_C, axis=1)
    qdist = (lax.broadcasted_iota(jnp.int32, (1, w), 1) + w).astype(F32)
    qlane = lax.broadcasted_iota(jnp.int32, (w, HD_C), 1)
    sink = sink_ref[...]
    for c in range(KVH_C):
        ka, vc = _swa_kv_aug(k_all, v_all, c)
        qas, sinks = [], []
        for gi in range(G_C):
            h = c * G_C + gi
            b1, b2, b3 = _bf16_parts(_swa_slope(h) * LOG2E)
            qpat = jnp.where(qlane == 0, b1, jnp.where(qlane == 1, b2, jnp.where(qlane == 2, b3, 0.0)))
            qa = jnp.concatenate([q_ref[:, h * HD_C:(h + 1) * HD_C] * (HD_C ** -0.5 * LOG2E), qpat], axis=1)
            qas.append(qa.astype(BF16))
            sinks.append((sink[:, h:h + 1] + _swa_slope(h) * qdist) * LOG2E)
        sink2 = jnp.concatenate(sinks, axis=1)
        st = _dot_nt(ka, jnp.concatenate(qas, axis=0)) + maskt
        mx = jnp.maximum(jnp.max(st, axis=0, keepdims=True), sink2)
        pt = jnp.exp2(st - mx)
        den = jnp.sum(pt, axis=0, keepdims=True) + jnp.exp2(sink2 - mx)
        ot = _dot_tn(vc, pt.astype(BF16)) / den
        for gi in range(0, G_C, 2):
            h = c * G_C + gi
            two = jnp.concatenate([ot[:, gi * w:(gi + 1) * w], ot[:, (gi + 1) * w:(gi + 2) * w]], axis=0)
            o_ref[:, h * HD_C:(h + 2) * HD_C] = two.T


def swa_prompt(p, sink_row):
    nblk = SEQ // WINDOW
    kvb = NH_C * HD_C // (2 * KVH_C * HD_C)
    return pl.pallas_call(
        _swa_prompt_kernel,
        grid=(BATCH, nblk),
        in_specs=[pl.BlockSpec((WINDOW, NH_C * HD_C), lambda b, i: (b * nblk + i, 0)),
                  pl.BlockSpec((WINDOW, 2 * KVH_C * HD_C), lambda b, i: (b * nblk + jnp.maximum(i - 1, 0), kvb)),
                  pl.BlockSpec((WINDOW, 2 * KVH_C * HD_C), lambda b, i: (b * nblk + i, kvb)),
                  pl.BlockSpec((1, 128), lambda b, i: (0, 0))],
        out_specs=pl.BlockSpec((WINDOW, NH_C * HD_C), lambda b, i: (b * nblk + i, 0)),
        out_shape=jax.ShapeDtypeStruct((NP, NH_C * HD_C), F32),
        compiler_params=_params("arbitrary", "arbitrary"),
        name="swa_prompt",
    )(p, p, p, sink_row)


def _swa_sample_kernel(q_ref, kn_ref, vn_ref, ck_ref, cv_ref, sink_ref, o_ref):
    w = WINDOW
    nk = w + 8
    nr = G_C * 8
    r = lax.broadcasted_iota(jnp.int32, (nr, nk), 0)
    cc = lax.broadcasted_iota(jnp.int32, (nr, nk), 1)
    di = w + r % 8 - cc
    ok = (di >= 0) & (di <= w) & (cc < w + DEC_SEQ)
    maskt = jnp.where(ok, 0.0, NEG_INF)
    rcol = lax.broadcasted_iota(jnp.int32, (nr, 1), 0)
    gcol = rcol // 8
    qdist = (w + rcol % 8).astype(F32)
    qlane = lax.broadcasted_iota(jnp.int32, (nr, HD_C), 1)
    sink = sink_ref[...]
    qpats, sink2s = [], []
    for c in range(KVH_C):
        heads = [c * G_C + gi for gi in range(G_C)]
        parts = [_bf16_parts(_swa_slope(h) * LOG2E) for h in heads]
        b = [_by_head(gcol, [pp[j] for pp in parts]) for j in range(3)]
        qpats.append(jnp.where(qlane == 0, b[0], jnp.where(qlane == 1, b[1], jnp.where(qlane == 2, b[2], 0.0))))
        slope = _by_head(gcol, [_swa_slope(h) for h in heads])
        sk = _by_head(gcol, [sink[:, h:h + 1] for h in heads])
        sink2s.append((sk + slope * qdist) * LOG2E)
    for sq in range(SWA_SEQ_PER_STEP):
        q = q_ref[sq] * (HD_C ** -0.5 * LOG2E)
        k_all = jnp.concatenate([ck_ref[sq], kn_ref[sq]], axis=0)
        v_all = jnp.concatenate([cv_ref[sq], vn_ref[sq]], axis=0)
        outs = []
        for c in range(KVH_C):
            ka, vc = _swa_kv_aug(k_all, v_all, c)
            qc = jnp.concatenate([q[:, (c * G_C + gi) * HD_C:(c * G_C + gi + 1) * HD_C] for gi in range(G_C)], axis=0)
            qa = jnp.concatenate([qc, qpats[c]], axis=1).astype(BF16)
            oc = _swa_softmax_pv(qa, ka, vc, maskt, sink2s[c])
            outs.extend(oc[gi * 8:(gi + 1) * 8, :] for gi in range(G_C))
        _store_heads(o_ref, outs, lead=(sq,))


def swa_sample(q, k_new, v_new, cache_k, cache_v, sink_row):
    n = SWA_SEQ_PER_STEP
    kw = KVH_C * HD_C
    blk = lambda i: (i, 0, 0)
    return pl.pallas_call(
        _swa_sample_kernel,
        grid=(DEC_BATCH // n,),
        in_specs=[pl.BlockSpec((n, 8, NH_C * HD_C), blk), pl.BlockSpec((n, 8, kw), blk), pl.BlockSpec((n, 8, kw), blk),
                  pl.BlockSpec((n, WINDOW, kw), blk), pl.BlockSpec((n, WINDOW, kw), blk),
                  pl.BlockSpec((1, 128), lambda i: (0, 0))],
        out_specs=pl.BlockSpec((n, 8, NH_C * HD_C), blk),
        out_shape=jax.ShapeDtypeStruct((DEC_BATCH, 8, NH_C * HD_C), F32),
        compiler_params=_params("arbitrary"),
        name="swa_sample",
    )(q, k_new, v_new, cache_k, cache_v, sink_row)


def _router_kernel(x_ref, wt_ref, bt_ref, meta_ref, wts_ref, seg_ref, tri_s, *, tm):
    @pl.when(pl.program_id(0) == 0)
    def _():
        rr = lax.broadcasted_iota(jnp.int32, (128, tm), 0)
        ccn = lax.broadcasted_iota(jnp.int32, (128, tm), 1)
        for ch in range(tm // 128):
            tri_s[ch * 128:(ch + 1) * 128, :] = (rr + ch * 128 < ccn).astype(BF16)

    lt = _dot_nt(wt_ref[...], x_ref[...], HIGHEST) + bt_ref[...]
    row = lax.broadcasted_iota(jnp.int32, lt.shape, 0)
    big = ROUTER_ROWS

    lg = jnp.where(row < N_GROUPS, lt, NEG_INF)
    mg = jnp.max(lg, axis=0, keepdims=True)
    eg = jnp.exp(lg - mg)
    pg = eg / jnp.sum(eg, axis=0, keepdims=True)
    pg1 = jnp.max(pg, axis=0, keepdims=True)
    g1 = jnp.min(jnp.where(pg == pg1, row, big), axis=0, keepdims=True)

    e = row - N_GROUPS
    ingroup = (e >= g1 * N_EXP) & (e < (g1 + 1) * N_EXP)
    le = jnp.where(ingroup, lt, NEG_INF)
    me = jnp.max(le, axis=0, keepdims=True)
    ee = jnp.exp(le - me)
    pe = ee / jnp.sum(ee, axis=0, keepdims=True)
    pe = jnp.where(ingroup, pe, NEG_INF)
    p1 = jnp.max(pe, axis=0, keepdims=True)
    i1 = jnp.min(jnp.where(pe == p1, row, big), axis=0, keepdims=True)
    pe2 = jnp.where(row == i1, NEG_INF, pe)
    p2 = jnp.max(pe2, axis=0, keepdims=True)
    i2 = jnp.min(jnp.where(pe2 == p2, row, big), axis=0, keepdims=True)
    tot = p1 + p2
    w1 = p1 / tot * pg1
    w2 = p2 / tot * pg1

    hit1 = row == i1
    hit2 = row == i2
    oh = jnp.where(hit1 | hit2, 1.0, 0.0)
    ohb = oh.astype(BF16)
    before = _dot(ohb, tri_s[...])
    cnt_col = jnp.sum(oh, axis=1, keepdims=True)
    pad_col = jnp.floor((cnt_col + 7.0) * 0.125) * 8.0
    r128 = lax.broadcasted_iota(jnp.int32, (128, 128), 0)
    c128 = lax.broadcasted_iota(jnp.int32, (128, 128), 1)
    off_col = jnp.dot((c128 < r128).astype(F32), pad_col + jnp.zeros((128, 128), F32),
                      precision=HIGHEST, preferred_element_type=F32)[:, 0:1]
    place = before + off_col
    pos1 = jnp.sum(jnp.where(hit1, place, 0.0), axis=0, keepdims=True)
    pos2 = jnp.sum(jnp.where(hit2, place, 0.0), axis=0, keepdims=True)
    cnt_row = _dot_nt(jnp.ones((8, tm), BF16), ohb)
    pad_row = jnp.floor((cnt_row + 7.0) * 0.125) * 8.0
    off_row = jnp.dot(pad_row, (r128 < c128).astype(F32), precision=HIGHEST, preferred_element_type=F32)

    r8 = lax.broadcasted_iota(jnp.int32, (8, tm), 0)
    meta_ref[...] = jnp.where(r8 == 0, pos1, jnp.where(r8 == 1, pos2, 0.0)).astype(I32)
    wts_ref[...] = jnp.where(r8 == 0, w1, jnp.where(r8 == 1, w2, 0.0))
    s8 = lax.broadcasted_iota(jnp.int32, (8, 128), 0)
    seg_ref[...] = jnp.where(s8 == 0, off_row, jnp.where(s8 == 1, cnt_row, 0.0)).astype(I32)


def moe_router(x, wt_router, bt_router, tm=MOE_TM):
    m = x.shape[0]
    nt = m // tm
    blk = lambda i: (i, 0, 0)
    return pl.pallas_call(
        functools.partial(_router_kernel, tm=tm),
        grid=(nt,),
        in_specs=[pl.BlockSpec((tm, D_MODEL), lambda i: (i, 0)),
                  pl.BlockSpec((ROUTER_ROWS, D_MODEL), lambda i: (0, 0)),
                  pl.BlockSpec((ROUTER_ROWS, 1), lambda i: (0, 0))],
        out_specs=[pl.BlockSpec((None, 8, tm), blk), pl.BlockSpec((None, 8, tm), blk),
                   pl.BlockSpec((None, 8, 128), blk)],
        out_shape=[jax.ShapeDtypeStruct((nt, 8, tm), I32), jax.ShapeDtypeStruct((nt, 8, tm), F32),
                   jax.ShapeDtypeStruct((nt, 8, 128), I32)],
        scratch_shapes=[pltpu.VMEM((tm, tm), BF16)],
        compiler_params=_params("arbitrary"),
        name="moe_router",
    )(x, wt_router, bt_router)


def _moe_kernel(x_ref, meta_hbm, wts_hbm, seg_hbm, wg_hbm, wu_hbm, wd_hbm, g_ref, b_ref, o_ref,
                xs, wbg, wbu, wbd, wsem, meta_s, wts_s, seg_s, msem, *, layer, tm):
    i = pl.program_id(0)
    nt = pl.num_programs(0)
    rc = MOE_RC
    ahead = MOE_WSLOTS - 1
    base = i * N_EXPERTS
    ms = i % 2

    def weight_copies(step):
        e = step % N_EXPERTS
        sl = step % MOE_WSLOTS
        return (pltpu.make_async_copy(wg_hbm.at[layer, e], wbg.at[sl], wsem.at[sl, 0]),
                pltpu.make_async_copy(wu_hbm.at[layer, e], wbu.at[sl], wsem.at[sl, 1]),
                pltpu.make_async_copy(wd_hbm.at[layer, e], wbd.at[sl], wsem.at[sl, 2]))

    def meta_copies(tile, sl):
        return (pltpu.make_async_copy(meta_hbm.at[tile], meta_s.at[sl], msem.at[sl, 0]),
                pltpu.make_async_copy(wts_hbm.at[tile], wts_s.at[sl], msem.at[sl, 1]),
                pltpu.make_async_copy(seg_hbm.at[tile], seg_s.at[sl], msem.at[sl, 2]))

    @pl.when(i == 0)
    def _():
        for cp in meta_copies(0, 0):
            cp.start()
        for step in range(ahead):
            for cp in weight_copies(step):
                cp.start()
        xs[...] = jnp.zeros_like(xs)

    for cp in meta_copies(i, ms):
        cp.wait()

    @pl.when(i + 1 < nt)
    def _():
        for cp in meta_copies(i + 1, 1 - ms):
            cp.start()

    def gather(t, carry):
        rowv = x_ref[pl.ds(t, 1), :]
        xs[pl.ds(meta_s[ms, 0, t], 1), :] = rowv
        xs[pl.ds(meta_s[ms, 1, t], 1), :] = rowv
        return carry

    lax.fori_loop(0, tm, gather, 0, unroll=8)

    def expert(e, carry):
        step = base + e
        sl = step % MOE_WSLOTS
        for cp in weight_copies(step):
            cp.wait()

        @pl.when(step + ahead < nt * N_EXPERTS)
        def _():
            for cp in weight_copies(step + ahead):
                cp.start()

        off = seg_s[ms, 0, N_GROUPS + e]
        cnt = seg_s[ms, 1, N_GROUPS + e]
        wgb = wbg[sl]
        wub = wbu[sl]
        wdb = wbd[sl]
        ridx = lax.broadcasted_iota(jnp.int32, (rc, 1), 0)

        def chunk(k, c2):
            row0 = pl.multiple_of(off + k * rc, 8)
            lhs = xs[pl.ds(row0, rc), :]
            lb = lhs.astype(BF16)
            hg = _dot(lb, wgb)
            hu = _dot(lb, wub)
            hh = hg / (1.0 + jnp.exp(-hg)) * hu
            out = _dot(hh.astype(BF16), wdb)
            xs[pl.ds(row0, rc), :] = jnp.where(ridx < cnt - k * rc, out, lhs)
            return c2

        lax.fori_loop(0, (cnt + rc - 1) // rc, chunk, 0)
        return carry

    lax.fori_loop(0, N_EXPERTS, expert, 0)

    def combine(t, carry):
        y = (wts_s[ms, 0, t] * xs[pl.ds(meta_s[ms, 0, t], 1), :]
             + wts_s[ms, 1, t] * xs[pl.ds(meta_s[ms, 1, t], 1), :])
        o_ref[pl.ds(t, 1), :] = y
        return carry

    lax.fori_loop(0, tm, combine, 0, unroll=8)
    o_ref[...] = _layer_norm(ALPHA * x_ref[...] + o_ref[...], g_ref[...], b_ref[...])


def moe_ffn_ln(x, meta, wts, seg, wg, wu, wd, g, b, layer, tm=MOE_TM):
    m = x.shape[0]
    row = lambda i: (i, 0)
    fixed = lambda i: (0, 0)
    hbm = pl.BlockSpec(memory_space=pl.ANY)
    return pl.pallas_call(
        functools.partial(_moe_kernel, layer=layer, tm=tm),
        grid=(m // tm,),
        in_specs=[pl.BlockSpec((tm, D_MODEL), row), hbm, hbm, hbm, hbm, hbm, hbm,
                  pl.BlockSpec((1, D_MODEL), fixed), pl.BlockSpec((1, D_MODEL), fixed)],
        out_specs=pl.BlockSpec((tm, D_MODEL), row),
        out_shape=jax.ShapeDtypeStruct((m, D_MODEL), F32),
        scratch_shapes=[pltpu.VMEM((MOE_ROWS, D_MODEL), F32),
                        pltpu.VMEM((MOE_WSLOTS, D_MODEL, D_FF), BF16), pltpu.VMEM((MOE_WSLOTS, D_MODEL, D_FF), BF16),
                        pltpu.VMEM((MOE_WSLOTS, D_FF, D_MODEL), BF16), pltpu.SemaphoreType.DMA((MOE_WSLOTS, 3)),
                        pltpu.SMEM((2, 8, tm), I32), pltpu.SMEM((2, 8, tm), F32), pltpu.SMEM((2, 8, 128), I32),
                        pltpu.SemaphoreType.DMA((2, 3))],
        compiler_params=_params("arbitrary"),
        name="moe_ffn_ln",
    )(x, meta, wts, seg, wg, wu, wd, g.reshape(1, D_MODEL), b.reshape(1, D_MODEL))


def _pad_rows(a, rows):
    return jnp.pad(a, ((0, 0), (0, rows - a.shape[1]), (0, 0)))


def _mlstm_layer(x, w_in, b_gate, norm_g, w_out, c0, n0, m0):
    w = jnp.pad(w_in, ((0, 0), (0, M_IN_PAD - M_IN))).astype(BF16)
    p = matmul(x, w)
    bg = jnp.pad(b_gate, (0, 128 - 2 * NH_M)).reshape(1, 128)
    ng = norm_g.reshape(1, M_V)
    nc = SEQ // M_CHUNK
    hp, cp, np_, mp = mlstm(p, jnp.zeros((BATCH, NH_M, DK_M, DV_M), F32), jnp.zeros((BATCH, NH_M, DK_M), F32),
                            jnp.zeros((BATCH, NH_M), F32), bg, ng, BATCH, nc, M_CHUNK, M_CHUNK)
    ps = _pad_rows(p[NP:].reshape(DEC_BATCH, DEC_SEQ, M_IN_PAD), 8).reshape(DEC_BATCH * 8, M_IN_PAD)
    hs, cs, ns, ms = mlstm(ps, c0, n0, m0, bg, ng, DEC_BATCH, 1, 8, DEC_SEQ)
    hs = hs.reshape(DEC_BATCH, 8, M_V)[:, :DEC_SEQ].reshape(NS, M_V)
    states = (cp, np_, mp.reshape(BATCH, NH_M), cs, ns, ms.reshape(DEC_BATCH, NH_M))
    return (hp, hs), w_out.astype(BF16), states


def _moba_sample_part(ps, cache_k, cache_v, page_rows):
    hq = NH_B * HD_B
    hk = KVH_B * HD_B
    slopes_h = jnp.exp2(-8.0 * jnp.arange(1, NH_B + 1, dtype=F32) / NH_B)
    q = (ps[:, :hq] * (HD_B ** -0.5)).reshape(DEC_BATCH, DEC_SEQ, KVH_B, G_B, HD_B)
    eye = jnp.eye(KVH_B, dtype=F32)
    nrow = KVH_B * DEC_SEQ * G_B
    q_aug_t = jnp.einsum("bscgd,ce->besgcd", q, eye).reshape(DEC_BATCH, nrow, hk)
    q_aug = jnp.pad(jnp.swapaxes(q_aug_t, 1, 2), ((0, 0), (0, 0), (0, 128 - nrow)))
    k_new = ps[:, hq:hq + hk].reshape(DEC_BATCH, DEC_SEQ, hk)
    v_new = ps[:, hq + hk:].reshape(DEC_BATCH, DEC_SEQ, hk)
    row = jnp.arange(nrow)
    row_c, row_s, row_g = row // (DEC_SEQ * G_B), (row // G_B) % DEC_SEQ, row % G_B
    slope_row = (slopes_h[row_c * G_B + row_g] * LOG2E)[:, None]
    qs_row = row_s.astype(F32)[:, None]
    past_bias = -slope_row * (float(PAST) + qs_row - jnp.arange(PAST, dtype=F32)[None, :])
    tok = jnp.arange(PAGE_SIZE, dtype=F32)[None, :]
    own_bias = jnp.where((tok <= qs_row) & (tok < float(DEC_SEQ)), -slope_row * (qs_row - tok), NEG_INF)
    bias = jnp.concatenate([past_bias, own_bias], axis=1)
    ot = moba_sample(page_rows, q_aug, q_aug_t, _pad_rows(k_new, 8), _pad_rows(v_new, 8), bias, cache_k, cache_v)
    ot = ot.reshape(DEC_BATCH, KVH_B, HD_B, KVH_B, DEC_SEQ, G_B)
    os_ = jnp.einsum("bcdesg,ce->bscgd", ot, eye).reshape(NS, hq)
    return os_, k_new, v_new


def _moba_layer(x, w_in, w_out, cache_k, cache_v, page_rows):
    p = matmul(x, w_in.astype(BF16))
    hq = NH_B * HD_B
    hk = KVH_B * HD_B
    ac = _moba_aug_const()
    q_aug, k_aug, v_t = moba_select(p, ac)
    op = moba_flash(q_aug, k_aug, v_t)
    os_, k_new, v_new = _moba_sample_part(p[NP:], cache_k, cache_v, page_rows)
    kv = (p[:NP, hq:hq + hk].reshape(BATCH, SEQ, KVH_B, HD_B), p[:NP, hq + hk:].reshape(BATCH, SEQ, KVH_B, HD_B),
          k_new.reshape(DEC_BATCH, DEC_SEQ, KVH_B, HD_B), v_new.reshape(DEC_BATCH, DEC_SEQ, KVH_B, HD_B))
    return (op, os_), w_out.astype(BF16), kv


def _swa_layer(x, w_in, sinks, w_out, cache_k, cache_v):
    p = matmul(x, w_in.astype(BF16))
    hq = NH_C * HD_C
    kw = KVH_C * HD_C
    sink_row = jnp.pad(sinks, (0, 128 - NH_C)).reshape(1, 128)
    op = swa_prompt(p, sink_row)
    ps = p[NP:]
    q = _pad_rows(ps[:, :hq].reshape(DEC_BATCH, DEC_SEQ, hq), 8)
    k_new = ps[:, hq:hq + kw].reshape(DEC_BATCH, DEC_SEQ, kw)
    v_new = ps[:, hq + kw:].reshape(DEC_BATCH, DEC_SEQ, kw)
    ck = cache_k.reshape(DEC_BATCH, WINDOW, kw)
    cv = cache_v.reshape(DEC_BATCH, WINDOW, kw)
    os_ = swa_sample(q, _pad_rows(k_new, 8), _pad_rows(v_new, 8), ck, cv, sink_row)[:, :DEC_SEQ].reshape(NS, hq)
    pp = p[:NP].reshape(BATCH, SEQ, C_IN)[:, SEQ - WINDOW:]
    kv = (pp[..., hq:hq + kw].reshape(BATCH, WINDOW, KVH_C, HD_C), pp[..., hq + kw:].reshape(BATCH, WINDOW, KVH_C, HD_C),
          jnp.concatenate([ck, k_new], axis=1)[:, DEC_SEQ:].reshape(DEC_BATCH, WINDOW, KVH_C, HD_C),
          jnp.concatenate([cv, v_new], axis=1)[:, DEC_SEQ:].reshape(DEC_BATCH, WINDOW, KVH_C, HD_C))
    return (op, os_), w_out.astype(BF16), kv


def _moe_layer(x, w_group, b_group, w_router, b_router, w_gate, w_up, w_down, g, b, layer):
    pad = ROUTER_ROWS - N_GROUPS - N_EXPERTS
    wt = jnp.concatenate([w_group.T, w_router.T, jnp.zeros((pad, D_MODEL), F32)], axis=0)
    bt = jnp.concatenate([b_group, b_router, jnp.zeros((pad,), F32)]).reshape(ROUTER_ROWS, 1)
    meta, wts, seg = moe_router(x, wt, bt)
    return moe_ffn_ln(x, meta, wts, seg, w_gate, w_up, w_down, g, b, layer)


def kernel(x_prompt, x_sample, state_mlstm_C, state_mlstm_n, state_mlstm_m, cache_moba_k, cache_moba_v, cache_swa_k, cache_swa_v, page_table, mlstm_w_in, mlstm_b_gate, mlstm_norm_g, mlstm_w_out, moba_w_in, moba_w_out, swa_w_in, swa_sinks, swa_w_out, ln_mix_g, ln_mix_b, ln_ffn_g, ln_ffn_b, moe_w_group, moe_b_group, moe_w_router, moe_b_router, moe_w_gate, moe_w_up, moe_w_down):
    x = jnp.concatenate([x_prompt.reshape(NP, D_MODEL), x_sample.reshape(NS, D_MODEL)], axis=0)
    n_pool = cache_moba_k.shape[1]
    moba_k = cache_moba_k.reshape(-1, PAGE_SIZE * KVH_B, HD_B)
    moba_v = cache_moba_v.reshape(-1, PAGE_SIZE * KVH_B, HD_B)
    wg_all, wu_all, wd_all = moe_w_gate.astype(BF16), moe_w_up.astype(BF16), moe_w_down.astype(BF16)
    m_states, b_kv, c_kv = [], [], []
    for layer in range(DEPTH):
        kind, slot = layer % 3, layer // 3
        if kind == 0:
            a, w_out, st = _mlstm_layer(x, mlstm_w_in[slot], mlstm_b_gate[slot], mlstm_norm_g[slot], mlstm_w_out[slot],
                                        state_mlstm_C[slot], state_mlstm_n[slot], state_mlstm_m[slot])
            m_states.append(st)
        elif kind == 1:
            a, w_out, kv = _moba_layer(x, moba_w_in[slot], moba_w_out[slot], moba_k, moba_v,
                                       page_table + slot * n_pool)
            b_kv.append(kv)
        else:
            a, w_out, kv = _swa_layer(x, swa_w_in[slot], swa_sinks[slot], swa_w_out[slot], cache_swa_k[slot],
                                      cache_swa_v[slot])
            c_kv.append(kv)
        x = outproj_ln(a[0], a[1], w_out, x, ln_mix_g[layer], ln_mix_b[layer])
        x = _moe_layer(x, moe_w_group[layer], moe_b_group[layer], moe_w_router[layer], moe_b_router[layer],
                       wg_all, wu_all, wd_all, ln_ffn_g[layer], ln_ffn_b[layer], layer)
    stack = lambda items, j: jnp.stack([it[j] for it in items])
    return (x[:NP].reshape(BATCH, SEQ, D_MODEL), x[NP:].reshape(DEC_BATCH, DEC_SEQ, D_MODEL),
            stack(m_states, 0), stack(m_states, 1), stack(m_states, 2),
            stack(m_states, 3), stack(m_states, 4), stack(m_states, 5),
            stack(b_kv, 0), stack(b_kv, 1), stack(b_kv, 2), stack(b_kv, 3),
            stack(c_kv, 0), stack(c_kv, 1), stack(c_kv, 2), stack(c_kv, 3))
```

```python
import functools

import jax
import jax.numpy as jnp
import numpy as np
from jax import lax
from jax.experimental import pallas as pl
from jax.experimental.pallas import tpu as pltpu

F32 = jnp.float32
BF16 = jnp.bfloat16
I32 = jnp.int32
HIGHEST = lax.Precision.HIGHEST

D_MODEL = 1024
BATCH = 2
SEQ = 8192
DEPTH = 4
DEC_BATCH = 128
DEC_SEQ = 4
PAGE_SIZE = 128
N_PAGES = 64
PAST = N_PAGES * PAGE_SIZE
NP = BATCH * SEQ
NS = DEC_BATCH * DEC_SEQ
NT = NP + NS

NH_M = 4
DK_M = 128
DV_M = 256
M_CHUNK = 128
M_QK = NH_M * DK_M
M_V = NH_M * DV_M
M_IN = 2 * M_QK + 2 * M_V + 2 * NH_M
M_IN_PAD = 3200
M_GATE_COL = 2 * M_QK + 2 * M_V
MLSTM_SEQ_PER_STEP = 4

NH_B = 8
KVH_B = 4
G_B = 2
HD_B = 128
MOBA_BLOCK = 256
MOBA_TOPK = 3
NB_B = SEQ // MOBA_BLOCK
NB_S = PAST // MOBA_BLOCK
B_IN = (NH_B + 2 * KVH_B) * HD_B
SAMPLE_PAGES_PER_STEP = 16
MOBA_QTILE = 2 * MOBA_BLOCK
AUG = 2 * HD_B
POS_LANE = NB_B

NH_C = 16
KVH_C = 2
G_C = 8
HD_C = 64
WINDOW = 128
C_IN = (NH_C + 2 * KVH_C) * HD_C
SWA_SEQ_PER_STEP = 8

N_GROUPS = 4
N_EXP = 8
N_EXPERTS = 32
D_FF = 256
ROUTER_ROWS = 128
MOE_TM = 1536
MOE_RC = 128
MOE_WSLOTS = 4
MOE_ROWS = 2 * MOE_TM + N_EXPERTS * 8 + MOE_RC

ALPHA = (2.0 * DEPTH) ** 0.25
EPS = 1e-5
LOG2E = 1.4426950408889634
NEG_INF = float("-inf")
MASKED = -1e30
VMEM_LIMIT = 56 * 1024 * 1024


def _params(*sem):
    return pltpu.CompilerParams(dimension_semantics=sem, vmem_limit_bytes=VMEM_LIMIT)


def _dot(a, b):
    return jnp.dot(a, b, preferred_element_type=F32)


def _dot_nt(a, b, precision=None):
    return lax.dot_general(a, b, (((1,), (1,)), ((), ())), precision=precision, preferred_element_type=F32)


def _dot_tn(a, b):
    return lax.dot_general(a, b, (((0,), (0,)), ((), ())), preferred_element_type=F32)


def _layer_norm(z, g, b):
    mu = jnp.mean(z, axis=-1, keepdims=True)
    zc = z - mu
    var = jnp.mean(zc * zc, axis=-1, keepdims=True)
    return zc * lax.rsqrt(var + EPS) * g + b


def _bf16_parts(x, n=3):
    parts = []
    for _ in range(n):
        bits = int(np.float32(x).view(np.uint32))
        rounded = ((bits + 0x7FFF + ((bits >> 16) & 1)) >> 16) << 16
        p = float(np.uint32(rounded & 0xFFFFFFFF).view(np.float32))
        parts.append(p)
        x = x - p
    return parts


def _mm_kernel(x_ref, w_ref, o_ref):
    o_ref[...] = _dot(x_ref[...].astype(BF16), w_ref[...])


def matmul(x, w_bf16, tm=512):
    m, k = x.shape
    n = w_bf16.shape[1]
    return pl.pallas_call(
        _mm_kernel,
        grid=(m // tm,),
        in_specs=[pl.BlockSpec((tm, k), lambda i: (i, 0)), pl.BlockSpec((k, n), lambda i: (0, 0))],
        out_specs=pl.BlockSpec((tm, n), lambda i: (i, 0)),
        out_shape=jax.ShapeDtypeStruct((m, n), F32),
        compiler_params=_params("arbitrary"),
        name="in_proj",
    )(x, w_bf16)


def _outproj_ln_kernel(ap_ref, as_ref, w_ref, r_ref, g_ref, b_ref, o_ref, *, n_p):
    a = jnp.where(pl.program_id(0) < n_p, ap_ref[...], as_ref[...])
    y = _dot(a.astype(BF16), w_ref[...])
    o_ref[...] = _layer_norm(ALPHA * r_ref[...] + y, g_ref[...], b_ref[...])


def outproj_ln(a_p, a_s, w_bf16, resid, g, b, tm=512):
    k = a_p.shape[1]
    n = w_bf16.shape[1]
    n_p = a_p.shape[0] // tm
    m = a_p.shape[0] + a_s.shape[0]
    row = lambda i: (i, 0)
    fixed = lambda i: (0, 0)
    return pl.pallas_call(
        functools.partial(_outproj_ln_kernel, n_p=n_p),
        grid=(m // tm,),
        in_specs=[pl.BlockSpec((tm, k), lambda i: (jnp.minimum(i, n_p - 1), 0)),
                  pl.BlockSpec((tm, k), lambda i: (jnp.maximum(i - n_p, 0), 0)),
                  pl.BlockSpec((k, n), fixed), pl.BlockSpec((tm, n), row),
                  pl.BlockSpec((1, n), fixed), pl.BlockSpec((1, n), fixed)],
        out_specs=pl.BlockSpec((tm, n), row),
        out_shape=jax.ShapeDtypeStruct((m, n), F32),
        compiler_params=_params("arbitrary"),
        name="out_proj_ln",
    )(a_p, a_s, w_bf16, resid, g.reshape(1, n), b.reshape(1, n))


def _log_sigmoid(x):
    return jnp.minimum(x, 0.0) - jnp.log1p(jnp.exp(-jnp.abs(x)))


def _mlstm_kernel(p_ref, c0_ref, n0_ref, m0_ref, bg_ref, ng_ref,
                  hg_ref, co_ref, no_ref, mo_ref, c_s, n_s, m_s, *, L, L_real):
    c = pl.program_id(1)

    @pl.when(c == 0)
    def _():
        c_s[...] = c0_ref[...]
        n_s[...] = n0_ref[...]
        m_s[...] = m0_ref[...]

    gates = p_ref[:, M_GATE_COL:M_IN_PAD] + bg_ref[...]
    lane = lax.broadcasted_iota(jnp.int32, (L, 128), 1)
    x = jnp.where(lane < NH_M, gates, _log_sigmoid(gates))
    if L_real < L:
        tok = lax.broadcasted_iota(jnp.int32, (L, 128), 0)
        x = jnp.where(tok < L_real, x, jnp.where(lane < NH_M, -1e30, 0.0))
    r = lax.broadcasted_iota(jnp.int32, (L, L), 0)
    s = lax.broadcasted_iota(jnp.int32, (L, L), 1)
    causal = r >= s
    tri = causal.astype(F32)
    bcum = jnp.dot(tri, x, precision=HIGHEST, preferred_element_type=F32)
    eye8 = (lax.broadcasted_iota(jnp.int32, (8, 128), 0) == lax.broadcasted_iota(jnp.int32, (8, 128), 1)).astype(F32)
    x_rows = _dot_nt(eye8, x, HIGHEST)
    b_rows = _dot_nt(eye8, bcum, HIGHEST)

    for h in range(NH_M):
        li_row = x_rows[h:h + 1, :]
        li_col = x[:, h:h + 1]
        b_row = b_rows[NH_M + h:NH_M + h + 1, :]
        b_col = bcum[:, NH_M + h:NH_M + h + 1]
        m_old = m_s[:, h:h + 1]
        dm = jnp.where(causal, b_col - b_row + li_row, NEG_INF)
        g_col = b_col + m_old
        mq = jnp.maximum(g_col, jnp.max(dm, axis=1, keepdims=True))
        w_intra = jnp.exp(dm - mq)
        w_inter = jnp.exp(g_col - mq)
        q = p_ref[:, h * DK_M:(h + 1) * DK_M] * (DK_M ** -0.5)
        k = p_ref[:, M_QK + h * DK_M:M_QK + (h + 1) * DK_M]
        v = p_ref[:, 2 * M_QK + h * DV_M:2 * M_QK + (h + 1) * DV_M]
        qb = q.astype(BF16)
        kb = k.astype(BF16)
        cmat = c_s[h]
        nrow = n_s[h:h + 1, :]
        sc = _dot_nt(qb, kb) * w_intra
        num = _dot(sc.astype(BF16), v.astype(BF16)) + w_inter * _dot(qb, cmat.astype(BF16))
        den = jnp.sum(sc, axis=1, keepdims=True) + w_inter * jnp.sum(q * nrow, axis=1, keepdims=True)
        hh = num / jnp.maximum(jnp.abs(den), jnp.exp(-mq))
        bl = b_col[L - 1:L, :]
        lw = bl - b_col + li_col
        m_new = jnp.maximum(bl + m_old, jnp.max(lw, axis=0, keepdims=True))
        wl = jnp.exp(lw - m_new)
        dec = jnp.exp(bl + m_old - m_new)
        c_s[h] = dec * cmat + _dot_tn(kb, (wl * v).astype(BF16))
        n_s[h:h + 1, :] = dec * nrow + jnp.sum(wl * k, axis=0, keepdims=True)
        m_s[:, h:h + 1] = m_new
        mu = jnp.mean(hh, axis=1, keepdims=True)
        hc = hh - mu
        var = jnp.mean(hc * hc, axis=1, keepdims=True)
        hn = hc * lax.rsqrt(var + EPS) * ng_ref[:, h * DV_M:(h + 1) * DV_M]
        o = p_ref[:, 2 * M_QK + M_V + h * DV_M:2 * M_QK + M_V + (h + 1) * DV_M]
        hg_ref[:, h * DV_M:(h + 1) * DV_M] = hn / (1.0 + jnp.exp(-o))

    @pl.when(c == pl.num_programs(1) - 1)
    def _():
        co_ref[...] = c_s[...]
        no_ref[...] = n_s[...]
        mo_ref[...] = m_s[...]


def _mlstm_multi_kernel(p_ref, c0_ref, n0_ref, m0_ref, bg_ref, ng_ref, hg_ref, co_ref, no_ref, mo_ref,
                        c_s, n_s, m_s, *, L, L_real, nseq):
    for s in range(nseq):
        rows = pl.ds(s * L, L)
        _mlstm_kernel(p_ref.at[rows], c0_ref.at[s], n0_ref.at[s], m0_ref.at[s], bg_ref, ng_ref,
                      hg_ref.at[rows], co_ref.at[s], no_ref.at[s], mo_ref.at[s],
                      c_s.at[s], n_s.at[s], m_s.at[s], L=L, L_real=L_real)


def mlstm(p, c0, n0, m0, b_gate_row, norm_g_row, nb, nc, L, L_real, nseq=1):
    assert nseq == 1 or nc == 1
    kern = functools.partial(_mlstm_multi_kernel, L=L, L_real=L_real, nseq=nseq)
    per_b = lambda b, c: (b, 0, 0)
    return pl.pallas_call(
        kern,
        grid=(nb // nseq, nc),
        in_specs=[pl.BlockSpec((nseq * L, M_IN_PAD), lambda b, c: (b * nc + c, 0)),
                  pl.BlockSpec((nseq, NH_M, DK_M, DV_M), lambda b, c: (b, 0, 0, 0)),
                  pl.BlockSpec((nseq, NH_M, DK_M), per_b),
                  pl.BlockSpec((nseq, 1, NH_M), per_b),
                  pl.BlockSpec((1, 128), lambda b, c: (0, 0)),
                  pl.BlockSpec((1, M_V), lambda b, c: (0, 0))],
        out_specs=[pl.BlockSpec((nseq * L, M_V), lambda b, c: (b * nc + c, 0)),
                   pl.BlockSpec((nseq, NH_M, DK_M, DV_M), lambda b, c: (b, 0, 0, 0)),
                   pl.BlockSpec((nseq, NH_M, DK_M), per_b),
                   pl.BlockSpec((nseq, 1, NH_M), per_b)],
        out_shape=[jax.ShapeDtypeStruct((nb * nc * L, M_V), F32),
                   jax.ShapeDtypeStruct((nb, NH_M, DK_M, DV_M), F32),
                   jax.ShapeDtypeStruct((nb, NH_M, DK_M), F32),
                   jax.ShapeDtypeStruct((nb, 1, NH_M), F32)],
        scratch_shapes=[pltpu.VMEM((nseq, NH_M, DK_M, DV_M), F32), pltpu.VMEM((nseq, NH_M, DK_M), F32),
                        pltpu.VMEM((nseq, 1, NH_M), F32)],
        compiler_params=_params("arbitrary", "arbitrary"),
        name=f"mlstm_L{L}",
    )(p, c0, n0, m0.reshape(nb, 1, NH_M), b_gate_row, norm_g_row)


def _topk_mask_lanes(g, k):
    lane = lax.broadcasted_iota(jnp.int32, g.shape, 1)
    n = g.shape[1]
    sel = jnp.zeros(g.shape, F32)
    for _ in range(k):
        mx = jnp.max(g, axis=1, keepdims=True)
        idx = jnp.min(jnp.where(g == mx, lane, n), axis=1, keepdims=True)
        hit = (lane == idx) & (mx > NEG_INF)
        sel = jnp.where(hit, 1.0, sel)
        g = jnp.where(lane == idx, NEG_INF, g)
    return sel


def _moba_select_kernel(q_ref, k_ref, v_ref, ac_ref, qa_ref, ka_ref, vt_ref, kmean_s, *, tq):
    i = pl.program_id(2)

    @pl.when(i == 0)
    def _():
        kmean_s[...] = jnp.zeros_like(kmean_s)
        kmean_s[0:NB_B, :] = jnp.sum(k_ref[...].reshape(NB_B, MOBA_BLOCK, HD_B), axis=1) * (1.0 / MOBA_BLOCK)
        rows = 1024
        lane = lax.broadcasted_iota(jnp.int32, (rows, HD_B), 1)
        rloc = lax.broadcasted_iota(jnp.int32, (rows, HD_B), 0)
        for ch in range(SEQ // rows):
            vt_ref[:, ch * rows:(ch + 1) * rows] = v_ref[ch * rows:(ch + 1) * rows, :].T.astype(BF16)
            row = rloc + ch * rows
            onehot = (lane == row // MOBA_BLOCK).astype(F32)
            offs = (row % MOBA_BLOCK).astype(F32)
            blks = (row // MOBA_BLOCK).astype(F32)
            pat = jnp.where(lane < NB_B, onehot,
                            jnp.where(lane < POS_LANE + 3, offs, jnp.where(lane < POS_LANE + 6, blks, 0.0)))
            ka_ref[ch * rows:(ch + 1) * rows, 0:HD_B] = k_ref[ch * rows:(ch + 1) * rows, :].astype(BF16)
            ka_ref[ch * rows:(ch + 1) * rows, HD_B:AUG] = pat.astype(BF16)

    km = kmean_s[...]
    lane = lax.broadcasted_iota(jnp.int32, (tq, HD_B), 1)
    own = (i * tq + lax.broadcasted_iota(jnp.int32, (tq, HD_B), 0)) // MOBA_BLOCK
    for g in range(G_B):
        q = q_ref[:, g * HD_B:(g + 1) * HD_B]
        gate = _dot_nt(q * (HD_B ** -0.5), km, HIGHEST)
        gate = jnp.where(lane < own, gate, NEG_INF)
        sel = _topk_mask_lanes(gate, MOBA_TOPK)
        sel = jnp.where(lane == own, 1.0, sel)
        aug = jnp.where(lane < NB_B, jnp.where(sel > 0.5, 0.0, MASKED), ac_ref[g, 0:1, :])
        qa_ref[:, g * AUG:g * AUG + HD_B] = (q * (HD_B ** -0.5 * LOG2E)).astype(BF16)
        qa_ref[:, g * AUG + HD_B:(g + 1) * AUG] = aug.astype(BF16)


def moba_select(p, aug_const, tq=1024):
    kern = functools.partial(_moba_select_kernel, tq=tq)
    nq = SEQ // tq
    per_bc = lambda b, c, i: (b, c, 0, 0)
    return pl.pallas_call(
        kern,
        grid=(BATCH, KVH_B, nq),
        in_specs=[pl.BlockSpec((tq, G_B * HD_B), lambda b, c, i: (b * nq + i, c)),
                  pl.BlockSpec((SEQ, HD_B), lambda b, c, i: (b, NH_B + c)),
                  pl.BlockSpec((SEQ, HD_B), lambda b, c, i: (b, NH_B + KVH_B + c)),
                  pl.BlockSpec((G_B, 8, 128), lambda b, c, i: (c, 0, 0))],
        out_specs=[pl.BlockSpec((tq, G_B * AUG), lambda b, c, i: (b * nq + i, c)),
                   pl.BlockSpec((None, None, SEQ, AUG), per_bc),
                   pl.BlockSpec((None, None, HD_B, SEQ), per_bc)],
        out_shape=[jax.ShapeDtypeStruct((NP, NH_B * AUG), BF16),
                   jax.ShapeDtypeStruct((BATCH, KVH_B, SEQ, AUG), BF16),
                   jax.ShapeDtypeStruct((BATCH, KVH_B, HD_B, SEQ), BF16)],
        scratch_shapes=[pltpu.VMEM((128, HD_B), F32)],
        compiler_params=_params("arbitrary", "arbitrary", "arbitrary"),
        name="moba_select",
    )(p, p, p, aug_const)


def _moba_flash_kernel(qa_ref, ka_ref, vt_ref, o_ref, s_buf):
    i = pl.program_id(2)
    bq = MOBA_QTILE
    tk = MOBA_QTILE
    nl = G_B * bq
    qa = jnp.concatenate([qa_ref[:, g * AUG:(g + 1) * AUG] for g in range(G_B)], axis=0)

    def scores(t, slot):
        off = pl.multiple_of(t * tk, tk)
        s_buf[slot] = _dot_nt(ka_ref[pl.ds(off, tk), :], qa)

    def softmax_pv(t, slot, last, carry):
        m, l, acc = carry
        st = s_buf[slot]
        if last:
            kpos = t * tk + lax.broadcasted_iota(jnp.int32, (tk, nl), 0)
            qpos = i * bq + lax.broadcasted_iota(jnp.int32, (tk, nl), 1) % bq
            st = jnp.where(kpos <= qpos, st, NEG_INF)
        m_new = jnp.maximum(m, jnp.max(st, axis=0, keepdims=True))
        pt = jnp.exp2(st - m_new)
        a = jnp.exp2(m - m_new)
        l = a * l + jnp.sum(pt, axis=0, keepdims=True)
        off = pl.multiple_of(t * tk, tk)
        acc = a * acc + _dot(vt_ref[:, pl.ds(off, tk)], pt.astype(BF16))
        return m_new, l, acc

    def pair(u, carry):
        scores(2 * u + 1, 1)
        carry = softmax_pv(2 * u, 0, False, carry)
        scores(2 * u + 2, 0)
        return softmax_pv(2 * u + 1, 1, False, carry)

    def tail_two(carry):
        t = 2 * npairs
        scores(t + 1, 1)
        return softmax_pv(t + 1, 1, True, softmax_pv(t, 0, False, carry))

    def tail_one(carry):
        return softmax_pv(2 * npairs, 0, True, carry)

    n_full = i
    npairs = n_full // 2
    init = (jnp.full((1, nl), MASKED, F32), jnp.zeros((1, nl), F32), jnp.zeros((HD_B, nl), F32))
    scores(0, 0)
    carry = lax.fori_loop(0, npairs, pair, init)
    m, l, acc = lax.cond(n_full % 2 == 1, tail_two, tail_one, carry)
    out = acc / l
    for g in range(G_B):
        o_ref[:, g * HD_B:(g + 1) * HD_B] = out[:, g * bq:(g + 1) * bq].T


def moba_flash(q_aug, k_aug, v_t):
    nq = SEQ // MOBA_QTILE
    per_bc = lambda b, c, i: (b, c, 0, 0)
    return pl.pallas_call(
        _moba_flash_kernel,
        grid=(BATCH, KVH_B, nq),
        in_specs=[pl.BlockSpec((MOBA_QTILE, G_B * AUG), lambda b, c, i: (b * nq + i, c)),
                  pl.BlockSpec((None, None, SEQ, AUG), per_bc),
                  pl.BlockSpec((None, None, HD_B, SEQ), per_bc)],
        out_specs=pl.BlockSpec((MOBA_QTILE, G_B * HD_B), lambda b, c, i: (b * nq + i, c)),
        out_shape=jax.ShapeDtypeStruct((NP, NH_B * HD_B), F32),
        scratch_shapes=[pltpu.VMEM((2, MOBA_QTILE, G_B * MOBA_QTILE), F32)],
        compiler_params=_params("arbitrary", "arbitrary", "arbitrary"),
        name="moba_flash",
    )(q_aug, k_aug, v_t)


def _moba_aug_const():
    c = np.zeros((NH_B, 8, 128), np.float32)
    for h in range(NH_B):
        beta = 2.0 ** (-8.0 * (h + 1) / NH_B) * LOG2E
        c[h, 0, POS_LANE:POS_LANE + 3] = _bf16_parts(beta)
        c[h, 0, POS_LANE + 3:POS_LANE + 6] = _bf16_parts(beta * MOBA_BLOCK)
    return jnp.asarray(c)


def _moba_sample_kernel(pt_ref, qa_ref, qat_ref, kn_ref, vn_ref, bias_ref, ck_ref, cv_ref,
                        o_ref, buf, sem, s_s, p_s, ksum_s, acc_s, inv_s):
    npg = SAMPLE_PAGES_PER_STEP
    w = KVH_B * HD_B
    nrow = KVH_B * DEC_SEQ * G_B

    def load_page(sl, pg):
        return jnp.concatenate([buf[sl, pg, pl.ds(c, PAGE_SIZE, stride=KVH_B), :] for c in range(KVH_B)], axis=1)

    b = pl.program_id(0)
    t = pl.program_id(1)
    nb = pl.num_programs(0)
    g = b * 8 + t
    slot = g % 2

    def page_copy(src_ref, page, sl, pg):
        return pltpu.make_async_copy(src_ref.at[page], buf.at[sl, pg], sem.at[sl])

    def start_fetch(bn, tn, sl):
        qn = tn % 4

        @pl.when(tn < 4)
        def _():
            for pg in range(npg):
                page_copy(ck_ref, pt_ref[bn, qn * npg + pg], sl, pg).start()

        @pl.when(tn >= 4)
        def _():
            for pg in range(npg):
                page_copy(cv_ref, pt_ref[bn, qn * npg + pg], sl, pg).start()

    @pl.when(g == 0)
    def _():
        start_fetch(b, t, slot)

    @pl.when(g + 1 < nb * 8)
    def _():
        tn = (t + 1) % 8
        bn = b + (t + 1) // 8
        start_fetch(bn, tn, 1 - slot)

    for pg in range(npg):
        page_copy(ck_ref, 0, slot, pg).wait()

    qab = (qa_ref[...] * LOG2E).astype(BF16)

    @pl.when(g == 0)
    def _():
        p_s[...] = jnp.zeros_like(p_s)

    @pl.when(t < 4)
    def _():
        for pg in range(npg):
            page = load_page(slot, pg)
            col0 = pl.multiple_of((t * npg + pg) * PAGE_SIZE, PAGE_SIZE)
            s_s[:, pl.ds(col0, PAGE_SIZE)] = _dot(page.astype(BF16), qab).T
            csum = jnp.sum(page, axis=0, keepdims=True)
            if pg % 2 == 0:
                prev = csum
            else:
                ksum_s[pl.ds(t * (npg // 2) + pg // 2, 1), :] = prev + csum

    @pl.when(t == 3)
    def _():
        qat = qat_ref[...]
        kmean = ksum_s[...] * (1.0 / MOBA_BLOCK)
        sel = _topk_mask_lanes(_dot_nt(qat, kmean, HIGHEST), MOBA_TOPK)
        kn = jnp.concatenate([kn_ref[...], jnp.zeros((PAGE_SIZE - 8, w), F32)], axis=0)
        s_s[0:nrow, PAST:PAST + PAGE_SIZE] = _dot_nt((qat * LOG2E).astype(BF16), kn.astype(BF16))
        selm = jnp.where(sel > 0.5, 0.0, NEG_INF)

        def logits(bk):
            if bk == NB_S:
                return s_s[0:nrow, PAST:PAST + PAGE_SIZE] + bias_ref[:, PAST:PAST + PAGE_SIZE]
            lo = bk * MOBA_BLOCK
            return s_s[0:nrow, lo:lo + MOBA_BLOCK] + bias_ref[:, lo:lo + MOBA_BLOCK] + selm[:, bk:bk + 1]

        macc = logits(0)
        for bk in range(1, NB_S):
            macc = jnp.maximum(macc, logits(bk))
        mx = jnp.maximum(jnp.max(macc, axis=1, keepdims=True), jnp.max(logits(NB_S), axis=1, keepdims=True))
        dacc = jnp.zeros((nrow, MOBA_BLOCK), F32)
        for bk in range(NB_S):
            pr = jnp.exp2(logits(bk) - mx)
            p_s[0:nrow, bk * MOBA_BLOCK:(bk + 1) * MOBA_BLOCK] = pr
            dacc = dacc + pr
        pr = jnp.exp2(logits(NB_S) - mx)
        p_s[0:nrow, PAST:PAST + PAGE_SIZE] = pr
        den = jnp.sum(dacc, axis=1, keepdims=True) + jnp.sum(pr, axis=1, keepdims=True)
        inv_s[...] = jnp.zeros_like(inv_s)
        inv_s[0:nrow, :] = jnp.broadcast_to(1.0 / den, (nrow, 128))
        inv_s[...] = inv_s[...].T

    @pl.when(t == 4)
    def _():
        vn = jnp.concatenate([vn_ref[...], jnp.zeros((PAGE_SIZE - 8, w), F32)], axis=0)
        acc_s[...] = _dot_tn(vn.astype(BF16), p_s[:, PAST:PAST + PAGE_SIZE].T.astype(BF16))

    @pl.when(t >= 4)
    def _():
        acc = acc_s[...]
        for pg in range(npg):
            col0 = pl.multiple_of(((t - 4) * npg + pg) * PAGE_SIZE, PAGE_SIZE)
            pt = p_s[:, pl.ds(col0, PAGE_SIZE)].T
            acc = acc + _dot_tn(load_page(slot, pg).astype(BF16), pt.astype(BF16))
        acc_s[...] = acc

    @pl.when(t == 7)
    def _():
        o_ref[...] = acc_s[:, 0:nrow] * inv_s[0:1, 0:nrow]


def moba_sample(page_table, q_aug, q_aug_t, k_new, v_new, bias, cache_k, cache_v):
    w = KVH_B * HD_B
    nrow = KVH_B * DEC_SEQ * G_B
    nkeys = PAST + PAGE_SIZE
    per_b3 = lambda b, t, pt: (b, 0, 0)
    fixed = lambda b, t, pt: (0, 0)
    grid_spec = pltpu.PrefetchScalarGridSpec(
        num_scalar_prefetch=1,
        grid=(DEC_BATCH, 8),
        in_specs=[pl.BlockSpec((None, w, 128), per_b3),
                  pl.BlockSpec((None, nrow, w), per_b3),
                  pl.BlockSpec((None, 8, w), per_b3),
                  pl.BlockSpec((None, 8, w), per_b3),
                  pl.BlockSpec((nrow, nkeys), fixed),
                  pl.BlockSpec(memory_space=pl.ANY),
                  pl.BlockSpec(memory_space=pl.ANY)],
        out_specs=pl.BlockSpec((None, w, nrow), per_b3),
        scratch_shapes=[pltpu.VMEM((2, SAMPLE_PAGES_PER_STEP, PAGE_SIZE * KVH_B, HD_B), F32),
                        pltpu.SemaphoreType.DMA((2,)),
                        pltpu.VMEM((128, nkeys), F32),
                        pltpu.VMEM((128, nkeys), F32),
                        pltpu.VMEM((NB_S, w), F32),
                        pltpu.VMEM((w, 128), F32),
                        pltpu.VMEM((128, 128), F32)],
    )
    return pl.pallas_call(
        _moba_sample_kernel,
        grid_spec=grid_spec,
        out_shape=jax.ShapeDtypeStruct((DEC_BATCH, w, nrow), F32),
        compiler_params=_params("arbitrary", "arbitrary"),
        name="moba_sample",
    )(page_table, q_aug, q_aug_t, k_new, v_new, bias, cache_k, cache_v)


def _swa_slope(h):
    return 2.0 ** (-8.0 * (h + 1) / NH_C)


def _by_head(gidx, values):
    out = values[-1]
    for g in range(len(values) - 2, -1, -1):
        out = jnp.where(gidx == g, values[g], out)
    return out


def _swa_kv_aug(k_all, v_all, c):
    nk = k_all.shape[0]
    lane = lax.broadcasted_iota(jnp.int32, (nk, HD_C), 1)
    kidx = lax.broadcasted_iota(jnp.int32, (nk, HD_C), 0).astype(F32)
    kpat = jnp.where(lane < 3, kidx, 0.0)
    ka = jnp.concatenate([k_all[:, c * HD_C:(c + 1) * HD_C], kpat], axis=1).astype(BF16)
    return ka, v_all[:, c * HD_C:(c + 1) * HD_C].astype(BF16)


def _swa_softmax_pv(qa, ka, vc, maskt, sink2):
    s2 = _dot_nt(qa, ka) + maskt
    mx = jnp.maximum(jnp.max(s2, axis=1, keepdims=True), sink2)
    pr = jnp.exp2(s2 - mx)
    den = jnp.sum(pr, axis=1, keepdims=True) + jnp.exp2(sink2 - mx)
    return _dot(pr.astype(BF16), vc) / den


def _store_heads(o_ref, outs, lead=()):
    for j in range(0, NH_C, 2):
        o_ref[lead + (slice(None), slice(j * HD_C, (j + 2) * HD_C))] = jnp.concatenate(outs[j:j + 2], axis=1)


def _swa_prompt_kernel(q_ref, kvp_ref, kvc_ref, sink_ref, o_ref):
    i = pl.program_id(1)
    w = WINDOW
    kw = KVH_C * HD_C
    k_all = jnp.concatenate([kvp_ref[:, 0:kw], kvc_ref[:, 0:kw]], axis=0)
    v_all = jnp.concatenate([kvp_ref[:, kw:2 * kw], kvc_ref[:, kw:2 * kw]], axis=0)
    kk = lax.broadcasted_iota(jnp.int32, (2 * w, w), 0)
    qq = lax.broadcasted_iota(jnp.int32, (2 * w, w), 1)
    di = qq - kk + w
    ok = (di >= 0) & (di <= w) & ((kk >= w) | (i > 0))
    mask1 = jnp.where(ok, 0.0, NEG_INF)
    maskt = jnp.concatenate([mask1] * G_C, axis=1)
    qdist = (lax.broadcasted_iota(jnp.int32, (1, w), 1) + w).astype(F32)
    qlane = lax.broadcasted_iota(jnp.int32, (w, HD_C), 1)
    sink = sink_ref[...]
    for c in range(KVH_C):
        ka, vc = _swa_kv_aug(k_all, v_all, c)
        qas, sinks = [], []
        for gi in range(G_C):
            h = c * G_C + gi
            b1, b2, b3 = _bf16_parts(_swa_slope(h) * LOG2E)
            qpat = jnp.where(qlane == 0, b1, jnp.where(qlane == 1, b2, jnp.where(qlane == 2, b3, 0.0)))
            qa = jnp.concatenate([q_ref[:, h * HD_C:(h + 1) * HD_C] * (HD_C ** -0.5 * LOG2E), qpat], axis=1)
            qas.append(qa.astype(BF16))
            sinks.append((sink[:, h:h + 1] + _swa_slope(h) * qdist) * LOG2E)
        sink2 = jnp.concatenate(sinks, axis=1)
        st = _dot_nt(ka, jnp.concatenate(qas, axis=0)) + maskt
        mx = jnp.maximum(jnp.max(st, axis=0, keepdims=True), sink2)
        pt = jnp.exp2(st - mx)
        den = jnp.sum(pt, axis=0, keepdims=True) + jnp.exp2(sink2 - mx)
        ot = _dot_tn(vc, pt.astype(BF16)) / den
        for gi in range(0, G_C, 2):
            h = c * G_C + gi
            two = jnp.concatenate([ot[:, gi * w:(gi + 1) * w], ot[:, (gi + 1) * w:(gi + 2) * w]], axis=0)
            o_ref[:, h * HD_C:(h + 2) * HD_C] = two.T


def swa_prompt(p, sink_row):
    nblk = SEQ // WINDOW
    kvb = NH_C * HD_C // (2 * KVH_C * HD_C)
    return pl.pallas_call(
        _swa_prompt_kernel,
        grid=(BATCH, nblk),
        in_specs=[pl.BlockSpec((WINDOW, NH_C * HD_C), lambda b, i: (b * nblk + i, 0)),
                  pl.BlockSpec((WINDOW, 2 * KVH_C * HD_C), lambda b, i: (b * nblk + jnp.maximum(i - 1, 0), kvb)),
                  pl.BlockSpec((WINDOW, 2 * KVH_C * HD_C), lambda b, i: (b * nblk + i, kvb)),
                  pl.BlockSpec((1, 128), lambda b, i: (0, 0))],
        out_specs=pl.BlockSpec((WINDOW, NH_C * HD_C), lambda b, i: (b * nblk + i, 0)),
        out_shape=jax.ShapeDtypeStruct((NP, NH_C * HD_C), F32),
        compiler_params=_params("arbitrary", "arbitrary"),
        name="swa_prompt",
    )(p, p, p, sink_row)


def _swa_sample_kernel(q_ref, kn_ref, vn_ref, ck_ref, cv_ref, sink_ref, o_ref):
    w = WINDOW
    nk = w + 8
    nr = G_C * 8
    r = lax.broadcasted_iota(jnp.int32, (nr, nk), 0)
    cc = lax.broadcasted_iota(jnp.int32, (nr, nk), 1)
    di = w + r % 8 - cc
    ok = (di >= 0) & (di <= w) & (cc < w + DEC_SEQ)
    maskt = jnp.where(ok, 0.0, NEG_INF)
    rcol = lax.broadcasted_iota(jnp.int32, (nr, 1), 0)
    gcol = rcol // 8
    qdist = (w + rcol % 8).astype(F32)
    qlane = lax.broadcasted_iota(jnp.int32, (nr, HD_C), 1)
    sink = sink_ref[...]
    qpats, sink2s = [], []
    for c in range(KVH_C):
        heads = [c * G_C + gi for gi in range(G_C)]
        parts = [_bf16_parts(_swa_slope(h) * LOG2E) for h in heads]
        b = [_by_head(gcol, [pp[j] for pp in parts]) for j in range(3)]
        qpats.append(jnp.where(qlane == 0, b[0], jnp.where(qlane == 1, b[1], jnp.where(qlane == 2, b[2], 0.0))))
        slope = _by_head(gcol, [_swa_slope(h) for h in heads])
        sk = _by_head(gcol, [sink[:, h:h + 1] for h in heads])
        sink2s.append((sk + slope * qdist) * LOG2E)
    for sq in range(SWA_SEQ_PER_STEP):
        q = q_ref[sq] * (HD_C ** -0.5 * LOG2E)
        k_all = jnp.concatenate([ck_ref[sq], kn_ref[sq]], axis=0)
        v_all = jnp.concatenate([cv_ref[sq], vn_ref[sq]], axis=0)
        outs = []
        for c in range(KVH_C):
            ka, vc = _swa_kv_aug(k_all, v_all, c)
            qc = jnp.concatenate([q[:, (c * G_C + gi) * HD_C:(c * G_C + gi + 1) * HD_C] for gi in range(G_C)], axis=0)
            qa = jnp.concatenate([qc, qpats[c]], axis=1).astype(BF16)
            oc = _swa_softmax_pv(qa, ka, vc, maskt, sink2s[c])
            outs.extend(oc[gi * 8:(gi + 1) * 8, :] for gi in range(G_C))
        _store_heads(o_ref, outs, lead=(sq,))


def swa_sample(q, k_new, v_new, cache_k, cache_v, sink_row):
    n = SWA_SEQ_PER_STEP
    kw = KVH_C * HD_C
    blk = lambda i: (i, 0, 0)
    return pl.pallas_call(
        _swa_sample_kernel,
        grid=(DEC_BATCH // n,),
        in_specs=[pl.BlockSpec((n, 8, NH_C * HD_C), blk), pl.BlockSpec((n, 8, kw), blk), pl.BlockSpec((n, 8, kw), blk),
                  pl.BlockSpec((n, WINDOW, kw), blk), pl.BlockSpec((n, WINDOW, kw), blk),
                  pl.BlockSpec((1, 128), lambda i: (0, 0))],
        out_specs=pl.BlockSpec((n, 8, NH_C * HD_C), blk),
        out_shape=jax.ShapeDtypeStruct((DEC_BATCH, 8, NH_C * HD_C), F32),
        compiler_params=_params("arbitrary"),
        name="swa_sample",
    )(q, k_new, v_new, cache_k, cache_v, sink_row)


def _router_kernel(x_ref, wt_ref, bt_ref, meta_ref, wts_ref, seg_ref, tri_s, *, tm):
    @pl.when(pl.program_id(0) == 0)
    def _():
        rr = lax.broadcasted_iota(jnp.int32, (128, tm), 0)
        ccn = lax.broadcasted_iota(jnp.int32, (128, tm), 1)
        for ch in range(tm // 128):
            tri_s[ch * 128:(ch + 1) * 128, :] = (rr + ch * 128 < ccn).astype(BF16)

    lt = _dot_nt(wt_ref[...], x_ref[...], HIGHEST) + bt_ref[...]
    row = lax.broadcasted_iota(jnp.int32, lt.shape, 0)
    big = ROUTER_ROWS

    lg = jnp.where(row < N_GROUPS, lt, NEG_INF)
    mg = jnp.max(lg, axis=0, keepdims=True)
    eg = jnp.exp(lg - mg)
    pg = eg / jnp.sum(eg, axis=0, keepdims=True)
    pg1 = jnp.max(pg, axis=0, keepdims=True)
    g1 = jnp.min(jnp.where(pg == pg1, row, big), axis=0, keepdims=True)

    e = row - N_GROUPS
    ingroup = (e >= g1 * N_EXP) & (e < (g1 + 1) * N_EXP)
    le = jnp.where(ingroup, lt, NEG_INF)
    me = jnp.max(le, axis=0, keepdims=True)
    ee = jnp.exp(le - me)
    pe = ee / jnp.sum(ee, axis=0, keepdims=True)
    pe = jnp.where(ingroup, pe, NEG_INF)
    p1 = jnp.max(pe, axis=0, keepdims=True)
    i1 = jnp.min(jnp.where(pe == p1, row, big), axis=0, keepdims=True)
    pe2 = jnp.where(row == i1, NEG_INF, pe)
    p2 = jnp.max(pe2, axis=0, keepdims=True)
    i2 = jnp.min(jnp.where(pe2 == p2, row, big), axis=0, keepdims=True)
    tot = p1 + p2
    w1 = p1 / tot * pg1
    w2 = p2 / tot * pg1

    hit1 = row == i1
    hit2 = row == i2
    oh = jnp.where(hit1 | hit2, 1.0, 0.0)
    ohb = oh.astype(BF16)
    before = _dot(ohb, tri_s[...])
    cnt_col = jnp.sum(oh, axis=1, keepdims=True)
    pad_col = jnp.floor((cnt_col + 7.0) * 0.125) * 8.0
    r128 = lax.broadcasted_iota(jnp.int32, (128, 128), 0)
    c128 = lax.broadcasted_iota(jnp.int32, (128, 128), 1)
    off_col = jnp.dot((c128 < r128).astype(F32), pad_col + jnp.zeros((128, 128), F32),
                      precision=HIGHEST, preferred_element_type=F32)[:, 0:1]
    place = before + off_col
    pos1 = jnp.sum(jnp.where(hit1, place, 0.0), axis=0, keepdims=True)
    pos2 = jnp.sum(jnp.where(hit2, place, 0.0), axis=0, keepdims=True)
    cnt_row = _dot_nt(jnp.ones((8, tm), BF16), ohb)
    pad_row = jnp.floor((cnt_row + 7.0) * 0.125) * 8.0
    off_row = jnp.dot(pad_row, (r128 < c128).astype(F32), precision=HIGHEST, preferred_element_type=F32)

    r8 = lax.broadcasted_iota(jnp.int32, (8, tm), 0)
    meta_ref[...] = jnp.where(r8 == 0, pos1, jnp.where(r8 == 1, pos2, 0.0)).astype(I32)
    wts_ref[...] = jnp.where(r8 == 0, w1, jnp.where(r8 == 1, w2, 0.0))
    s8 = lax.broadcasted_iota(jnp.int32, (8, 128), 0)
    seg_ref[...] = jnp.where(s8 == 0, off_row, jnp.where(s8 == 1, cnt_row, 0.0)).astype(I32)


def moe_router(x, wt_router, bt_router, tm=MOE_TM):
    m = x.shape[0]
    nt = m // tm
    blk = lambda i: (i, 0, 0)
    return pl.pallas_call(
        functools.partial(_router_kernel, tm=tm),
        grid=(nt,),
        in_specs=[pl.BlockSpec((tm, D_MODEL), lambda i: (i, 0)),
                  pl.BlockSpec((ROUTER_ROWS, D_MODEL), lambda i: (0, 0)),
                  pl.BlockSpec((ROUTER_ROWS, 1), lambda i: (0, 0))],
        out_specs=[pl.BlockSpec((None, 8, tm), blk), pl.BlockSpec((None, 8, tm), blk),
                   pl.BlockSpec((None, 8, 128), blk)],
        out_shape=[jax.ShapeDtypeStruct((nt, 8, tm), I32), jax.ShapeDtypeStruct((nt, 8, tm), F32),
                   jax.ShapeDtypeStruct((nt, 8, 128), I32)],
        scratch_shapes=[pltpu.VMEM((tm, tm), BF16)],
        compiler_params=_params("arbitrary"),
        name="moe_router",
    )(x, wt_router, bt_router)


def _moe_kernel(x_ref, meta_hbm, wts_hbm, seg_hbm, wg_hbm, wu_hbm, wd_hbm, g_ref, b_ref, o_ref,
                xs, wbg, wbu, wbd, wsem, meta_s, wts_s, seg_s, msem, *, layer, tm):
    i = pl.program_id(0)
    nt = pl.num_programs(0)
    rc = MOE_RC
    ahead = MOE_WSLOTS - 1
    base = i * N_EXPERTS
    ms = i % 2

    def weight_copies(step):
        e = step % N_EXPERTS
        sl = step % MOE_WSLOTS
        return (pltpu.make_async_copy(wg_hbm.at[layer, e], wbg.at[sl], wsem.at[sl, 0]),
                pltpu.make_async_copy(wu_hbm.at[layer, e], wbu.at[sl], wsem.at[sl, 1]),
                pltpu.make_async_copy(wd_hbm.at[layer, e], wbd.at[sl], wsem.at[sl, 2]))

    def meta_copies(tile, sl):
        return (pltpu.make_async_copy(meta_hbm.at[tile], meta_s.at[sl], msem.at[sl, 0]),
                pltpu.make_async_copy(wts_hbm.at[tile], wts_s.at[sl], msem.at[sl, 1]),
                pltpu.make_async_copy(seg_hbm.at[tile], seg_s.at[sl], msem.at[sl, 2]))

    @pl.when(i == 0)
    def _():
        for cp in meta_copies(0, 0):
            cp.start()
        for step in range(ahead):
            for cp in weight_copies(step):
                cp.start()
        xs[...] = jnp.zeros_like(xs)

    for cp in meta_copies(i, ms):
        cp.wait()

    @pl.when(i + 1 < nt)
    def _():
        for cp in meta_copies(i + 1, 1 - ms):
            cp.start()

    def gather(t, carry):
        rowv = x_ref[pl.ds(t, 1), :]
        xs[pl.ds(meta_s[ms, 0, t], 1), :] = rowv
        xs[pl.ds(meta_s[ms, 1, t], 1), :] = rowv
        return carry

    lax.fori_loop(0, tm, gather, 0, unroll=8)

    def expert(e, carry):
        step = base + e
        sl = step % MOE_WSLOTS
        for cp in weight_copies(step):
            cp.wait()

        @pl.when(step + ahead < nt * N_EXPERTS)
        def _():
            for cp in weight_copies(step + ahead):
                cp.start()

        off = seg_s[ms, 0, N_GROUPS + e]
        cnt = seg_s[ms, 1, N_GROUPS + e]
        wgb = wbg[sl]
        wub = wbu[sl]
        wdb = wbd[sl]
        ridx = lax.broadcasted_iota(jnp.int32, (rc, 1), 0)

        def chunk(k, c2):
            row0 = pl.multiple_of(off + k * rc, 8)
            lhs = xs[pl.ds(row0, rc), :]
            lb = lhs.astype(BF16)
            hg = _dot(lb, wgb)
            hu = _dot(lb, wub)
            hh = hg / (1.0 + jnp.exp(-hg)) * hu
            out = _dot(hh.astype(BF16), wdb)
            xs[pl.ds(row0, rc), :] = jnp.where(ridx < cnt - k * rc, out, lhs)
            return c2

        lax.fori_loop(0, (cnt + rc - 1) // rc, chunk, 0)
        return carry

    lax.fori_loop(0, N_EXPERTS, expert, 0)

    def combine(t, carry):
        y = (wts_s[ms, 0, t] * xs[pl.ds(meta_s[ms, 0, t], 1), :]
             + wts_s[ms, 1, t] * xs[pl.ds(meta_s[ms, 1, t], 1), :])
        o_ref[pl.ds(t, 1), :] = y
        return carry

    lax.fori_loop(0, tm, combine, 0, unroll=8)
    o_ref[...] = _layer_norm(ALPHA * x_ref[...] + o_ref[...], g_ref[...], b_ref[...])


def moe_ffn_ln(x, meta, wts, seg, wg, wu, wd, g, b, layer, tm=MOE_TM):
    m = x.shape[0]
    row = lambda i: (i, 0)
    fixed = lambda i: (0, 0)
    hbm = pl.BlockSpec(memory_space=pl.ANY)
    return pl.pallas_call(
        functools.partial(_moe_kernel, layer=layer, tm=tm),
        grid=(m // tm,),
        in_specs=[pl.BlockSpec((tm, D_MODEL), row), hbm, hbm, hbm, hbm, hbm, hbm,
                  pl.BlockSpec((1, D_MODEL), fixed), pl.BlockSpec((1, D_MODEL), fixed)],
        out_specs=pl.BlockSpec((tm, D_MODEL), row),
        out_shape=jax.ShapeDtypeStruct((m, D_MODEL), F32),
        scratch_shapes=[pltpu.VMEM((MOE_ROWS, D_MODEL), F32),
                        pltpu.VMEM((MOE_WSLOTS, D_MODEL, D_FF), BF16), pltpu.VMEM((MOE_WSLOTS, D_MODEL, D_FF), BF16),
                        pltpu.VMEM((MOE_WSLOTS, D_FF, D_MODEL), BF16), pltpu.SemaphoreType.DMA((MOE_WSLOTS, 3)),
                        pltpu.SMEM((2, 8, tm), I32), pltpu.SMEM((2, 8, tm), F32), pltpu.SMEM((2, 8, 128), I32),
                        pltpu.SemaphoreType.DMA((2, 3))],
        compiler_params=_params("arbitrary"),
        name="moe_ffn_ln",
    )(x, meta, wts, seg, wg, wu, wd, g.reshape(1, D_MODEL), b.reshape(1, D_MODEL))


def _pad_rows(a, rows):
    return jnp.pad(a, ((0, 0), (0, rows - a.shape[1]), (0, 0)))


def _mlstm_layer(x, w_in, b_gate, norm_g, w_out, c0, n0, m0):
    w = jnp.pad(w_in, ((0, 0), (0, M_IN_PAD - M_IN))).astype(BF16)
    p = matmul(x, w)
    bg = jnp.pad(b_gate, (0, 128 - 2 * NH_M)).reshape(1, 128)
    ng = norm_g.reshape(1, M_V)
    nc = SEQ // M_CHUNK
    hp, cp, np_, mp = mlstm(p, jnp.zeros((BATCH, NH_M, DK_M, DV_M), F32), jnp.zeros((BATCH, NH_M, DK_M), F32),
                            jnp.zeros((BATCH, NH_M), F32), bg, ng, BATCH, nc, M_CHUNK, M_CHUNK)
    ps = _pad_rows(p[NP:].reshape(DEC_BATCH, DEC_SEQ, M_IN_PAD), 8).reshape(DEC_BATCH * 8, M_IN_PAD)
    hs, cs, ns, ms = mlstm(ps, c0, n0, m0, bg, ng, DEC_BATCH, 1, 8, DEC_SEQ, nseq=MLSTM_SEQ_PER_STEP)
    hs = hs.reshape(DEC_BATCH, 8, M_V)[:, :DEC_SEQ].reshape(NS, M_V)
    states = (cp, np_, mp.reshape(BATCH, NH_M), cs, ns, ms.reshape(DEC_BATCH, NH_M))
    return (hp, hs), w_out.astype(BF16), states


def _moba_sample_part(ps, cache_k, cache_v, page_rows):
    hq = NH_B * HD_B
    hk = KVH_B * HD_B
    slopes_h = jnp.exp2(-8.0 * jnp.arange(1, NH_B + 1, dtype=F32) / NH_B)
    q = (ps[:, :hq] * (HD_B ** -0.5)).reshape(DEC_BATCH, DEC_SEQ, KVH_B, G_B, HD_B)
    eye = jnp.eye(KVH_B, dtype=F32)
    nrow = KVH_B * DEC_SEQ * G_B
    q_aug_t = jnp.einsum("bscgd,ce->besgcd", q, eye).reshape(DEC_BATCH, nrow, hk)
    q_aug = jnp.pad(jnp.swapaxes(q_aug_t, 1, 2), ((0, 0), (0, 0), (0, 128 - nrow)))
    k_new = ps[:, hq:hq + hk].reshape(DEC_BATCH, DEC_SEQ, hk)
    v_new = ps[:, hq + hk:].reshape(DEC_BATCH, DEC_SEQ, hk)
    row = jnp.arange(nrow)
    row_c, row_s, row_g = row // (DEC_SEQ * G_B), (row // G_B) % DEC_SEQ, row % G_B
    slope_row = (slopes_h[row_c * G_B + row_g] * LOG2E)[:, None]
    qs_row = row_s.astype(F32)[:, None]
    past_bias = -slope_row * (float(PAST) + qs_row - jnp.arange(PAST, dtype=F32)[None, :])
    tok = jnp.arange(PAGE_SIZE, dtype=F32)[None, :]
    own_bias = jnp.where((tok <= qs_row) & (tok < float(DEC_SEQ)), -slope_row * (qs_row - tok), NEG_INF)
    bias = jnp.concatenate([past_bias, own_bias], axis=1)
    ot = moba_sample(page_rows, q_aug, q_aug_t, _pad_rows(k_new, 8), _pad_rows(v_new, 8), bias, cache_k, cache_v)
    ot = ot.reshape(DEC_BATCH, KVH_B, HD_B, KVH_B, DEC_SEQ, G_B)
    os_ = jnp.einsum("bcdesg,ce->bscgd", ot, eye).reshape(NS, hq)
    return os_, k_new, v_new


def _moba_layer(x, w_in, w_out, cache_k, cache_v, page_rows):
    p = matmul(x, w_in.astype(BF16))
    hq = NH_B * HD_B
    hk = KVH_B * HD_B
    ac = _moba_aug_const()
    q_aug, k_aug, v_t = moba_select(p, ac)
    op = moba_flash(q_aug, k_aug, v_t)
    os_, k_new, v_new = _moba_sample_part(p[NP:], cache_k, cache_v, page_rows)
    kv = (p[:NP, hq:hq + hk].reshape(BATCH, SEQ, KVH_B, HD_B), p[:NP, hq + hk:].reshape(BATCH, SEQ, KVH_B, HD_B),
          k_new.reshape(DEC_BATCH, DEC_SEQ, KVH_B, HD_B), v_new.reshape(DEC_BATCH, DEC_SEQ, KVH_B, HD_B))
    return (op, os_), w_out.astype(BF16), kv


def _swa_layer(x, w_in, sinks, w_out, cache_k, cache_v):
    p = matmul(x, w_in.astype(BF16))
    hq = NH_C * HD_C
    kw = KVH_C * HD_C
    sink_row = jnp.pad(sinks, (0, 128 - NH_C)).reshape(1, 128)
    op = swa_prompt(p, sink_row)
    ps = p[NP:]
    q = _pad_rows(ps[:, :hq].reshape(DEC_BATCH, DEC_SEQ, hq), 8)
    k_new = ps[:, hq:hq + kw].reshape(DEC_BATCH, DEC_SEQ, kw)
    v_new = ps[:, hq + kw:].reshape(DEC_BATCH, DEC_SEQ, kw)
    ck = cache_k.reshape(DEC_BATCH, WINDOW, kw)
    cv = cache_v.reshape(DEC_BATCH, WINDOW, kw)
    os_ = swa_sample(q, _pad_rows(k_new, 8), _pad_rows(v_new, 8), ck, cv, sink_row)[:, :DEC_SEQ].reshape(NS, hq)
    pp = p[:NP].reshape(BATCH, SEQ, C_IN)[:, SEQ - WINDOW:]
    kv = (pp[..., hq:hq + kw].reshape(BATCH, WINDOW, KVH_C, HD_C), pp[..., hq + kw:].reshape(BATCH, WINDOW, KVH_C, HD_C),
          jnp.concatenate([ck, k_new], axis=1)[:, DEC_SEQ:].reshape(DEC_BATCH, WINDOW, KVH_C, HD_C),
          jnp.concatenate([cv, v_new], axis=1)[:, DEC_SEQ:].reshape(DEC_BATCH, WINDOW, KVH_C, HD_C))
    return (op, os_), w_out.astype(BF16), kv


def _moe_layer(x, w_group, b_group, w_router, b_router, w_gate, w_up, w_down, g, b, layer):
    pad = ROUTER_ROWS - N_GROUPS - N_EXPERTS
    wt = jnp.concatenate([w_group.T, w_router.T, jnp.zeros((pad, D_MODEL), F32)], axis=0)
    bt = jnp.concatenate([b_group, b_router, jnp.zeros((pad,), F32)]).reshape(ROUTER_ROWS, 1)
    meta, wts, seg = moe_router(x, wt, bt)
    return moe_ffn_ln(x, meta, wts, seg, w_gate, w_up, w_down, g, b, layer)


def kernel(x_prompt, x_sample, state_mlstm_C, state_mlstm_n, state_mlstm_m, cache_moba_k, cache_moba_v, cache_swa_k, cache_swa_v, page_table, mlstm_w_in, mlstm_b_gate, mlstm_norm_g, mlstm_w_out, moba_w_in, moba_w_out, swa_w_in, swa_sinks, swa_w_out, ln_mix_g, ln_mix_b, ln_ffn_g, ln_ffn_b, moe_w_group, moe_b_group, moe_w_router, moe_b_router, moe_w_gate, moe_w_up, moe_w_down):
    x = jnp.concatenate([x_prompt.reshape(NP, D_MODEL), x_sample.reshape(NS, D_MODEL)], axis=0)
    n_pool = cache_moba_k.shape[1]
    moba_k = cache_moba_k.reshape(-1, PAGE_SIZE * KVH_B, HD_B)
    moba_v = cache_moba_v.reshape(-1, PAGE_SIZE * KVH_B, HD_B)
    wg_all, wu_all, wd_all = moe_w_gate.astype(BF16), moe_w_up.astype(BF16), moe_w_down.astype(BF16)
    m_states, b_kv, c_kv = [], [], []
    for layer in range(DEPTH):
        kind, slot = layer % 3, layer // 3
        if kind == 0:
            a, w_out, st = _mlstm_layer(x, mlstm_w_in[slot], mlstm_b_gate[slot], mlstm_norm_g[slot], mlstm_w_out[slot],
                                        state_mlstm_C[slot], state_mlstm_n[slot], state_mlstm_m[slot])
            m_states.append(st)
        elif kind == 1:
            a, w_out, kv = _moba_layer(x, moba_w_in[slot], moba_w_out[slot], moba_k, moba_v,
                                       page_table + slot * n_pool)
            b_kv.append(kv)
        else:
            a, w_out, kv = _swa_layer(x, swa_w_in[slot], swa_sinks[slot], swa_w_out[slot], cache_swa_k[slot],
                                      cache_swa_v[slot])
            c_kv.append(kv)
        x = outproj_ln(a[0], a[1], w_out, x, ln_mix_g[layer], ln_mix_b[layer])
        x = _moe_layer(x, moe_w_group[layer], moe_b_group[layer], moe_w_router[layer], moe_b_router[layer],
                       wg_all, wu_all, wd_all, ln_ffn_g[layer], ln_ffn_b[layer], layer)
    stack = lambda items, j: jnp.stack([it[j] for it in items])
    return (x[:NP].reshape(BATCH, SEQ, D_MODEL), x[NP:].reshape(DEC_BATCH, DEC_SEQ, D_MODEL),
            stack(m_states, 0), stack(m_states, 1), stack(m_states, 2),
            stack(m_states, 3), stack(m_states, 4), stack(m_states, 5),
            stack(b_kv, 0), stack(b_kv, 1), stack(b_kv, 2), stack(b_kv, 3),
            stack(c_kv, 0), stack(c_kv, 1), stack(c_kv, 2), stack(c_kv, 3))
```

```python
import functools

import jax
import jax.numpy as jnp
import numpy as np
from jax import lax
from jax.experimental import pallas as pl
from jax.experimental.pallas import tpu as pltpu

F32 = jnp.float32
BF16 = jnp.bfloat16
I32 = jnp.int32
HIGHEST = lax.Precision.HIGHEST

D_MODEL = 1024
BATCH = 2
SEQ = 8192
DEPTH = 4
DEC_BATCH = 128
DEC_SEQ = 4
PAGE_SIZE = 128
N_PAGES = 64
PAST = N_PAGES * PAGE_SIZE
NP = BATCH * SEQ
NS = DEC_BATCH * DEC_SEQ
NT = NP + NS

NH_M = 4
DK_M = 128
DV_M = 256
M_CHUNK = 128
M_QK = NH_M * DK_M
M_V = NH_M * DV_M
M_IN = 2 * M_QK + 2 * M_V + 2 * NH_M
M_IN_PAD = 3200
M_GATE_COL = 2 * M_QK + 2 * M_V
MLSTM_SEQ_PER_STEP = 4

NH_B = 8
KVH_B = 4
G_B = 2
HD_B = 128
MOBA_BLOCK = 256
MOBA_TOPK = 3
NB_B = SEQ // MOBA_BLOCK
NB_S = PAST // MOBA_BLOCK
B_IN = (NH_B + 2 * KVH_B) * HD_B
SAMPLE_PAGES_PER_STEP = 16
MOBA_QTILE = 2 * MOBA_BLOCK
AUG = 2 * HD_B
POS_LANE = NB_B

NH_C = 16
KVH_C = 2
G_C = 8
HD_C = 64
WINDOW = 128
C_IN = (NH_C + 2 * KVH_C) * HD_C
SWA_SEQ_PER_STEP = 8

N_GROUPS = 4
N_EXP = 8
N_EXPERTS = 32
D_FF = 256
ROUTER_ROWS = 128
MOE_TM = 1536
MOE_RC = 128
MOE_WSLOTS = 4
MOE_ROWS = 2 * MOE_TM + N_EXPERTS * 8 + MOE_RC

ALPHA = (2.0 * DEPTH) ** 0.25
EPS = 1e-5
LOG2E = 1.4426950408889634
NEG_INF = float("-inf")
MASKED = -1e30
VMEM_LIMIT = 56 * 1024 * 1024


def _params(*sem):
    return pltpu.CompilerParams(dimension_semantics=sem, vmem_limit_bytes=VMEM_LIMIT)


def _dot(a, b):
    return jnp.dot(a, b, preferred_element_type=F32)


def _dot_nt(a, b, precision=None):
    return lax.dot_general(a, b, (((1,), (1,)), ((), ())), precision=precision, preferred_element_type=F32)


def _dot_tn(a, b):
    return lax.dot_general(a, b, (((0,), (0,)), ((), ())), preferred_element_type=F32)


def _layer_norm(z, g, b):
    mu = jnp.mean(z, axis=-1, keepdims=True)
    zc = z - mu
    var = jnp.mean(zc * zc, axis=-1, keepdims=True)
    return zc * lax.rsqrt(var + EPS) * g + b


def _bf16_parts(x, n=3):
    parts = []
    for _ in range(n):
        bits = int(np.float32(x).view(np.uint32))
        rounded = ((bits + 0x7FFF + ((bits >> 16) & 1)) >> 16) << 16
        p = float(np.uint32(rounded & 0xFFFFFFFF).view(np.float32))
        parts.append(p)
        x = x - p
    return parts


def _mm_kernel(x_ref, w_ref, o_ref):
    o_ref[...] = _dot(x_ref[...].astype(BF16), w_ref[...])


def matmul(x, w_bf16, tm=512):
    m, k = x.shape
    n = w_bf16.shape[1]
    return pl.pallas_call(
        _mm_kernel,
        grid=(m // tm,),
        in_specs=[pl.BlockSpec((tm, k), lambda i: (i, 0)), pl.BlockSpec((k, n), lambda i: (0, 0))],
        out_specs=pl.BlockSpec((tm, n), lambda i: (i, 0)),
        out_shape=jax.ShapeDtypeStruct((m, n), F32),
        compiler_params=_params("arbitrary"),
        name="in_proj",
    )(x, w_bf16)


def _outproj_ln_kernel(ap_ref, as_ref, w_ref, r_ref, g_ref, b_ref, o_ref, *, n_p):
    a = jnp.where(pl.program_id(0) < n_p, ap_ref[...], as_ref[...])
    y = _dot(a.astype(BF16), w_ref[...])
    o_ref[...] = _layer_norm(ALPHA * r_ref[...] + y, g_ref[...], b_ref[...])


def outproj_ln(a_p, a_s, w_bf16, resid, g, b, tm=512):
    k = a_p.shape[1]
    n = w_bf16.shape[1]
    n_p = a_p.shape[0] // tm
    m = a_p.shape[0] + a_s.shape[0]
    row = lambda i: (i, 0)
    fixed = lambda i: (0, 0)
    return pl.pallas_call(
        functools.partial(_outproj_ln_kernel, n_p=n_p),
        grid=(m // tm,),
        in_specs=[pl.BlockSpec((tm, k), lambda i: (jnp.minimum(i, n_p - 1), 0)),
                  pl.BlockSpec((tm, k), lambda i: (jnp.maximum(i - n_p, 0), 0)),
                  pl.BlockSpec((k, n), fixed), pl.BlockSpec((tm, n), row),
                  pl.BlockSpec((1, n), fixed), pl.BlockSpec((1, n), fixed)],
        out_specs=pl.BlockSpec((tm, n), row),
        out_shape=jax.ShapeDtypeStruct((m, n), F32),
        compiler_params=_params("arbitrary"),
        name="out_proj_ln",
    )(a_p, a_s, w_bf16, resid, g.reshape(1, n), b.reshape(1, n))


def _log_sigmoid(x):
    return jnp.minimum(x, 0.0) - jnp.log1p(jnp.exp(-jnp.abs(x)))


def _mlstm_chunk(p_ref, bg_ref, ng_ref, hg_ref, c_s, n_s, m_s, *, L, L_real):
    gates = p_ref[:, M_GATE_COL:M_IN_PAD] + bg_ref[...]
    lane = lax.broadcasted_iota(jnp.int32, (L, 128), 1)
    x = jnp.where(lane < NH_M, gates, _log_sigmoid(gates))
    if L_real < L:
        tok = lax.broadcasted_iota(jnp.int32, (L, 128), 0)
        x = jnp.where(tok < L_real, x, jnp.where(lane < NH_M, -1e30, 0.0))
    r = lax.broadcasted_iota(jnp.int32, (L, L), 0)
    s = lax.broadcasted_iota(jnp.int32, (L, L), 1)
    causal = r >= s
    tri = causal.astype(F32)
    bcum = jnp.dot(tri, x, precision=HIGHEST, preferred_element_type=F32)
    eye8 = (lax.broadcasted_iota(jnp.int32, (8, 128), 0) == lax.broadcasted_iota(jnp.int32, (8, 128), 1)).astype(F32)
    x_rows = _dot_nt(eye8, x, HIGHEST)
    b_rows = _dot_nt(eye8, bcum, HIGHEST)

    for h in range(NH_M):
        li_row = x_rows[h:h + 1, :]
        li_col = x[:, h:h + 1]
        b_row = b_rows[NH_M + h:NH_M + h + 1, :]
        b_col = bcum[:, NH_M + h:NH_M + h + 1]
        m_old = m_s[:, h:h + 1]
        dm = jnp.where(causal, b_col - b_row + li_row, NEG_INF)
        g_col = b_col + m_old
        mq = jnp.maximum(g_col, jnp.max(dm, axis=1, keepdims=True))
        w_intra = jnp.exp(dm - mq)
        w_inter = jnp.exp(g_col - mq)
        q = p_ref[:, h * DK_M:(h + 1) * DK_M] * (DK_M ** -0.5)
        k = p_ref[:, M_QK + h * DK_M:M_QK + (h + 1) * DK_M]
        v = p_ref[:, 2 * M_QK + h * DV_M:2 * M_QK + (h + 1) * DV_M]
        qb = q.astype(BF16)
        kb = k.astype(BF16)
        cmat = c_s[h]
        nrow = n_s[h:h + 1, :]
        sc = _dot_nt(qb, kb) * w_intra
        num = _dot(sc.astype(BF16), v.astype(BF16)) + w_inter * _dot(qb, cmat.astype(BF16))
        den = jnp.sum(sc, axis=1, keepdims=True) + w_inter * jnp.sum(q * nrow, axis=1, keepdims=True)
        hh = num / jnp.maximum(jnp.abs(den), jnp.exp(-mq))
        bl = b_col[L - 1:L, :]
        lw = bl - b_col + li_col
        m_new = jnp.maximum(bl + m_old, jnp.max(lw, axis=0, keepdims=True))
        wl = jnp.exp(lw - m_new)
        dec = jnp.exp(bl + m_old - m_new)
        c_s[h] = dec * cmat + _dot_tn(kb, (wl * v).astype(BF16))
        n_s[h:h + 1, :] = dec * nrow + jnp.sum(wl * k, axis=0, keepdims=True)
        m_s[:, h:h + 1] = m_new
        mu = jnp.mean(hh, axis=1, keepdims=True)
        hc = hh - mu
        var = jnp.mean(hc * hc, axis=1, keepdims=True)
        hn = hc * lax.rsqrt(var + EPS) * ng_ref[:, h * DV_M:(h + 1) * DV_M]
        o = p_ref[:, 2 * M_QK + M_V + h * DV_M:2 * M_QK + M_V + (h + 1) * DV_M]
        hg_ref[:, h * DV_M:(h + 1) * DV_M] = hn / (1.0 + jnp.exp(-o))


def _mlstm_prompt_kernel(*refs, L, nseq):
    p_refs = refs[:nseq]
    bg_ref, ng_ref, hg_ref, co_ref, no_ref, mo_ref, c_s, n_s, m_s = refs[nseq:]
    c = pl.program_id(0)

    @pl.when(c == 0)
    def _():
        c_s[...] = jnp.zeros_like(c_s)
        n_s[...] = jnp.zeros_like(n_s)
        m_s[...] = jnp.zeros_like(m_s)

    for s in range(nseq):
        _mlstm_chunk(p_refs[s], bg_ref, ng_ref, hg_ref.at[s], c_s.at[s], n_s.at[s], m_s.at[s], L=L, L_real=L)

    @pl.when(c == pl.num_programs(0) - 1)
    def _():
        co_ref[...] = c_s[...]
        no_ref[...] = n_s[...]
        mo_ref[...] = m_s[...]


def mlstm_prompt(p, b_gate_row, norm_g_row):
    L = M_CHUNK
    nc = SEQ // L
    fixed = lambda c: (0, 0)
    whole3 = lambda c: (0, 0, 0)
    p_specs = [pl.BlockSpec((L, M_IN_PAD), functools.partial(lambda c, s: (s * nc + c, 0), s=s)) for s in range(BATCH)]
    hg, cst, nst, mst = pl.pallas_call(
        functools.partial(_mlstm_prompt_kernel, L=L, nseq=BATCH),
        grid=(nc,),
        in_specs=p_specs + [pl.BlockSpec((1, 128), fixed), pl.BlockSpec((1, M_V), fixed)],
        out_specs=[pl.BlockSpec((BATCH, L, M_V), lambda c: (0, c, 0)),
                   pl.BlockSpec((BATCH, NH_M, DK_M, DV_M), lambda c: (0, 0, 0, 0)),
                   pl.BlockSpec((BATCH, NH_M, DK_M), whole3),
                   pl.BlockSpec((BATCH, 1, NH_M), whole3)],
        out_shape=[jax.ShapeDtypeStruct((BATCH, SEQ, M_V), F32),
                   jax.ShapeDtypeStruct((BATCH, NH_M, DK_M, DV_M), F32),
                   jax.ShapeDtypeStruct((BATCH, NH_M, DK_M), F32),
                   jax.ShapeDtypeStruct((BATCH, 1, NH_M), F32)],
        scratch_shapes=[pltpu.VMEM((BATCH, NH_M, DK_M, DV_M), F32), pltpu.VMEM((BATCH, NH_M, DK_M), F32),
                        pltpu.VMEM((BATCH, 1, NH_M), F32)],
        compiler_params=_params("arbitrary"),
        name="mlstm_prompt",
    )(*([p] * BATCH), b_gate_row, norm_g_row)
    return hg.reshape(NP, M_V), cst, nst, mst.reshape(BATCH, NH_M)


def _mlstm_sample_kernel(p_ref, c0_ref, n0_ref, m0_ref, bg_ref, ng_ref, hg_ref, co_ref, no_ref, mo_ref,
                         *, L, L_real, nseq):
    co_ref[...] = c0_ref[...]
    no_ref[...] = n0_ref[...]
    mo_ref[...] = m0_ref[...]
    for s in range(nseq):
        _mlstm_chunk(p_ref.at[pl.ds(s * L, L)], bg_ref, ng_ref, hg_ref.at[pl.ds(s * L, L)],
                     co_ref.at[s], no_ref.at[s], mo_ref.at[s], L=L, L_real=L_real)


def mlstm_sample(p, c0_all, n0_all, m0_all, slot, b_gate_row, norm_g_row, nseq=MLSTM_SEQ_PER_STEP):
    L = 8
    nb = DEC_BATCH
    m0_all = m0_all.reshape(m0_all.shape[0], nb, 1, NH_M)
    fixed = lambda b: (0, 0)
    per_b = lambda b: (b, 0, 0)
    return pl.pallas_call(
        functools.partial(_mlstm_sample_kernel, L=L, L_real=DEC_SEQ, nseq=nseq),
        grid=(nb // nseq,),
        in_specs=[pl.BlockSpec((nseq * L, M_IN_PAD), lambda b: (b, 0)),
                  pl.BlockSpec((None, nseq, NH_M, DK_M, DV_M), lambda b: (slot, b, 0, 0, 0)),
                  pl.BlockSpec((None, nseq, NH_M, DK_M), lambda b: (slot, b, 0, 0)),
                  pl.BlockSpec((None, nseq, 1, NH_M), lambda b: (slot, b, 0, 0)),
                  pl.BlockSpec((1, 128), fixed),
                  pl.BlockSpec((1, M_V), fixed)],
        out_specs=[pl.BlockSpec((nseq * L, M_V), lambda b: (b, 0)),
                   pl.BlockSpec((nseq, NH_M, DK_M, DV_M), lambda b: (b, 0, 0, 0)),
                   pl.BlockSpec((nseq, NH_M, DK_M), per_b),
                   pl.BlockSpec((nseq, 1, NH_M), per_b)],
        out_shape=[jax.ShapeDtypeStruct((nb * L, M_V), F32),
                   jax.ShapeDtypeStruct((nb, NH_M, DK_M, DV_M), F32),
                   jax.ShapeDtypeStruct((nb, NH_M, DK_M), F32),
                   jax.ShapeDtypeStruct((nb, 1, NH_M), F32)],
        compiler_params=_params("arbitrary"),
        name="mlstm_sample",
    )(p, c0_all, n0_all, m0_all, b_gate_row, norm_g_row)


def _topk_mask_lanes(g, k):
    lane = lax.broadcasted_iota(jnp.int32, g.shape, 1)
    n = g.shape[1]
    sel = jnp.zeros(g.shape, F32)
    for _ in range(k):
        mx = jnp.max(g, axis=1, keepdims=True)
        idx = jnp.min(jnp.where(g == mx, lane, n), axis=1, keepdims=True)
        hit = (lane == idx) & (mx > NEG_INF)
        sel = jnp.where(hit, 1.0, sel)
        g = jnp.where(lane == idx, NEG_INF, g)
    return sel


def _moba_select_kernel(q_ref, k_ref, v_ref, ac_ref, qa_ref, ka_ref, vt_ref, kmean_s, *, tq):
    i = pl.program_id(2)

    @pl.when(i == 0)
    def _():
        kmean_s[...] = jnp.zeros_like(kmean_s)
        kmean_s[0:NB_B, :] = jnp.sum(k_ref[...].reshape(NB_B, MOBA_BLOCK, HD_B), axis=1) * (1.0 / MOBA_BLOCK)
        rows = 1024
        lane = lax.broadcasted_iota(jnp.int32, (rows, HD_B), 1)
        rloc = lax.broadcasted_iota(jnp.int32, (rows, HD_B), 0)
        for ch in range(SEQ // rows):
            vt_ref[:, ch * rows:(ch + 1) * rows] = v_ref[ch * rows:(ch + 1) * rows, :].T.astype(BF16)
            row = rloc + ch * rows
            onehot = (lane == row // MOBA_BLOCK).astype(F32)
            offs = (row % MOBA_BLOCK).astype(F32)
            blks = (row // MOBA_BLOCK).astype(F32)
            pat = jnp.where(lane < NB_B, onehot,
                            jnp.where(lane < POS_LANE + 3, offs, jnp.where(lane < POS_LANE + 6, blks, 0.0)))
            ka_ref[ch * rows:(ch + 1) * rows, 0:HD_B] = k_ref[ch * rows:(ch + 1) * rows, :].astype(BF16)
            ka_ref[ch * rows:(ch + 1) * rows, HD_B:AUG] = pat.astype(BF16)

    km = kmean_s[...]
    lane = lax.broadcasted_iota(jnp.int32, (tq, HD_B), 1)
    blk = lax.broadcasted_iota(jnp.int32, (NB_B, tq), 0)
    own = (i * tq + lax.broadcasted_iota(jnp.int32, (NB_B, tq), 1)) // MOBA_BLOCK
    for g in range(G_B):
        q = q_ref[:, g * HD_B:(g + 1) * HD_B]
        gate = _dot_nt(km, q * (HD_B ** -0.5), HIGHEST)[0:NB_B, :]
        gate = jnp.where(blk < own, gate, NEG_INF)
        sel = jnp.zeros((NB_B, tq), F32)
        for _ in range(MOBA_TOPK):
            mx = jnp.max(gate, axis=0, keepdims=True)
            idx = jnp.min(jnp.where(gate == mx, blk, NB_B), axis=0, keepdims=True)
            sel = jnp.where((blk == idx) & (mx > NEG_INF), 1.0, sel)
            gate = jnp.where(blk == idx, NEG_INF, gate)
        sel = jnp.where(blk == own, 1.0, sel)
        maskt = jnp.concatenate([jnp.where(sel > 0.5, 0.0, MASKED), jnp.zeros((128 - NB_B, tq), F32)], axis=0)
        aug = jnp.where(lane < NB_B, maskt.T, ac_ref[g, 0:1, :])
        qa_ref[:, g * AUG:g * AUG + HD_B] = (q * (HD_B ** -0.5 * LOG2E)).astype(BF16)
        qa_ref[:, g * AUG + HD_B:(g + 1) * AUG] = aug.astype(BF16)


def moba_select(p, aug_const, tq=1024):
    kern = functools.partial(_moba_select_kernel, tq=tq)
    nq = SEQ // tq
    per_bc = lambda b, c, i: (b, c, 0, 0)
    return pl.pallas_call(
        kern,
        grid=(BATCH, KVH_B, nq),
        in_specs=[pl.BlockSpec((tq, G_B * HD_B), lambda b, c, i: (b * nq + i, c)),
                  pl.BlockSpec((SEQ, HD_B), lambda b, c, i: (b, NH_B + c)),
                  pl.BlockSpec((SEQ, HD_B), lambda b, c, i: (b, NH_B + KVH_B + c)),
                  pl.BlockSpec((G_B, 8, 128), lambda b, c, i: (c, 0, 0))],
        out_specs=[pl.BlockSpec((tq, G_B * AUG), lambda b, c, i: (b * nq + i, c)),
                   pl.BlockSpec((None, None, SEQ, AUG), per_bc),
                   pl.BlockSpec((None, None, HD_B, SEQ), per_bc)],
        out_shape=[jax.ShapeDtypeStruct((NP, NH_B * AUG), BF16),
                   jax.ShapeDtypeStruct((BATCH, KVH_B, SEQ, AUG), BF16),
                   jax.ShapeDtypeStruct((BATCH, KVH_B, HD_B, SEQ), BF16)],
        scratch_shapes=[pltpu.VMEM((128, HD_B), F32)],
        compiler_params=_params("arbitrary", "arbitrary", "arbitrary"),
        name="moba_select",
    )(p, p, p, aug_const)


def _moba_flash_kernel(qa_ref, ka_ref, vt_ref, o_ref, s_buf):
    i = pl.program_id(2)
    bq = MOBA_QTILE
    tk = MOBA_QTILE
    nl = G_B * bq
    qa = jnp.concatenate([qa_ref[:, g * AUG:(g + 1) * AUG] for g in range(G_B)], axis=0)

    def scores(t, slot):
        off = pl.multiple_of(t * tk, tk)
        s_buf[slot] = _dot_nt(ka_ref[pl.ds(off, tk), :], qa)

    def softmax_pv(t, slot, last, carry):
        m, l, acc = carry
        st = s_buf[slot]
        if last:
            kpos = t * tk + lax.broadcasted_iota(jnp.int32, (tk, nl), 0)
            qpos = i * bq + lax.broadcasted_iota(jnp.int32, (tk, nl), 1) % bq
            st = jnp.where(kpos <= qpos, st, NEG_INF)
        m_new = jnp.maximum(m, jnp.max(st, axis=0, keepdims=True))
        pt = jnp.exp2(st - m_new)
        a = jnp.exp2(m - m_new)
        l = a * l + jnp.sum(pt, axis=0, keepdims=True)
        off = pl.multiple_of(t * tk, tk)
        acc = a * acc + _dot(vt_ref[:, pl.ds(off, tk)], pt.astype(BF16))
        return m_new, l, acc

    def pair(u, carry):
        scores(2 * u + 1, 1)
        carry = softmax_pv(2 * u, 0, False, carry)
        scores(2 * u + 2, 0)
        return softmax_pv(2 * u + 1, 1, False, carry)

    def tail_two(carry):
        t = 2 * npairs
        scores(t + 1, 1)
        return softmax_pv(t + 1, 1, True, softmax_pv(t, 0, False, carry))

    def tail_one(carry):
        return softmax_pv(2 * npairs, 0, True, carry)

    n_full = i
    npairs = n_full // 2
    init = (jnp.full((1, nl), MASKED, F32), jnp.zeros((1, nl), F32), jnp.zeros((HD_B, nl), F32))
    scores(0, 0)
    carry = lax.fori_loop(0, npairs, pair, init)
    m, l, acc = lax.cond(n_full % 2 == 1, tail_two, tail_one, carry)
    out = acc / l
    for g in range(G_B):
        o_ref[:, g * HD_B:(g + 1) * HD_B] = out[:, g * bq:(g + 1) * bq].T


def moba_flash(q_aug, k_aug, v_t):
    nq = SEQ // MOBA_QTILE
    per_bc = lambda b, c, i: (b, c, 0, 0)
    return pl.pallas_call(
        _moba_flash_kernel,
        grid=(BATCH, KVH_B, nq),
        in_specs=[pl.BlockSpec((MOBA_QTILE, G_B * AUG), lambda b, c, i: (b * nq + i, c)),
                  pl.BlockSpec((None, None, SEQ, AUG), per_bc),
                  pl.BlockSpec((None, None, HD_B, SEQ), per_bc)],
        out_specs=pl.BlockSpec((MOBA_QTILE, G_B * HD_B), lambda b, c, i: (b * nq + i, c)),
        out_shape=jax.ShapeDtypeStruct((NP, NH_B * HD_B), F32),
        scratch_shapes=[pltpu.VMEM((2, MOBA_QTILE, G_B * MOBA_QTILE), F32)],
        compiler_params=_params("arbitrary", "arbitrary", "arbitrary"),
        name="moba_flash",
    )(q_aug, k_aug, v_t)


def _moba_aug_const():
    c = np.zeros((NH_B, 8, 128), np.float32)
    for h in range(NH_B):
        beta = 2.0 ** (-8.0 * (h + 1) / NH_B) * LOG2E
        c[h, 0, POS_LANE:POS_LANE + 3] = _bf16_parts(beta)
        c[h, 0, POS_LANE + 3:POS_LANE + 6] = _bf16_parts(beta * MOBA_BLOCK)
    return jnp.asarray(c)


def _moba_sample_kernel(pt_ref, qa_ref, qat_ref, kn_ref, vn_ref, bias_ref, ck_ref, cv_ref,
                        o_ref, buf, sem, s_s, p_s, ksum_s, acc_s, inv_s):
    npg = SAMPLE_PAGES_PER_STEP
    w = KVH_B * HD_B
    nrow = KVH_B * DEC_SEQ * G_B

    def load_page(sl, pg):
        return jnp.concatenate([buf[sl, pg, pl.ds(c, PAGE_SIZE, stride=KVH_B), :] for c in range(KVH_B)], axis=1)

    b = pl.program_id(0)
    t = pl.program_id(1)
    nb = pl.num_programs(0)
    g = b * 8 + t
    slot = g % 2

    def page_copy(src_ref, page, sl, pg):
        return pltpu.make_async_copy(src_ref.at[page], buf.at[sl, pg], sem.at[sl])

    def start_fetch(bn, tn, sl):
        qn = tn % 4

        @pl.when(tn < 4)
        def _():
            for pg in range(npg):
                page_copy(ck_ref, pt_ref[bn, qn * npg + pg], sl, pg).start()

        @pl.when(tn >= 4)
        def _():
            for pg in range(npg):
                page_copy(cv_ref, pt_ref[bn, qn * npg + pg], sl, pg).start()

    @pl.when(g == 0)
    def _():
        start_fetch(b, t, slot)

    @pl.when(g + 1 < nb * 8)
    def _():
        tn = (t + 1) % 8
        bn = b + (t + 1) // 8
        start_fetch(bn, tn, 1 - slot)

    for pg in range(npg):
        page_copy(ck_ref, 0, slot, pg).wait()

    qab = (qa_ref[...] * LOG2E).astype(BF16)

    @pl.when(g == 0)
    def _():
        p_s[...] = jnp.zeros_like(p_s)

    @pl.when(t < 4)
    def _():
        for pg in range(npg):
            page = load_page(slot, pg)
            col0 = pl.multiple_of((t * npg + pg) * PAGE_SIZE, PAGE_SIZE)
            s_s[:, pl.ds(col0, PAGE_SIZE)] = _dot(page.astype(BF16), qab).T
            csum = jnp.sum(page, axis=0, keepdims=True)
            if pg % 2 == 0:
                prev = csum
            else:
                ksum_s[pl.ds(t * (npg // 2) + pg // 2, 1), :] = prev + csum

    @pl.when(t == 3)
    def _():
        qat = qat_ref[...]
        kmean = ksum_s[...] * (1.0 / MOBA_BLOCK)
        sel = _topk_mask_lanes(_dot_nt(qat, kmean, HIGHEST), MOBA_TOPK)
        kn = jnp.concatenate([kn_ref[...], jnp.zeros((PAGE_SIZE - 8, w), F32)], axis=0)
        s_s[0:nrow, PAST:PAST + PAGE_SIZE] = _dot_nt((qat * LOG2E).astype(BF16), kn.astype(BF16))
        selm = jnp.where(sel > 0.5, 0.0, NEG_INF)

        def logits(bk):
            if bk == NB_S:
                return s_s[0:nrow, PAST:PAST + PAGE_SIZE] + bias_ref[:, PAST:PAST + PAGE_SIZE]
            lo = bk * MOBA_BLOCK
            return s_s[0:nrow, lo:lo + MOBA_BLOCK] + bias_ref[:, lo:lo + MOBA_BLOCK] + selm[:, bk:bk + 1]

        macc = logits(0)
        for bk in range(1, NB_S):
            macc = jnp.maximum(macc, logits(bk))
        mx = jnp.maximum(jnp.max(macc, axis=1, keepdims=True), jnp.max(logits(NB_S), axis=1, keepdims=True))
        dacc = jnp.zeros((nrow, MOBA_BLOCK), F32)
        for bk in range(NB_S):
            pr = jnp.exp2(logits(bk) - mx)
            p_s[0:nrow, bk * MOBA_BLOCK:(bk + 1) * MOBA_BLOCK] = pr
            dacc = dacc + pr
        pr = jnp.exp2(logits(NB_S) - mx)
        p_s[0:nrow, PAST:PAST + PAGE_SIZE] = pr
        den = jnp.sum(dacc, axis=1, keepdims=True) + jnp.sum(pr, axis=1, keepdims=True)
        inv_s[...] = jnp.zeros_like(inv_s)
        inv_s[0:nrow, :] = jnp.broadcast_to(1.0 / den, (nrow, 128))
        inv_s[...] = inv_s[...].T

    @pl.when(t == 4)
    def _():
        vn = jnp.concatenate([vn_ref[...], jnp.zeros((PAGE_SIZE - 8, w), F32)], axis=0)
        acc_s[...] = _dot_tn(vn.astype(BF16), p_s[:, PAST:PAST + PAGE_SIZE].T.astype(BF16))

    @pl.when(t >= 4)
    def _():
        acc = acc_s[...]
        for pg in range(npg):
            col0 = pl.multiple_of(((t - 4) * npg + pg) * PAGE_SIZE, PAGE_SIZE)
            pt = p_s[:, pl.ds(col0, PAGE_SIZE)].T
            acc = acc + _dot_tn(load_page(slot, pg).astype(BF16), pt.astype(BF16))
        acc_s[...] = acc

    @pl.when(t == 7)
    def _():
        o_ref[...] = acc_s[:, 0:nrow] * inv_s[0:1, 0:nrow]


def moba_sample(page_table, q_aug, q_aug_t, k_new, v_new, bias, cache_k, cache_v):
    w = KVH_B * HD_B
    nrow = KVH_B * DEC_SEQ * G_B
    nkeys = PAST + PAGE_SIZE
    per_b3 = lambda b, t, pt: (b, 0, 0)
    fixed = lambda b, t, pt: (0, 0)
    grid_spec = pltpu.PrefetchScalarGridSpec(
        num_scalar_prefetch=1,
        grid=(DEC_BATCH, 8),
        in_specs=[pl.BlockSpec((None, w, 128), per_b3),
                  pl.BlockSpec((None, nrow, w), per_b3),
                  pl.BlockSpec((None, 8, w), per_b3),
                  pl.BlockSpec((None, 8, w), per_b3),
                  pl.BlockSpec((nrow, nkeys), fixed),
                  pl.BlockSpec(memory_space=pl.ANY),
                  pl.BlockSpec(memory_space=pl.ANY)],
        out_specs=pl.BlockSpec((None, w, nrow), per_b3),
        scratch_shapes=[pltpu.VMEM((2, SAMPLE_PAGES_PER_STEP, PAGE_SIZE * KVH_B, HD_B), F32),
                        pltpu.SemaphoreType.DMA((2,)),
                        pltpu.VMEM((128, nkeys), F32),
                        pltpu.VMEM((128, nkeys), F32),
                        pltpu.VMEM((NB_S, w), F32),
                        pltpu.VMEM((w, 128), F32),
                        pltpu.VMEM((128, 128), F32)],
    )
    return pl.pallas_call(
        _moba_sample_kernel,
        grid_spec=grid_spec,
        out_shape=jax.ShapeDtypeStruct((DEC_BATCH, w, nrow), F32),
        compiler_params=_params("arbitrary", "arbitrary"),
        name="moba_sample",
    )(page_table, q_aug, q_aug_t, k_new, v_new, bias, cache_k, cache_v)


def _swa_slope(h):
    return 2.0 ** (-8.0 * (h + 1) / NH_C)


def _by_head(gidx, values):
    out = values[-1]
    for g in range(len(values) - 2, -1, -1):
        out = jnp.where(gidx == g, values[g], out)
    return out


def _swa_kv_aug(k_all, v_all, c):
    nk = k_all.shape[0]
    lane = lax.broadcasted_iota(jnp.int32, (nk, HD_C), 1)
    kidx = lax.broadcasted_iota(jnp.int32, (nk, HD_C), 0).astype(F32)
    kpat = jnp.where(lane < 3, kidx, 0.0)
    ka = jnp.concatenate([k_all[:, c * HD_C:(c + 1) * HD_C], kpat], axis=1).astype(BF16)
    return ka, v_all[:, c * HD_C:(c + 1) * HD_C].astype(BF16)


def _swa_softmax_pv(qa, ka, vc, maskt, sink2):
    s2 = _dot_nt(qa, ka) + maskt
    mx = jnp.maximum(jnp.max(s2, axis=1, keepdims=True), sink2)
    pr = jnp.exp2(s2 - mx)
    den = jnp.sum(pr, axis=1, keepdims=True) + jnp.exp2(sink2 - mx)
    return _dot(pr.astype(BF16), vc) / den


def _store_heads(o_ref, outs, lead=()):
    for j in range(0, NH_C, 2):
        o_ref[lead + (slice(None), slice(j * HD_C, (j + 2) * HD_C))] = jnp.concatenate(outs[j:j + 2], axis=1)


def _swa_prompt_kernel(q_ref, kvp_ref, kvc_ref, sink_ref, o_ref):
    i = pl.program_id(1)
    w = WINDOW
    kw = KVH_C * HD_C
    k_all = jnp.concatenate([kvp_ref[:, 0:kw], kvc_ref[:, 0:kw]], axis=0)
    v_all = jnp.concatenate([kvp_ref[:, kw:2 * kw], kvc_ref[:, kw:2 * kw]], axis=0)
    kk = lax.broadcasted_iota(jnp.int32, (2 * w, w), 0)
    qq = lax.broadcasted_iota(jnp.int32, (2 * w, w), 1)
    di = qq - kk + w
    ok = (di >= 0) & (di <= w) & ((kk >= w) | (i > 0))
    mask1 = jnp.where(ok, 0.0, NEG_INF)
    maskt = jnp.concatenate([mask1] * G_C, axis=1)
    qdist = (lax.broadcasted_iota(jnp.int32, (1, w), 1) + w).astype(F32)
    qlane = lax.broadcasted_iota(jnp.int32, (w, HD_C), 1)
    sink = sink_ref[...]
    for c in range(KVH_C):
        ka, vc = _swa_kv_aug(k_all, v_all, c)
        qas, sinks = [], []
        for gi in range(G_C):
            h = c * G_C + gi
            b1, b2, b3 = _bf16_parts(_swa_slope(h) * LOG2E)
            qpat = jnp.where(qlane == 0, b1, jnp.where(qlane == 1, b2, jnp.where(qlane == 2, b3, 0.0)))
            qa = jnp.concatenate([q_ref[:, h * HD_C:(h + 1) * HD_C] * (HD_C ** -0.5 * LOG2E), qpat], axis=1)
            qas.append(qa.astype(BF16))
            sinks.append((sink[:, h:h + 1] + _swa_slope(h) * qdist) * LOG2E)
        sink2 = jnp.concatenate(sinks, axis=1)
        st = _dot_nt(ka, jnp.concatenate(qas, axis=0)) + maskt
        mx = jnp.maximum(jnp.max(st, axis=0, keepdims=True), sink2)
        pt = jnp.exp2(st - mx)
        den = jnp.sum(pt, axis=0, keepdims=True) + jnp.exp2(sink2 - mx)
        ot = _dot_tn(vc, pt.astype(BF16)) / den
        for gi in range(0, G_C, 2):
            h = c * G_C + gi
            two = jnp.concatenate([ot[:, gi * w:(gi + 1) * w], ot[:, (gi + 1) * w:(gi + 2) * w]], axis=0)
            o_ref[:, h * HD_C:(h + 2) * HD_C] = two.T


def swa_prompt(p, sink_row):
    nblk = SEQ // WINDOW
    kvb = NH_C * HD_C // (2 * KVH_C * HD_C)
    return pl.pallas_call(
        _swa_prompt_kernel,
        grid=(BATCH, nblk),
        in_specs=[pl.BlockSpec((WINDOW, NH_C * HD_C), lambda b, i: (b * nblk + i, 0)),
                  pl.BlockSpec((WINDOW, 2 * KVH_C * HD_C), lambda b, i: (b * nblk + jnp.maximum(i - 1, 0), kvb)),
                  pl.BlockSpec((WINDOW, 2 * KVH_C * HD_C), lambda b, i: (b * nblk + i, kvb)),
                  pl.BlockSpec((1, 128), lambda b, i: (0, 0))],
        out_specs=pl.BlockSpec((WINDOW, NH_C * HD_C), lambda b, i: (b * nblk + i, 0)),
        out_shape=jax.ShapeDtypeStruct((NP, NH_C * HD_C), F32),
        compiler_params=_params("arbitrary", "arbitrary"),
        name="swa_prompt",
    )(p, p, p, sink_row)


def _swa_sample_kernel(q_ref, kn_ref, vn_ref, ck_ref, cv_ref, sink_ref, o_ref):
    w = WINDOW
    nk = w + 8
    nr = G_C * 8
    r = lax.broadcasted_iota(jnp.int32, (nr, nk), 0)
    cc = lax.broadcasted_iota(jnp.int32, (nr, nk), 1)
    di = w + r % 8 - cc
    ok = (di >= 0) & (di <= w) & (cc < w + DEC_SEQ)
    maskt = jnp.where(ok, 0.0, NEG_INF)
    rcol = lax.broadcasted_iota(jnp.int32, (nr, 1), 0)
    gcol = rcol // 8
    qdist = (w + rcol % 8).astype(F32)
    qlane = lax.broadcasted_iota(jnp.int32, (nr, HD_C), 1)
    sink = sink_ref[...]
    qpats, sink2s = [], []
    for c in range(KVH_C):
        heads = [c * G_C + gi for gi in range(G_C)]
        parts = [_bf16_parts(_swa_slope(h) * LOG2E) for h in heads]
        b = [_by_head(gcol, [pp[j] for pp in parts]) for j in range(3)]
        qpats.append(jnp.where(qlane == 0, b[0], jnp.where(qlane == 1, b[1], jnp.where(qlane == 2, b[2], 0.0))))
        slope = _by_head(gcol, [_swa_slope(h) for h in heads])
        sk = _by_head(gcol, [sink[:, h:h + 1] for h in heads])
        sink2s.append((sk + slope * qdist) * LOG2E)
    for sq in range(SWA_SEQ_PER_STEP):
        q = q_ref[sq] * (HD_C ** -0.5 * LOG2E)
        k_all = jnp.concatenate([ck_ref[sq], kn_ref[sq]], axis=0)
        v_all = jnp.concatenate([cv_ref[sq], vn_ref[sq]], axis=0)
        outs = []
        for c in range(KVH_C):
            ka, vc = _swa_kv_aug(k_all, v_all, c)
            qc = jnp.concatenate([q[:, (c * G_C + gi) * HD_C:(c * G_C + gi + 1) * HD_C] for gi in range(G_C)], axis=0)
            qa = jnp.concatenate([qc, qpats[c]], axis=1).astype(BF16)
            oc = _swa_softmax_pv(qa, ka, vc, maskt, sink2s[c])
            outs.extend(oc[gi * 8:(gi + 1) * 8, :] for gi in range(G_C))
        _store_heads(o_ref, outs, lead=(sq,))


def swa_sample(q, k_new, v_new, cache_k, cache_v, sink_row):
    n = SWA_SEQ_PER_STEP
    kw = KVH_C * HD_C
    blk = lambda i: (i, 0, 0)
    return pl.pallas_call(
        _swa_sample_kernel,
        grid=(DEC_BATCH // n,),
        in_specs=[pl.BlockSpec((n, 8, NH_C * HD_C), blk), pl.BlockSpec((n, 8, kw), blk), pl.BlockSpec((n, 8, kw), blk),
                  pl.BlockSpec((n, WINDOW, kw), blk), pl.BlockSpec((n, WINDOW, kw), blk),
                  pl.BlockSpec((1, 128), lambda i: (0, 0))],
        out_specs=pl.BlockSpec((n, 8, NH_C * HD_C), blk),
        out_shape=jax.ShapeDtypeStruct((DEC_BATCH, 8, NH_C * HD_C), F32),
        compiler_params=_params("arbitrary"),
        name="swa_sample",
    )(q, k_new, v_new, cache_k, cache_v, sink_row)


def _router_kernel(x_ref, wt_ref, bt_ref, meta_ref, wts_ref, seg_ref, tri_s, *, tm):
    @pl.when(pl.program_id(0) == 0)
    def _():
        rr = lax.broadcasted_iota(jnp.int32, (128, tm), 0)
        ccn = lax.broadcasted_iota(jnp.int32, (128, tm), 1)
        for ch in range(tm // 128):
            tri_s[ch * 128:(ch + 1) * 128, :] = (rr + ch * 128 < ccn).astype(BF16)

    lt = _dot_nt(wt_ref[...], x_ref[...], HIGHEST) + bt_ref[...]
    row = lax.broadcasted_iota(jnp.int32, lt.shape, 0)
    big = ROUTER_ROWS

    lg = jnp.where(row < N_GROUPS, lt, NEG_INF)
    mg = jnp.max(lg, axis=0, keepdims=True)
    eg = jnp.exp(lg - mg)
    pg = eg / jnp.sum(eg, axis=0, keepdims=True)
    pg1 = jnp.max(pg, axis=0, keepdims=True)
    g1 = jnp.min(jnp.where(pg == pg1, row, big), axis=0, keepdims=True)

    e = row - N_GROUPS
    ingroup = (e >= g1 * N_EXP) & (e < (g1 + 1) * N_EXP)
    le = jnp.where(ingroup, lt, NEG_INF)
    me = jnp.max(le, axis=0, keepdims=True)
    ee = jnp.exp(le - me)
    pe = ee / jnp.sum(ee, axis=0, keepdims=True)
    pe = jnp.where(ingroup, pe, NEG_INF)
    p1 = jnp.max(pe, axis=0, keepdims=True)
    i1 = jnp.min(jnp.where(pe == p1, row, big), axis=0, keepdims=True)
    pe2 = jnp.where(row == i1, NEG_INF, pe)
    p2 = jnp.max(pe2, axis=0, keepdims=True)
    i2 = jnp.min(jnp.where(pe2 == p2, row, big), axis=0, keepdims=True)
    tot = p1 + p2
    w1 = p1 / tot * pg1
    w2 = p2 / tot * pg1

    hit1 = row == i1
    hit2 = row == i2
    oh = jnp.where(hit1 | hit2, 1.0, 0.0)
    ohb = oh.astype(BF16)
    before = _dot(ohb, tri_s[...])
    cnt_col = jnp.sum(oh, axis=1, keepdims=True)
    pad_col = jnp.floor((cnt_col + 7.0) * 0.125) * 8.0
    r128 = lax.broadcasted_iota(jnp.int32, (128, 128), 0)
    c128 = lax.broadcasted_iota(jnp.int32, (128, 128), 1)
    off_col = jnp.dot((c128 < r128).astype(F32), pad_col + jnp.zeros((128, 128), F32),
                      precision=HIGHEST, preferred_element_type=F32)[:, 0:1]
    place = before + off_col
    pos1 = jnp.sum(jnp.where(hit1, place, 0.0), axis=0, keepdims=True)
    pos2 = jnp.sum(jnp.where(hit2, place, 0.0), axis=0, keepdims=True)
    cnt_row = _dot_nt(jnp.ones((8, tm), BF16), ohb)
    pad_row = jnp.floor((cnt_row + 7.0) * 0.125) * 8.0
    off_row = jnp.dot(pad_row, (r128 < c128).astype(F32), precision=HIGHEST, preferred_element_type=F32)

    r8 = lax.broadcasted_iota(jnp.int32, (8, tm), 0)
    meta_ref[...] = jnp.where(r8 == 0, pos1, jnp.where(r8 == 1, pos2, 0.0)).astype(I32)
    wts_ref[...] = jnp.where(r8 == 0, w1, jnp.where(r8 == 1, w2, 0.0))
    s8 = lax.broadcasted_iota(jnp.int32, (8, 128), 0)
    seg_ref[...] = jnp.where(s8 == 0, off_row, jnp.where(s8 == 1, cnt_row, 0.0)).astype(I32)


def moe_router(x, wt_router, bt_router, tm=MOE_TM):
    m = x.shape[0]
    nt = m // tm
    blk = lambda i: (i, 0, 0)
    return pl.pallas_call(
        functools.partial(_router_kernel, tm=tm),
        grid=(nt,),
        in_specs=[pl.BlockSpec((tm, D_MODEL), lambda i: (i, 0)),
                  pl.BlockSpec((ROUTER_ROWS, D_MODEL), lambda i: (0, 0)),
                  pl.BlockSpec((ROUTER_ROWS, 1), lambda i: (0, 0))],
        out_specs=[pl.BlockSpec((None, 8, tm), blk), pl.BlockSpec((None, 8, tm), blk),
                   pl.BlockSpec((None, 8, 128), blk)],
        out_shape=[jax.ShapeDtypeStruct((nt, 8, tm), I32), jax.ShapeDtypeStruct((nt, 8, tm), F32),
                   jax.ShapeDtypeStruct((nt, 8, 128), I32)],
        scratch_shapes=[pltpu.VMEM((tm, tm), BF16)],
        compiler_params=_params("arbitrary"),
        name="moe_router",
    )(x, wt_router, bt_router)


def _moe_kernel(x_ref, meta_hbm, wts_hbm, seg_hbm, wg_hbm, wu_hbm, wd_hbm, g_ref, b_ref, o_ref,
                xs, wbg, wbu, wbd, wsem, meta_s, wts_s, seg_s, msem, *, layer, tm):
    i = pl.program_id(0)
    nt = pl.num_programs(0)
    rc = MOE_RC
    ahead = MOE_WSLOTS - 1
    base = i * N_EXPERTS
    ms = i % 2

    def weight_copies(step):
        e = step % N_EXPERTS
        sl = step % MOE_WSLOTS
        return (pltpu.make_async_copy(wg_hbm.at[layer, e], wbg.at[sl], wsem.at[sl, 0]),
                pltpu.make_async_copy(wu_hbm.at[layer, e], wbu.at[sl], wsem.at[sl, 1]),
                pltpu.make_async_copy(wd_hbm.at[layer, e], wbd.at[sl], wsem.at[sl, 2]))

    def meta_copies(tile, sl):
        return (pltpu.make_async_copy(meta_hbm.at[tile], meta_s.at[sl], msem.at[sl, 0]),
                pltpu.make_async_copy(wts_hbm.at[tile], wts_s.at[sl], msem.at[sl, 1]),
                pltpu.make_async_copy(seg_hbm.at[tile], seg_s.at[sl], msem.at[sl, 2]))

    @pl.when(i == 0)
    def _():
        for cp in meta_copies(0, 0):
            cp.start()
        for step in range(ahead):
            for cp in weight_copies(step):
                cp.start()
        xs[...] = jnp.zeros_like(xs)

    for cp in meta_copies(i, ms):
        cp.wait()

    @pl.when(i + 1 < nt)
    def _():
        for cp in meta_copies(i + 1, 1 - ms):
            cp.start()

    def gather(t, carry):
        rowv = x_ref[pl.ds(t, 1), :]
        xs[pl.ds(meta_s[ms, 0, t], 1), :] = rowv
        xs[pl.ds(meta_s[ms, 1, t], 1), :] = rowv
        return carry

    lax.fori_loop(0, tm, gather, 0, unroll=8)

    def expert(e, carry):
        step = base + e
        sl = step % MOE_WSLOTS
        for cp in weight_copies(step):
            cp.wait()

        @pl.when(step + ahead < nt * N_EXPERTS)
        def _():
            for cp in weight_copies(step + ahead):
                cp.start()

        off = seg_s[ms, 0, N_GROUPS + e]
        cnt = seg_s[ms, 1, N_GROUPS + e]
        wgb = wbg[sl]
        wub = wbu[sl]
        wdb = wbd[sl]
        ridx = lax.broadcasted_iota(jnp.int32, (rc, 1), 0)

        def chunk(k, c2):
            row0 = pl.multiple_of(off + k * rc, 8)
            lhs = xs[pl.ds(row0, rc), :]
            lb = lhs.astype(BF16)
            hg = _dot(lb, wgb)
            hu = _dot(lb, wub)
            hh = hg / (1.0 + jnp.exp(-hg)) * hu
            out = _dot(hh.astype(BF16), wdb)
            xs[pl.ds(row0, rc), :] = jnp.where(ridx < cnt - k * rc, out, lhs)
            return c2

        lax.fori_loop(0, (cnt + rc - 1) // rc, chunk, 0)
        return carry

    lax.fori_loop(0, N_EXPERTS, expert, 0)

    def combine(t, carry):
        y = (wts_s[ms, 0, t] * xs[pl.ds(meta_s[ms, 0, t], 1), :]
             + wts_s[ms, 1, t] * xs[pl.ds(meta_s[ms, 1, t], 1), :])
        o_ref[pl.ds(t, 1), :] = y
        return carry

    lax.fori_loop(0, tm, combine, 0, unroll=8)
    o_ref[...] = _layer_norm(ALPHA * x_ref[...] + o_ref[...], g_ref[...], b_ref[...])


def moe_ffn_ln(x, meta, wts, seg, wg, wu, wd, g, b, layer, tm=MOE_TM):
    m = x.shape[0]
    row = lambda i: (i, 0)
    fixed = lambda i: (0, 0)
    hbm = pl.BlockSpec(memory_space=pl.ANY)
    return pl.pallas_call(
        functools.partial(_moe_kernel, layer=layer, tm=tm),
        grid=(m // tm,),
        in_specs=[pl.BlockSpec((tm, D_MODEL), row), hbm, hbm, hbm, hbm, hbm, hbm,
                  pl.BlockSpec((1, D_MODEL), fixed), pl.BlockSpec((1, D_MODEL), fixed)],
        out_specs=pl.BlockSpec((tm, D_MODEL), row),
        out_shape=jax.ShapeDtypeStruct((m, D_MODEL), F32),
        scratch_shapes=[pltpu.VMEM((MOE_ROWS, D_MODEL), F32),
                        pltpu.VMEM((MOE_WSLOTS, D_MODEL, D_FF), BF16), pltpu.VMEM((MOE_WSLOTS, D_MODEL, D_FF), BF16),
                        pltpu.VMEM((MOE_WSLOTS, D_FF, D_MODEL), BF16), pltpu.SemaphoreType.DMA((MOE_WSLOTS, 3)),
                        pltpu.SMEM((2, 8, tm), I32), pltpu.SMEM((2, 8, tm), F32), pltpu.SMEM((2, 8, 128), I32),
                        pltpu.SemaphoreType.DMA((2, 3))],
        compiler_params=_params("arbitrary"),
        name="moe_ffn_ln",
    )(x, meta, wts, seg, wg, wu, wd, g.reshape(1, D_MODEL), b.reshape(1, D_MODEL))


def _pad_rows(a, rows):
    return jnp.pad(a, ((0, 0), (0, rows - a.shape[1]), (0, 0)))


def _mlstm_layer(x, w_in, b_gate, norm_g, w_out, c0_all, n0_all, m0_all, slot):
    w = jnp.pad(w_in, ((0, 0), (0, M_IN_PAD - M_IN))).astype(BF16)
    p = matmul(x, w)
    bg = jnp.pad(b_gate, (0, 128 - 2 * NH_M)).reshape(1, 128)
    ng = norm_g.reshape(1, M_V)
    hp, cp, np_, mp = mlstm_prompt(p, bg, ng)
    ps = _pad_rows(p[NP:].reshape(DEC_BATCH, DEC_SEQ, M_IN_PAD), 8).reshape(DEC_BATCH * 8, M_IN_PAD)
    hs, cs, ns, ms = mlstm_sample(ps, c0_all, n0_all, m0_all, slot, bg, ng)
    hs = hs.reshape(DEC_BATCH, 8, M_V)[:, :DEC_SEQ].reshape(NS, M_V)
    states = (cp, np_, mp, cs, ns, ms.reshape(DEC_BATCH, NH_M))
    return (hp, hs), w_out.astype(BF16), states


def _moba_sample_part(ps, cache_k, cache_v, page_rows):
    hq = NH_B * HD_B
    hk = KVH_B * HD_B
    slopes_h = jnp.exp2(-8.0 * jnp.arange(1, NH_B + 1, dtype=F32) / NH_B)
    q = (ps[:, :hq] * (HD_B ** -0.5)).reshape(DEC_BATCH, DEC_SEQ, KVH_B, G_B, HD_B)
    eye = jnp.eye(KVH_B, dtype=F32)
    nrow = KVH_B * DEC_SEQ * G_B
    q_aug_t = jnp.einsum("bscgd,ce->besgcd", q, eye).reshape(DEC_BATCH, nrow, hk)
    q_aug = jnp.pad(jnp.swapaxes(q_aug_t, 1, 2), ((0, 0), (0, 0), (0, 128 - nrow)))
    k_new = ps[:, hq:hq + hk].reshape(DEC_BATCH, DEC_SEQ, hk)
    v_new = ps[:, hq + hk:].reshape(DEC_BATCH, DEC_SEQ, hk)
    row = jnp.arange(nrow)
    row_c, row_s, row_g = row // (DEC_SEQ * G_B), (row // G_B) % DEC_SEQ, row % G_B
    slope_row = (slopes_h[row_c * G_B + row_g] * LOG2E)[:, None]
    qs_row = row_s.astype(F32)[:, None]
    past_bias = -slope_row * (float(PAST) + qs_row - jnp.arange(PAST, dtype=F32)[None, :])
    tok = jnp.arange(PAGE_SIZE, dtype=F32)[None, :]
    own_bias = jnp.where((tok <= qs_row) & (tok < float(DEC_SEQ)), -slope_row * (qs_row - tok), NEG_INF)
    bias = jnp.concatenate([past_bias, own_bias], axis=1)
    ot = moba_sample(page_rows, q_aug, q_aug_t, _pad_rows(k_new, 8), _pad_rows(v_new, 8), bias, cache_k, cache_v)
    ot = ot.reshape(DEC_BATCH, KVH_B, HD_B, KVH_B, DEC_SEQ, G_B)
    os_ = jnp.einsum("bcdesg,ce->bscgd", ot, eye).reshape(NS, hq)
    return os_, k_new, v_new


def _moba_layer(x, w_in, w_out, cache_k, cache_v, page_rows):
    p = matmul(x, w_in.astype(BF16))
    hq = NH_B * HD_B
    hk = KVH_B * HD_B
    ac = _moba_aug_const()
    q_aug, k_aug, v_t = moba_select(p, ac)
    op = moba_flash(q_aug, k_aug, v_t)
    os_, k_new, v_new = _moba_sample_part(p[NP:], cache_k, cache_v, page_rows)
    kv = (p[:NP, hq:hq + hk].reshape(BATCH, SEQ, KVH_B, HD_B), p[:NP, hq + hk:].reshape(BATCH, SEQ, KVH_B, HD_B),
          k_new.reshape(DEC_BATCH, DEC_SEQ, KVH_B, HD_B), v_new.reshape(DEC_BATCH, DEC_SEQ, KVH_B, HD_B))
    return (op, os_), w_out.astype(BF16), kv


def _swa_layer(x, w_in, sinks, w_out, cache_k, cache_v):
    p = matmul(x, w_in.astype(BF16))
    hq = NH_C * HD_C
    kw = KVH_C * HD_C
    sink_row = jnp.pad(sinks, (0, 128 - NH_C)).reshape(1, 128)
    op = swa_prompt(p, sink_row)
    ps = p[NP:]
    q = _pad_rows(ps[:, :hq].reshape(DEC_BATCH, DEC_SEQ, hq), 8)
    k_new = ps[:, hq:hq + kw].reshape(DEC_BATCH, DEC_SEQ, kw)
    v_new = ps[:, hq + kw:].reshape(DEC_BATCH, DEC_SEQ, kw)
    ck = cache_k.reshape(DEC_BATCH, WINDOW, kw)
    cv = cache_v.reshape(DEC_BATCH, WINDOW, kw)
    os_ = swa_sample(q, _pad_rows(k_new, 8), _pad_rows(v_new, 8), ck, cv, sink_row)[:, :DEC_SEQ].reshape(NS, hq)
    pp = p[:NP].reshape(BATCH, SEQ, C_IN)[:, SEQ - WINDOW:]
    kv = (pp[..., hq:hq + kw].reshape(BATCH, WINDOW, KVH_C, HD_C), pp[..., hq + kw:].reshape(BATCH, WINDOW, KVH_C, HD_C),
          jnp.concatenate([ck, k_new], axis=1)[:, DEC_SEQ:].reshape(DEC_BATCH, WINDOW, KVH_C, HD_C),
          jnp.concatenate([cv, v_new], axis=1)[:, DEC_SEQ:].reshape(DEC_BATCH, WINDOW, KVH_C, HD_C))
    return (op, os_), w_out.astype(BF16), kv


def _moe_layer(x, w_group, b_group, w_router, b_router, w_gate, w_up, w_down, g, b, layer):
    pad = ROUTER_ROWS - N_GROUPS - N_EXPERTS
    wt = jnp.concatenate([w_group.T, w_router.T, jnp.zeros((pad, D_MODEL), F32)], axis=0)
    bt = jnp.concatenate([b_group, b_router, jnp.zeros((pad,), F32)]).reshape(ROUTER_ROWS, 1)
    meta, wts, seg = moe_router(x, wt, bt)
    return moe_ffn_ln(x, meta, wts, seg, w_gate, w_up, w_down, g, b, layer)


def kernel(x_prompt, x_sample, state_mlstm_C, state_mlstm_n, state_mlstm_m, cache_moba_k, cache_moba_v, cache_swa_k, cache_swa_v, page_table, mlstm_w_in, mlstm_b_gate, mlstm_norm_g, mlstm_w_out, moba_w_in, moba_w_out, swa_w_in, swa_sinks, swa_w_out, ln_mix_g, ln_mix_b, ln_ffn_g, ln_ffn_b, moe_w_group, moe_b_group, moe_w_router, moe_b_router, moe_w_gate, moe_w_up, moe_w_down):
    x = jnp.concatenate([x_prompt.reshape(NP, D_MODEL), x_sample.reshape(NS, D_MODEL)], axis=0)
    n_pool = cache_moba_k.shape[1]
    moba_k = cache_moba_k.reshape(-1, PAGE_SIZE * KVH_B, HD_B)
    moba_v = cache_moba_v.reshape(-1, PAGE_SIZE * KVH_B, HD_B)
    wg_all, wu_all, wd_all = moe_w_gate.astype(BF16), moe_w_up.astype(BF16), moe_w_down.astype(BF16)
    m_states, b_kv, c_kv = [], [], []
    for layer in range(DEPTH):
        kind, slot = layer % 3, layer // 3
        if kind == 0:
            a, w_out, st = _mlstm_layer(x, mlstm_w_in[slot], mlstm_b_gate[slot], mlstm_norm_g[slot], mlstm_w_out[slot],
                                        state_mlstm_C, state_mlstm_n, state_mlstm_m, slot)
            m_states.append(st)
        elif kind == 1:
            a, w_out, kv = _moba_layer(x, moba_w_in[slot], moba_w_out[slot], moba_k, moba_v,
                                       page_table + slot * n_pool)
            b_kv.append(kv)
        else:
            a, w_out, kv = _swa_layer(x, swa_w_in[slot], swa_sinks[slot], swa_w_out[slot], cache_swa_k[slot],
                                      cache_swa_v[slot])
            c_kv.append(kv)
        x = outproj_ln(a[0], a[1], w_out, x, ln_mix_g[layer], ln_mix_b[layer])
        x = _moe_layer(x, moe_w_group[layer], moe_b_group[layer], moe_w_router[layer], moe_b_router[layer],
                       wg_all, wu_all, wd_all, ln_ffn_g[layer], ln_ffn_b[layer], layer)
    stack = lambda items, j: jnp.stack([it[j] for it in items])
    return (x[:NP].reshape(BATCH, SEQ, D_MODEL), x[NP:].reshape(DEC_BATCH, DEC_SEQ, D_MODEL),
            stack(m_states, 0), stack(m_states, 1), stack(m_states, 2),
            stack(m_states, 3), stack(m_states, 4), stack(m_states, 5),
            stack(b_kv, 0), stack(b_kv, 1), stack(b_kv, 2), stack(b_kv, 3),
            stack(c_kv, 0), stack(c_kv, 1), stack(c_kv, 2), stack(c_kv, 3))
```

```python
import functools

import jax
import jax.numpy as jnp
import numpy as np
from jax import lax
from jax.experimental import pallas as pl
from jax.experimental.pallas import tpu as pltpu

F32 = jnp.float32
BF16 = jnp.bfloat16
I32 = jnp.int32
HIGHEST = lax.Precision.HIGHEST

D_MODEL = 1024
BATCH = 2
SEQ = 8192
DEPTH = 4
DEC_BATCH = 128
DEC_SEQ = 4
PAGE_SIZE = 128
N_PAGES = 64
PAST = N_PAGES * PAGE_SIZE
NP = BATCH * SEQ
NS = DEC_BATCH * DEC_SEQ
NT = NP + NS

NH_M = 4
DK_M = 128
DV_M = 256
M_CHUNK = 128
M_QK = NH_M * DK_M
M_V = NH_M * DV_M
M_IN = 2 * M_QK + 2 * M_V + 2 * NH_M
M_IN_PAD = 3200
M_GATE_COL = 2 * M_QK + 2 * M_V
MLSTM_SEQ_PER_STEP = 4

NH_B = 8
KVH_B = 4
G_B = 2
HD_B = 128
MOBA_BLOCK = 256
MOBA_TOPK = 3
NB_B = SEQ // MOBA_BLOCK
NB_S = PAST // MOBA_BLOCK
B_IN = (NH_B + 2 * KVH_B) * HD_B
SAMPLE_PAGES_PER_STEP = 16
MOBA_QTILE = 2 * MOBA_BLOCK
AUG = 2 * HD_B
POS_LANE = NB_B

NH_C = 16
KVH_C = 2
G_C = 8
HD_C = 64
WINDOW = 128
C_IN = (NH_C + 2 * KVH_C) * HD_C
SWA_SEQ_PER_STEP = 8

N_GROUPS = 4
N_EXP = 8
N_EXPERTS = 32
D_FF = 256
ROUTER_ROWS = 128
MOE_TM = 1536
MOE_RC = 128
MOE_WSLOTS = 4
MOE_ROWS = 2 * MOE_TM + N_EXPERTS * 8 + MOE_RC

ALPHA = (2.0 * DEPTH) ** 0.25
EPS = 1e-5
LOG2E = 1.4426950408889634
NEG_INF = float("-inf")
MASKED = -1e30
VMEM_LIMIT = 56 * 1024 * 1024


def _params(*sem):
    return pltpu.CompilerParams(dimension_semantics=sem, vmem_limit_bytes=VMEM_LIMIT)


def _dot(a, b):
    return jnp.dot(a, b, preferred_element_type=F32)


def _dot_nt(a, b, precision=None):
    return lax.dot_general(a, b, (((1,), (1,)), ((), ())), precision=precision, preferred_element_type=F32)


def _dot_tn(a, b):
    return lax.dot_general(a, b, (((0,), (0,)), ((), ())), preferred_element_type=F32)


def _layer_norm(z, g, b):
    mu = jnp.mean(z, axis=-1, keepdims=True)
    zc = z - mu
    var = jnp.mean(zc * zc, axis=-1, keepdims=True)
    return zc * lax.rsqrt(var + EPS) * g + b


def _bf16_parts(x, n=3):
    parts = []
    for _ in range(n):
        bits = int(np.float32(x).view(np.uint32))
        rounded = ((bits + 0x7FFF + ((bits >> 16) & 1)) >> 16) << 16
        p = float(np.uint32(rounded & 0xFFFFFFFF).view(np.float32))
        parts.append(p)
        x = x - p
    return parts


def _mm_kernel(x_ref, w_ref, o_ref):
    o_ref[...] = _dot(x_ref[...].astype(BF16), w_ref[...])


def matmul(x, w_bf16, tm=512):
    m, k = x.shape
    n = w_bf16.shape[1]
    return pl.pallas_call(
        _mm_kernel,
        grid=(m // tm,),
        in_specs=[pl.BlockSpec((tm, k), lambda i: (i, 0)), pl.BlockSpec((k, n), lambda i: (0, 0))],
        out_specs=pl.BlockSpec((tm, n), lambda i: (i, 0)),
        out_shape=jax.ShapeDtypeStruct((m, n), F32),
        compiler_params=_params("arbitrary"),
        name="in_proj",
    )(x, w_bf16)


def _outproj_ln_kernel(ap_ref, as_ref, w_ref, r_ref, g_ref, b_ref, o_ref, *, n_p):
    a = jnp.where(pl.program_id(0) < n_p, ap_ref[...], as_ref[...])
    y = _dot(a.astype(BF16), w_ref[...])
    o_ref[...] = _layer_norm(ALPHA * r_ref[...] + y, g_ref[...], b_ref[...])


def outproj_ln(a_p, a_s, w_bf16, resid, g, b, tm=512):
    k = a_p.shape[1]
    n = w_bf16.shape[1]
    n_p = a_p.shape[0] // tm
    m = a_p.shape[0] + a_s.shape[0]
    row = lambda i: (i, 0)
    fixed = lambda i: (0, 0)
    return pl.pallas_call(
        functools.partial(_outproj_ln_kernel, n_p=n_p),
        grid=(m // tm,),
        in_specs=[pl.BlockSpec((tm, k), lambda i: (jnp.minimum(i, n_p - 1), 0)),
                  pl.BlockSpec((tm, k), lambda i: (jnp.maximum(i - n_p, 0), 0)),
                  pl.BlockSpec((k, n), fixed), pl.BlockSpec((tm, n), row),
                  pl.BlockSpec((1, n), fixed), pl.BlockSpec((1, n), fixed)],
        out_specs=pl.BlockSpec((tm, n), row),
        out_shape=jax.ShapeDtypeStruct((m, n), F32),
        compiler_params=_params("arbitrary"),
        name="out_proj_ln",
    )(a_p, a_s, w_bf16, resid, g.reshape(1, n), b.reshape(1, n))


def _log_sigmoid(x):
    return jnp.minimum(x, 0.0) - jnp.log1p(jnp.exp(-jnp.abs(x)))


def _mlstm_chunk(p_ref, bg_ref, ng_ref, hg_ref, c_s, n_s, m_s, *, L, L_real):
    gates = p_ref[:, M_GATE_COL:M_IN_PAD] + bg_ref[...]
    lane = lax.broadcasted_iota(jnp.int32, (L, 128), 1)
    x = jnp.where(lane < NH_M, gates, _log_sigmoid(gates))
    if L_real < L:
        tok = lax.broadcasted_iota(jnp.int32, (L, 128), 0)
        x = jnp.where(tok < L_real, x, jnp.where(lane < NH_M, -1e30, 0.0))
    r = lax.broadcasted_iota(jnp.int32, (L, L), 0)
    s = lax.broadcasted_iota(jnp.int32, (L, L), 1)
    causal = r >= s
    tri = causal.astype(F32)
    bcum = jnp.dot(tri, x, precision=HIGHEST, preferred_element_type=F32)
    eye8 = (lax.broadcasted_iota(jnp.int32, (8, 128), 0) == lax.broadcasted_iota(jnp.int32, (8, 128), 1)).astype(F32)
    x_rows = _dot_nt(eye8, x, HIGHEST)
    b_rows = _dot_nt(eye8, bcum, HIGHEST)

    for h in range(NH_M):
        li_row = x_rows[h:h + 1, :]
        li_col = x[:, h:h + 1]
        b_row = b_rows[NH_M + h:NH_M + h + 1, :]
        b_col = bcum[:, NH_M + h:NH_M + h + 1]
        m_old = m_s[:, h:h + 1]
        dm = jnp.where(causal, b_col - b_row + li_row, NEG_INF)
        g_col = b_col + m_old
        mq = jnp.maximum(g_col, jnp.max(dm, axis=1, keepdims=True))
        w_intra = jnp.exp(dm - mq)
        w_inter = jnp.exp(g_col - mq)
        q = p_ref[:, h * DK_M:(h + 1) * DK_M] * (DK_M ** -0.5)
        k = p_ref[:, M_QK + h * DK_M:M_QK + (h + 1) * DK_M]
        v = p_ref[:, 2 * M_QK + h * DV_M:2 * M_QK + (h + 1) * DV_M]
        qb = q.astype(BF16)
        kb = k.astype(BF16)
        cmat = c_s[h]
        nrow = n_s[h:h + 1, :]
        sc = _dot_nt(qb, kb) * w_intra
        num = _dot(sc.astype(BF16), v.astype(BF16)) + w_inter * _dot(qb, cmat.astype(BF16))
        den = jnp.sum(sc, axis=1, keepdims=True) + w_inter * jnp.sum(q * nrow, axis=1, keepdims=True)
        hh = num / jnp.maximum(jnp.abs(den), jnp.exp(-mq))
        bl = b_col[L - 1:L, :]
        lw = bl - b_col + li_col
        m_new = jnp.maximum(bl + m_old, jnp.max(lw, axis=0, keepdims=True))
        wl = jnp.exp(lw - m_new)
        dec = jnp.exp(bl + m_old - m_new)
        c_s[h] = dec * cmat + _dot_tn(kb, (wl * v).astype(BF16))
        n_s[h:h + 1, :] = dec * nrow + jnp.sum(wl * k, axis=0, keepdims=True)
        m_s[:, h:h + 1] = m_new
        mu = jnp.mean(hh, axis=1, keepdims=True)
        hc = hh - mu
        var = jnp.mean(hc * hc, axis=1, keepdims=True)
        hn = hc * lax.rsqrt(var + EPS) * ng_ref[:, h * DV_M:(h + 1) * DV_M]
        o = p_ref[:, 2 * M_QK + M_V + h * DV_M:2 * M_QK + M_V + (h + 1) * DV_M]
        hg_ref[:, h * DV_M:(h + 1) * DV_M] = hn / (1.0 + jnp.exp(-o))


def _mlstm_prompt_kernel(*refs, L, nseq):
    p_refs = refs[:nseq]
    bg_ref, ng_ref, hg_ref, co_ref, no_ref, mo_ref, c_s, n_s, m_s = refs[nseq:]
    c = pl.program_id(0)

    @pl.when(c == 0)
    def _():
        c_s[...] = jnp.zeros_like(c_s)
        n_s[...] = jnp.zeros_like(n_s)
        m_s[...] = jnp.zeros_like(m_s)

    for s in range(nseq):
        _mlstm_chunk(p_refs[s], bg_ref, ng_ref, hg_ref.at[s], c_s.at[s], n_s.at[s], m_s.at[s], L=L, L_real=L)

    @pl.when(c == pl.num_programs(0) - 1)
    def _():
        co_ref[...] = c_s[...]
        no_ref[...] = n_s[...]
        mo_ref[...] = m_s[...]


def mlstm_prompt(p, b_gate_row, norm_g_row):
    L = M_CHUNK
    nc = SEQ // L
    fixed = lambda c: (0, 0)
    whole3 = lambda c: (0, 0, 0)
    p_specs = [pl.BlockSpec((L, M_IN_PAD), functools.partial(lambda c, s: (s * nc + c, 0), s=s)) for s in range(BATCH)]
    hg, cst, nst, mst = pl.pallas_call(
        functools.partial(_mlstm_prompt_kernel, L=L, nseq=BATCH),
        grid=(nc,),
        in_specs=p_specs + [pl.BlockSpec((1, 128), fixed), pl.BlockSpec((1, M_V), fixed)],
        out_specs=[pl.BlockSpec((BATCH, L, M_V), lambda c: (0, c, 0)),
                   pl.BlockSpec((BATCH, NH_M, DK_M, DV_M), lambda c: (0, 0, 0, 0)),
                   pl.BlockSpec((BATCH, NH_M, DK_M), whole3),
                   pl.BlockSpec((BATCH, 1, NH_M), whole3)],
        out_shape=[jax.ShapeDtypeStruct((BATCH, SEQ, M_V), F32),
                   jax.ShapeDtypeStruct((BATCH, NH_M, DK_M, DV_M), F32),
                   jax.ShapeDtypeStruct((BATCH, NH_M, DK_M), F32),
                   jax.ShapeDtypeStruct((BATCH, 1, NH_M), F32)],
        scratch_shapes=[pltpu.VMEM((BATCH, NH_M, DK_M, DV_M), F32), pltpu.VMEM((BATCH, NH_M, DK_M), F32),
                        pltpu.VMEM((BATCH, 1, NH_M), F32)],
        compiler_params=_params("arbitrary"),
        name="mlstm_prompt",
    )(*([p] * BATCH), b_gate_row, norm_g_row)
    return hg.reshape(NP, M_V), cst, nst, mst.reshape(BATCH, NH_M)


def _mlstm_sample_kernel(p_ref, c0_ref, n0_ref, m0_ref, bg_ref, ng_ref, hg_ref, co_ref, no_ref, mo_ref,
                         *, L, L_real, nseq):
    co_ref[...] = c0_ref[...]
    no_ref[...] = n0_ref[...]
    mo_ref[...] = m0_ref[...]
    for s in range(nseq):
        _mlstm_chunk(p_ref.at[pl.ds(s * L, L)], bg_ref, ng_ref, hg_ref.at[pl.ds(s * L, L)],
                     co_ref.at[s], no_ref.at[s], mo_ref.at[s], L=L, L_real=L_real)


def mlstm_sample(p, c0_all, n0_all, m0_all, slot, b_gate_row, norm_g_row, nseq=MLSTM_SEQ_PER_STEP):
    L = 8
    nb = DEC_BATCH
    m0_all = m0_all.reshape(m0_all.shape[0], nb, 1, NH_M)
    fixed = lambda b: (0, 0)
    per_b = lambda b: (b, 0, 0)
    return pl.pallas_call(
        functools.partial(_mlstm_sample_kernel, L=L, L_real=DEC_SEQ, nseq=nseq),
        grid=(nb // nseq,),
        in_specs=[pl.BlockSpec((nseq * L, M_IN_PAD), lambda b: (b, 0)),
                  pl.BlockSpec((None, nseq, NH_M, DK_M, DV_M), lambda b: (slot, b, 0, 0, 0)),
                  pl.BlockSpec((None, nseq, NH_M, DK_M), lambda b: (slot, b, 0, 0)),
                  pl.BlockSpec((None, nseq, 1, NH_M), lambda b: (slot, b, 0, 0)),
                  pl.BlockSpec((1, 128), fixed),
                  pl.BlockSpec((1, M_V), fixed)],
        out_specs=[pl.BlockSpec((nseq * L, M_V), lambda b: (b, 0)),
                   pl.BlockSpec((nseq, NH_M, DK_M, DV_M), lambda b: (b, 0, 0, 0)),
                   pl.BlockSpec((nseq, NH_M, DK_M), per_b),
                   pl.BlockSpec((nseq, 1, NH_M), per_b)],
        out_shape=[jax.ShapeDtypeStruct((nb * L, M_V), F32),
                   jax.ShapeDtypeStruct((nb, NH_M, DK_M, DV_M), F32),
                   jax.ShapeDtypeStruct((nb, NH_M, DK_M), F32),
                   jax.ShapeDtypeStruct((nb, 1, NH_M), F32)],
        compiler_params=_params("arbitrary"),
        name="mlstm_sample",
    )(p, c0_all, n0_all, m0_all, b_gate_row, norm_g_row)


def _topk_mask_lanes(g, k):
    lane = lax.broadcasted_iota(jnp.int32, g.shape, 1)
    n = g.shape[1]
    sel = jnp.zeros(g.shape, F32)
    for _ in range(k):
        mx = jnp.max(g, axis=1, keepdims=True)
        idx = jnp.min(jnp.where(g == mx, lane, n), axis=1, keepdims=True)
        hit = (lane == idx) & (mx > NEG_INF)
        sel = jnp.where(hit, 1.0, sel)
        g = jnp.where(lane == idx, NEG_INF, g)
    return sel


def _moba_select_kernel(q_ref, k_ref, v_ref, ac_ref, qa_ref, ka_ref, vt_ref, kmean_s, *, tq):
    i = pl.program_id(2)

    @pl.when(i == 0)
    def _():
        kmean_s[...] = jnp.zeros_like(kmean_s)
        kmean_s[0:NB_B, :] = jnp.sum(k_ref[...].reshape(NB_B, MOBA_BLOCK, HD_B), axis=1) * (1.0 / MOBA_BLOCK)
        rows = 1024
        lane = lax.broadcasted_iota(jnp.int32, (rows, HD_B), 1)
        rloc = lax.broadcasted_iota(jnp.int32, (rows, HD_B), 0)
        for ch in range(SEQ // rows):
            vt_ref[:, ch * rows:(ch + 1) * rows] = v_ref[ch * rows:(ch + 1) * rows, :].T.astype(BF16)
            row = rloc + ch * rows
            onehot = (lane == row // MOBA_BLOCK).astype(F32)
            offs = (row % MOBA_BLOCK).astype(F32)
            blks = (row // MOBA_BLOCK).astype(F32)
            pat = jnp.where(lane < NB_B, onehot,
                            jnp.where(lane < POS_LANE + 3, offs, jnp.where(lane < POS_LANE + 6, blks, 0.0)))
            ka_ref[ch * rows:(ch + 1) * rows, 0:HD_B] = k_ref[ch * rows:(ch + 1) * rows, :].astype(BF16)
            ka_ref[ch * rows:(ch + 1) * rows, HD_B:AUG] = pat.astype(BF16)

    km = kmean_s[...]
    lane = lax.broadcasted_iota(jnp.int32, (tq, HD_B), 1)
    blk = lax.broadcasted_iota(jnp.int32, (NB_B, tq), 0)
    own = (i * tq + lax.broadcasted_iota(jnp.int32, (NB_B, tq), 1)) // MOBA_BLOCK
    for g in range(G_B):
        q = q_ref[:, g * HD_B:(g + 1) * HD_B]
        gate = _dot_nt(km, q * (HD_B ** -0.5), HIGHEST)[0:NB_B, :]
        gate = jnp.where(blk < own, gate, NEG_INF)
        sel = jnp.zeros((NB_B, tq), F32)
        for _ in range(MOBA_TOPK):
            mx = jnp.max(gate, axis=0, keepdims=True)
            idx = jnp.min(jnp.where(gate == mx, blk, NB_B), axis=0, keepdims=True)
            sel = jnp.where((blk == idx) & (mx > NEG_INF), 1.0, sel)
            gate = jnp.where(blk == idx, NEG_INF, gate)
        sel = jnp.where(blk == own, 1.0, sel)
        maskt = jnp.concatenate([jnp.where(sel > 0.5, 0.0, MASKED), jnp.zeros((128 - NB_B, tq), F32)], axis=0)
        aug = jnp.where(lane < NB_B, maskt.T, ac_ref[g, 0:1, :])
        qa_ref[:, g * AUG:g * AUG + HD_B] = (q * (HD_B ** -0.5 * LOG2E)).astype(BF16)
        qa_ref[:, g * AUG + HD_B:(g + 1) * AUG] = aug.astype(BF16)


def moba_select(p, aug_const, tq=1024):
    kern = functools.partial(_moba_select_kernel, tq=tq)
    nq = SEQ // tq
    per_bc = lambda b, c, i: (b, c, 0, 0)
    return pl.pallas_call(
        kern,
        grid=(BATCH, KVH_B, nq),
        in_specs=[pl.BlockSpec((tq, G_B * HD_B), lambda b, c, i: (b * nq + i, c)),
                  pl.BlockSpec((SEQ, HD_B), lambda b, c, i: (b, NH_B + c)),
                  pl.BlockSpec((SEQ, HD_B), lambda b, c, i: (b, NH_B + KVH_B + c)),
                  pl.BlockSpec((G_B, 8, 128), lambda b, c, i: (c, 0, 0))],
        out_specs=[pl.BlockSpec((tq, G_B * AUG), lambda b, c, i: (b * nq + i, c)),
                   pl.BlockSpec((None, None, SEQ, AUG), per_bc),
                   pl.BlockSpec((None, None, HD_B, SEQ), per_bc)],
        out_shape=[jax.ShapeDtypeStruct((NP, NH_B * AUG), BF16),
                   jax.ShapeDtypeStruct((BATCH, KVH_B, SEQ, AUG), BF16),
                   jax.ShapeDtypeStruct((BATCH, KVH_B, HD_B, SEQ), BF16)],
        scratch_shapes=[pltpu.VMEM((128, HD_B), F32)],
        compiler_params=_params("arbitrary", "arbitrary", "arbitrary"),
        name="moba_select",
    )(p, p, p, aug_const)


def _moba_flash_kernel(qa_ref, ka_ref, vt_ref, o_ref, s_buf):
    i = pl.program_id(2)
    bq = MOBA_QTILE
    tk = MOBA_QTILE
    nl = G_B * bq
    qa = jnp.concatenate([qa_ref[:, g * AUG:(g + 1) * AUG] for g in range(G_B)], axis=0)

    def scores(t, slot):
        off = pl.multiple_of(t * tk, tk)
        s_buf[slot] = _dot_nt(ka_ref[pl.ds(off, tk), :], qa)

    def softmax_pv(t, slot, last, carry):
        m, l, acc = carry
        st = s_buf[slot]
        if last:
            kpos = t * tk + lax.broadcasted_iota(jnp.int32, (tk, nl), 0)
            qpos = i * bq + lax.broadcasted_iota(jnp.int32, (tk, nl), 1) % bq
            st = jnp.where(kpos <= qpos, st, NEG_INF)
        m_new = jnp.maximum(m, jnp.max(st, axis=0, keepdims=True))
        pt = jnp.exp2(st - m_new)
        a = jnp.exp2(m - m_new)
        l = a * l + jnp.sum(pt, axis=0, keepdims=True)
        off = pl.multiple_of(t * tk, tk)
        acc = a * acc + _dot(vt_ref[:, pl.ds(off, tk)], pt.astype(BF16))
        return m_new, l, acc

    def pair(u, carry):
        scores(2 * u + 1, 1)
        carry = softmax_pv(2 * u, 0, False, carry)
        scores(2 * u + 2, 0)
        return softmax_pv(2 * u + 1, 1, False, carry)

    def tail_two(carry):
        t = 2 * npairs
        scores(t + 1, 1)
        return softmax_pv(t + 1, 1, True, softmax_pv(t, 0, False, carry))

    def tail_one(carry):
        return softmax_pv(2 * npairs, 0, True, carry)

    n_full = i
    npairs = n_full // 2
    init = (jnp.full((1, nl), MASKED, F32), jnp.zeros((1, nl), F32), jnp.zeros((HD_B, nl), F32))
    scores(0, 0)
    carry = lax.fori_loop(0, npairs, pair, init)
    m, l, acc = lax.cond(n_full % 2 == 1, tail_two, tail_one, carry)
    out = acc / l
    for g in range(G_B):
        o_ref[:, g * HD_B:(g + 1) * HD_B] = out[:, g * bq:(g + 1) * bq].T


def moba_flash(q_aug, k_aug, v_t):
    nq = SEQ // MOBA_QTILE
    per_bc = lambda b, c, i: (b, c, 0, 0)
    return pl.pallas_call(
        _moba_flash_kernel,
        grid=(BATCH, KVH_B, nq),
        in_specs=[pl.BlockSpec((MOBA_QTILE, G_B * AUG), lambda b, c, i: (b * nq + i, c)),
                  pl.BlockSpec((None, None, SEQ, AUG), per_bc),
                  pl.BlockSpec((None, None, HD_B, SEQ), per_bc)],
        out_specs=pl.BlockSpec((MOBA_QTILE, G_B * HD_B), lambda b, c, i: (b * nq + i, c)),
        out_shape=jax.ShapeDtypeStruct((NP, NH_B * HD_B), F32),
        scratch_shapes=[pltpu.VMEM((2, MOBA_QTILE, G_B * MOBA_QTILE), F32)],
        compiler_params=_params("arbitrary", "arbitrary", "arbitrary"),
        name="moba_flash",
    )(q_aug, k_aug, v_t)


def _kv_rows_kernel(k_ref, v_ref, ko_ref, vo_ref, *, tm):
    for c in range(KVH_B):
        ko_ref[pl.ds(c, tm, stride=KVH_B), :] = k_ref[:, c * HD_B:(c + 1) * HD_B]
        vo_ref[pl.ds(c, tm, stride=KVH_B), :] = v_ref[:, c * HD_B:(c + 1) * HD_B]


def moba_kv_rows(p, tm=512):
    w = KVH_B * HD_B
    kcol = NH_B * HD_B // w
    out = jax.ShapeDtypeStruct((NP * KVH_B, HD_B), F32)
    return pl.pallas_call(
        functools.partial(_kv_rows_kernel, tm=tm),
        grid=(NP // tm,),
        in_specs=[pl.BlockSpec((tm, w), lambda i: (i, kcol)), pl.BlockSpec((tm, w), lambda i: (i, kcol + 1))],
        out_specs=[pl.BlockSpec((tm * KVH_B, HD_B), lambda i: (i, 0))] * 2,
        out_shape=[out, out],
        compiler_params=_params("arbitrary"),
        name="moba_kv_rows",
    )(p, p)


def _moba_aug_const():
    c = np.zeros((NH_B, 8, 128), np.float32)
    for h in range(NH_B):
        beta = 2.0 ** (-8.0 * (h + 1) / NH_B) * LOG2E
        c[h, 0, POS_LANE:POS_LANE + 3] = _bf16_parts(beta)
        c[h, 0, POS_LANE + 3:POS_LANE + 6] = _bf16_parts(beta * MOBA_BLOCK)
    return jnp.asarray(c)


def _moba_sample_kernel(pt_ref, qa_ref, qat_ref, kn_ref, vn_ref, bias_ref, ck_ref, cv_ref,
                        o_ref, buf, sem, s_s, p_s, ksum_s, acc_s, inv_s):
    npg = SAMPLE_PAGES_PER_STEP
    w = KVH_B * HD_B
    nrow = KVH_B * DEC_SEQ * G_B

    def load_page(sl, pg):
        return jnp.concatenate([buf[sl, pg, pl.ds(c, PAGE_SIZE, stride=KVH_B), :] for c in range(KVH_B)], axis=1)

    b = pl.program_id(0)
    t = pl.program_id(1)
    nb = pl.num_programs(0)
    g = b * 8 + t
    slot = g % 2

    def page_copy(src_ref, page, sl, pg):
        return pltpu.make_async_copy(src_ref.at[page], buf.at[sl, pg], sem.at[sl])

    def start_fetch(bn, tn, sl):
        qn = tn % 4

        @pl.when(tn < 4)
        def _():
            for pg in range(npg):
                page_copy(ck_ref, pt_ref[bn, qn * npg + pg], sl, pg).start()

        @pl.when(tn >= 4)
        def _():
            for pg in range(npg):
                page_copy(cv_ref, pt_ref[bn, qn * npg + pg], sl, pg).start()

    @pl.when(g == 0)
    def _():
        start_fetch(b, t, slot)

    @pl.when(g + 1 < nb * 8)
    def _():
        tn = (t + 1) % 8
        bn = b + (t + 1) // 8
        start_fetch(bn, tn, 1 - slot)

    for pg in range(npg):
        page_copy(ck_ref, 0, slot, pg).wait()

    qab = (qa_ref[...] * LOG2E).astype(BF16)

    @pl.when(g == 0)
    def _():
        p_s[...] = jnp.zeros_like(p_s)

    @pl.when(t < 4)
    def _():
        for pg in range(npg):
            page = load_page(slot, pg)
            col0 = pl.multiple_of((t * npg + pg) * PAGE_SIZE, PAGE_SIZE)
            s_s[:, pl.ds(col0, PAGE_SIZE)] = _dot(page.astype(BF16), qab).T
            csum = jnp.sum(page, axis=0, keepdims=True)
            if pg % 2 == 0:
                prev = csum
            else:
                ksum_s[pl.ds(t * (npg // 2) + pg // 2, 1), :] = prev + csum

    @pl.when(t == 3)
    def _():
        qat = qat_ref[...]
        kmean = ksum_s[...] * (1.0 / MOBA_BLOCK)
        sel = _topk_mask_lanes(_dot_nt(qat, kmean, HIGHEST), MOBA_TOPK)
        kn = jnp.concatenate([kn_ref[...], jnp.zeros((PAGE_SIZE - 8, w), F32)], axis=0)
        s_s[0:nrow, PAST:PAST + PAGE_SIZE] = _dot_nt((qat * LOG2E).astype(BF16), kn.astype(BF16))
        selm = jnp.where(sel > 0.5, 0.0, NEG_INF)

        def logits(bk):
            if bk == NB_S:
                return s_s[0:nrow, PAST:PAST + PAGE_SIZE] + bias_ref[:, PAST:PAST + PAGE_SIZE]
            lo = bk * MOBA_BLOCK
            return s_s[0:nrow, lo:lo + MOBA_BLOCK] + bias_ref[:, lo:lo + MOBA_BLOCK] + selm[:, bk:bk + 1]

        macc = logits(0)
        for bk in range(1, NB_S):
            macc = jnp.maximum(macc, logits(bk))
        mx = jnp.maximum(jnp.max(macc, axis=1, keepdims=True), jnp.max(logits(NB_S), axis=1, keepdims=True))
        dacc = jnp.zeros((nrow, MOBA_BLOCK), F32)
        for bk in range(NB_S):
            pr = jnp.exp2(logits(bk) - mx)
            p_s[0:nrow, bk * MOBA_BLOCK:(bk + 1) * MOBA_BLOCK] = pr
            dacc = dacc + pr
        pr = jnp.exp2(logits(NB_S) - mx)
        p_s[0:nrow, PAST:PAST + PAGE_SIZE] = pr
        den = jnp.sum(dacc, axis=1, keepdims=True) + jnp.sum(pr, axis=1, keepdims=True)
        inv_s[...] = jnp.zeros_like(inv_s)
        inv_s[0:nrow, :] = jnp.broadcast_to(1.0 / den, (nrow, 128))
        inv_s[...] = inv_s[...].T

    @pl.when(t == 4)
    def _():
        vn = jnp.concatenate([vn_ref[...], jnp.zeros((PAGE_SIZE - 8, w), F32)], axis=0)
        acc_s[...] = _dot_tn(vn.astype(BF16), p_s[:, PAST:PAST + PAGE_SIZE].T.astype(BF16))

    @pl.when(t >= 4)
    def _():
        acc = acc_s[...]
        for pg in range(npg):
            col0 = pl.multiple_of(((t - 4) * npg + pg) * PAGE_SIZE, PAGE_SIZE)
            pt = p_s[:, pl.ds(col0, PAGE_SIZE)].T
            acc = acc + _dot_tn(load_page(slot, pg).astype(BF16), pt.astype(BF16))
        acc_s[...] = acc

    @pl.when(t == 7)
    def _():
        o_ref[...] = acc_s[:, 0:nrow] * inv_s[0:1, 0:nrow]


def moba_sample(page_table, q_aug, q_aug_t, k_new, v_new, bias, cache_k, cache_v):
    w = KVH_B * HD_B
    nrow = KVH_B * DEC_SEQ * G_B
    nkeys = PAST + PAGE_SIZE
    per_b3 = lambda b, t, pt: (b, 0, 0)
    fixed = lambda b, t, pt: (0, 0)
    grid_spec = pltpu.PrefetchScalarGridSpec(
        num_scalar_prefetch=1,
        grid=(DEC_BATCH, 8),
        in_specs=[pl.BlockSpec((None, w, 128), per_b3),
                  pl.BlockSpec((None, nrow, w), per_b3),
                  pl.BlockSpec((None, 8, w), per_b3),
                  pl.BlockSpec((None, 8, w), per_b3),
                  pl.BlockSpec((nrow, nkeys), fixed),
                  pl.BlockSpec(memory_space=pl.ANY),
                  pl.BlockSpec(memory_space=pl.ANY)],
        out_specs=pl.BlockSpec((None, w, nrow), per_b3),
        scratch_shapes=[pltpu.VMEM((2, SAMPLE_PAGES_PER_STEP, PAGE_SIZE * KVH_B, HD_B), F32),
                        pltpu.SemaphoreType.DMA((2,)),
                        pltpu.VMEM((128, nkeys), F32),
                        pltpu.VMEM((128, nkeys), F32),
                        pltpu.VMEM((NB_S, w), F32),
                        pltpu.VMEM((w, 128), F32),
                        pltpu.VMEM((128, 128), F32)],
    )
    return pl.pallas_call(
        _moba_sample_kernel,
        grid_spec=grid_spec,
        out_shape=jax.ShapeDtypeStruct((DEC_BATCH, w, nrow), F32),
        compiler_params=_params("arbitrary", "arbitrary"),
        name="moba_sample",
    )(page_table, q_aug, q_aug_t, k_new, v_new, bias, cache_k, cache_v)


def _swa_slope(h):
    return 2.0 ** (-8.0 * (h + 1) / NH_C)


def _by_head(gidx, values):
    out = values[-1]
    for g in range(len(values) - 2, -1, -1):
        out = jnp.where(gidx == g, values[g], out)
    return out


def _swa_kv_aug(k_all, v_all, c):
    nk = k_all.shape[0]
    lane = lax.broadcasted_iota(jnp.int32, (nk, HD_C), 1)
    kidx = lax.broadcasted_iota(jnp.int32, (nk, HD_C), 0).astype(F32)
    kpat = jnp.where(lane < 3, kidx, 0.0)
    ka = jnp.concatenate([k_all[:, c * HD_C:(c + 1) * HD_C], kpat], axis=1).astype(BF16)
    return ka, v_all[:, c * HD_C:(c + 1) * HD_C].astype(BF16)


def _swa_softmax_pv(qa, ka, vc, maskt, sink2):
    s2 = _dot_nt(qa, ka) + maskt
    mx = jnp.maximum(jnp.max(s2, axis=1, keepdims=True), sink2)
    pr = jnp.exp2(s2 - mx)
    den = jnp.sum(pr, axis=1, keepdims=True) + jnp.exp2(sink2 - mx)
    return _dot(pr.astype(BF16), vc) / den


def _store_heads(o_ref, outs, lead=()):
    for j in range(0, NH_C, 2):
        o_ref[lead + (slice(None), slice(j * HD_C, (j + 2) * HD_C))] = jnp.concatenate(outs[j:j + 2], axis=1)


def _swa_prompt_kernel(q_ref, kvp_ref, kvc_ref, sink_ref, o_ref):
    i = pl.program_id(1)
    w = WINDOW
    kw = KVH_C * HD_C
    k_all = jnp.concatenate([kvp_ref[:, 0:kw], kvc_ref[:, 0:kw]], axis=0)
    v_all = jnp.concatenate([kvp_ref[:, kw:2 * kw], kvc_ref[:, kw:2 * kw]], axis=0)
    kk = lax.broadcasted_iota(jnp.int32, (2 * w, w), 0)
    qq = lax.broadcasted_iota(jnp.int32, (2 * w, w), 1)
    di = qq - kk + w
    ok = (di >= 0) & (di <= w) & ((kk >= w) | (i > 0))
    mask1 = jnp.where(ok, 0.0, NEG_INF)
    maskt = jnp.concatenate([mask1] * G_C, axis=1)
    qdist = (lax.broadcasted_iota(jnp.int32, (1, w), 1) + w).astype(F32)
    qlane = lax.broadcasted_iota(jnp.int32, (w, HD_C), 1)
    sink = sink_ref[...]
    for c in range(KVH_C):
        ka, vc = _swa_kv_aug(k_all, v_all, c)
        qas, sinks = [], []
        for gi in range(G_C):
            h = c * G_C + gi
            b1, b2, b3 = _bf16_parts(_swa_slope(h) * LOG2E)
            qpat = jnp.where(qlane == 0, b1, jnp.where(qlane == 1, b2, jnp.where(qlane == 2, b3, 0.0)))
            qa = jnp.concatenate([q_ref[:, h * HD_C:(h + 1) * HD_C] * (HD_C ** -0.5 * LOG2E), qpat], axis=1)
            qas.append(qa.astype(BF16))
            sinks.append((sink[:, h:h + 1] + _swa_slope(h) * qdist) * LOG2E)
        sink2 = jnp.concatenate(sinks, axis=1)
        st = _dot_nt(ka, jnp.concatenate(qas, axis=0)) + maskt
        mx = jnp.maximum(jnp.max(st, axis=0, keepdims=True), sink2)
        pt = jnp.exp2(st - mx)
        den = jnp.sum(pt, axis=0, keepdims=True) + jnp.exp2(sink2 - mx)
        ot = _dot_tn(vc, pt.astype(BF16)) / den
        for gi in range(0, G_C, 2):
            h = c * G_C + gi
            two = jnp.concatenate([ot[:, gi * w:(gi + 1) * w], ot[:, (gi + 1) * w:(gi + 2) * w]], axis=0)
            o_ref[:, h * HD_C:(h + 2) * HD_C] = two.T


def swa_prompt(p, sink_row):
    nblk = SEQ // WINDOW
    kvb = NH_C * HD_C // (2 * KVH_C * HD_C)
    return pl.pallas_call(
        _swa_prompt_kernel,
        grid=(BATCH, nblk),
        in_specs=[pl.BlockSpec((WINDOW, NH_C * HD_C), lambda b, i: (b * nblk + i, 0)),
                  pl.BlockSpec((WINDOW, 2 * KVH_C * HD_C), lambda b, i: (b * nblk + jnp.maximum(i - 1, 0), kvb)),
                  pl.BlockSpec((WINDOW, 2 * KVH_C * HD_C), lambda b, i: (b * nblk + i, kvb)),
                  pl.BlockSpec((1, 128), lambda b, i: (0, 0))],
        out_specs=pl.BlockSpec((WINDOW, NH_C * HD_C), lambda b, i: (b * nblk + i, 0)),
        out_shape=jax.ShapeDtypeStruct((NP, NH_C * HD_C), F32),
        compiler_params=_params("arbitrary", "arbitrary"),
        name="swa_prompt",
    )(p, p, p, sink_row)


def _swa_sample_kernel(q_ref, kn_ref, vn_ref, ck_ref, cv_ref, sink_ref, o_ref):
    w = WINDOW
    nk = w + 8
    nr = G_C * 8
    r = lax.broadcasted_iota(jnp.int32, (nr, nk), 0)
    cc = lax.broadcasted_iota(jnp.int32, (nr, nk), 1)
    di = w + r % 8 - cc
    ok = (di >= 0) & (di <= w) & (cc < w + DEC_SEQ)
    maskt = jnp.where(ok, 0.0, NEG_INF)
    rcol = lax.broadcasted_iota(jnp.int32, (nr, 1), 0)
    gcol = rcol // 8
    qdist = (w + rcol % 8).astype(F32)
    qlane = lax.broadcasted_iota(jnp.int32, (nr, HD_C), 1)
    sink = sink_ref[...]
    qpats, sink2s = [], []
    for c in range(KVH_C):
        heads = [c * G_C + gi for gi in range(G_C)]
        parts = [_bf16_parts(_swa_slope(h) * LOG2E) for h in heads]
        b = [_by_head(gcol, [pp[j] for pp in parts]) for j in range(3)]
        qpats.append(jnp.where(qlane == 0, b[0], jnp.where(qlane == 1, b[1], jnp.where(qlane == 2, b[2], 0.0))))
        slope = _by_head(gcol, [_swa_slope(h) for h in heads])
        sk = _by_head(gcol, [sink[:, h:h + 1] for h in heads])
        sink2s.append((sk + slope * qdist) * LOG2E)
    for sq in range(SWA_SEQ_PER_STEP):
        q = q_ref[sq] * (HD_C ** -0.5 * LOG2E)
        k_all = jnp.concatenate([ck_ref[sq], kn_ref[sq]], axis=0)
        v_all = jnp.concatenate([cv_ref[sq], vn_ref[sq]], axis=0)
        outs = []
        for c in range(KVH_C):
            ka, vc = _swa_kv_aug(k_all, v_all, c)
            qc = jnp.concatenate([q[:, (c * G_C + gi) * HD_C:(c * G_C + gi + 1) * HD_C] for gi in range(G_C)], axis=0)
            qa = jnp.concatenate([qc, qpats[c]], axis=1).astype(BF16)
            oc = _swa_softmax_pv(qa, ka, vc, maskt, sink2s[c])
            outs.extend(oc[gi * 8:(gi + 1) * 8, :] for gi in range(G_C))
        _store_heads(o_ref, outs, lead=(sq,))


def swa_sample(q, k_new, v_new, cache_k, cache_v, sink_row):
    n = SWA_SEQ_PER_STEP
    kw = KVH_C * HD_C
    blk = lambda i: (i, 0, 0)
    return pl.pallas_call(
        _swa_sample_kernel,
        grid=(DEC_BATCH // n,),
        in_specs=[pl.BlockSpec((n, 8, NH_C * HD_C), blk), pl.BlockSpec((n, 8, kw), blk), pl.BlockSpec((n, 8, kw), blk),
                  pl.BlockSpec((n, WINDOW, kw), blk), pl.BlockSpec((n, WINDOW, kw), blk),
                  pl.BlockSpec((1, 128), lambda i: (0, 0))],
        out_specs=pl.BlockSpec((n, 8, NH_C * HD_C), blk),
        out_shape=jax.ShapeDtypeStruct((DEC_BATCH, 8, NH_C * HD_C), F32),
        compiler_params=_params("arbitrary"),
        name="swa_sample",
    )(q, k_new, v_new, cache_k, cache_v, sink_row)


def _router_kernel(x_ref, wt_ref, bt_ref, meta_ref, wts_ref, seg_ref, tri_s, *, tm):
    @pl.when(pl.program_id(0) == 0)
    def _():
        rr = lax.broadcasted_iota(jnp.int32, (128, tm), 0)
        ccn = lax.broadcasted_iota(jnp.int32, (128, tm), 1)
        for ch in range(tm // 128):
            tri_s[ch * 128:(ch + 1) * 128, :] = (rr + ch * 128 < ccn).astype(BF16)

    lt = _dot_nt(wt_ref[...], x_ref[...], HIGHEST) + bt_ref[...]
    row = lax.broadcasted_iota(jnp.int32, lt.shape, 0)
    big = ROUTER_ROWS

    lg = jnp.where(row < N_GROUPS, lt, NEG_INF)
    mg = jnp.max(lg, axis=0, keepdims=True)
    eg = jnp.exp(lg - mg)
    pg = eg / jnp.sum(eg, axis=0, keepdims=True)
    pg1 = jnp.max(pg, axis=0, keepdims=True)
    g1 = jnp.min(jnp.where(pg == pg1, row, big), axis=0, keepdims=True)

    e = row - N_GROUPS
    ingroup = (e >= g1 * N_EXP) & (e < (g1 + 1) * N_EXP)
    le = jnp.where(ingroup, lt, NEG_INF)
    me = jnp.max(le, axis=0, keepdims=True)
    ee = jnp.exp(le - me)
    pe = ee / jnp.sum(ee, axis=0, keepdims=True)
    pe = jnp.where(ingroup, pe, NEG_INF)
    p1 = jnp.max(pe, axis=0, keepdims=True)
    i1 = jnp.min(jnp.where(pe == p1, row, big), axis=0, keepdims=True)
    pe2 = jnp.where(row == i1, NEG_INF, pe)
    p2 = jnp.max(pe2, axis=0, keepdims=True)
    i2 = jnp.min(jnp.where(pe2 == p2, row, big), axis=0, keepdims=True)
    tot = p1 + p2
    w1 = p1 / tot * pg1
    w2 = p2 / tot * pg1

    hit1 = row == i1
    hit2 = row == i2
    oh = jnp.where(hit1 | hit2, 1.0, 0.0)
    ohb = oh.astype(BF16)
    before = _dot(ohb, tri_s[...])
    cnt_col = jnp.sum(oh, axis=1, keepdims=True)
    pad_col = jnp.floor((cnt_col + 7.0) * 0.125) * 8.0
    r128 = lax.broadcasted_iota(jnp.int32, (128, 128), 0)
    c128 = lax.broadcasted_iota(jnp.int32, (128, 128), 1)
    off_col = jnp.dot((c128 < r128).astype(F32), pad_col + jnp.zeros((128, 128), F32),
                      precision=HIGHEST, preferred_element_type=F32)[:, 0:1]
    place = before + off_col
    pos1 = jnp.sum(jnp.where(hit1, place, 0.0), axis=0, keepdims=True)
    pos2 = jnp.sum(jnp.where(hit2, place, 0.0), axis=0, keepdims=True)
    cnt_row = _dot_nt(jnp.ones((8, tm), BF16), ohb)
    pad_row = jnp.floor((cnt_row + 7.0) * 0.125) * 8.0
    off_row = jnp.dot(pad_row, (r128 < c128).astype(F32), precision=HIGHEST, preferred_element_type=F32)

    r8 = lax.broadcasted_iota(jnp.int32, (8, tm), 0)
    meta_ref[...] = jnp.where(r8 == 0, pos1, jnp.where(r8 == 1, pos2, 0.0)).astype(I32)
    wts_ref[...] = jnp.where(r8 == 0, w1, jnp.where(r8 == 1, w2, 0.0))
    s8 = lax.broadcasted_iota(jnp.int32, (8, 128), 0)
    seg_ref[...] = jnp.where(s8 == 0, off_row, jnp.where(s8 == 1, cnt_row, 0.0)).astype(I32)


def moe_router(x, wt_router, bt_router, tm=MOE_TM):
    m = x.shape[0]
    nt = m // tm
    blk = lambda i: (i, 0, 0)
    return pl.pallas_call(
        functools.partial(_router_kernel, tm=tm),
        grid=(nt,),
        in_specs=[pl.BlockSpec((tm, D_MODEL), lambda i: (i, 0)),
                  pl.BlockSpec((ROUTER_ROWS, D_MODEL), lambda i: (0, 0)),
                  pl.BlockSpec((ROUTER_ROWS, 1), lambda i: (0, 0))],
        out_specs=[pl.BlockSpec((None, 8, tm), blk), pl.BlockSpec((None, 8, tm), blk),
                   pl.BlockSpec((None, 8, 128), blk)],
        out_shape=[jax.ShapeDtypeStruct((nt, 8, tm), I32), jax.ShapeDtypeStruct((nt, 8, tm), F32),
                   jax.ShapeDtypeStruct((nt, 8, 128), I32)],
        scratch_shapes=[pltpu.VMEM((tm, tm), BF16)],
        compiler_params=_params("arbitrary"),
        name="moe_router",
    )(x, wt_router, bt_router)


def _moe_kernel(x_ref, meta_hbm, wts_hbm, seg_hbm, wg_hbm, wu_hbm, wd_hbm, g_ref, b_ref, o_ref,
                xs, wbg, wbu, wbd, wsem, meta_s, wts_s, seg_s, msem, *, layer, tm):
    i = pl.program_id(0)
    nt = pl.num_programs(0)
    rc = MOE_RC
    ahead = MOE_WSLOTS - 1
    base = i * N_EXPERTS
    ms = i % 2

    def weight_copies(step):
        e = step % N_EXPERTS
        sl = step % MOE_WSLOTS
        return (pltpu.make_async_copy(wg_hbm.at[layer, e], wbg.at[sl], wsem.at[sl, 0]),
                pltpu.make_async_copy(wu_hbm.at[layer, e], wbu.at[sl], wsem.at[sl, 1]),
                pltpu.make_async_copy(wd_hbm.at[layer, e], wbd.at[sl], wsem.at[sl, 2]))

    def meta_copies(tile, sl):
        return (pltpu.make_async_copy(meta_hbm.at[tile], meta_s.at[sl], msem.at[sl, 0]),
                pltpu.make_async_copy(wts_hbm.at[tile], wts_s.at[sl], msem.at[sl, 1]),
                pltpu.make_async_copy(seg_hbm.at[tile], seg_s.at[sl], msem.at[sl, 2]))

    @pl.when(i == 0)
    def _():
        for cp in meta_copies(0, 0):
            cp.start()
        for step in range(ahead):
            for cp in weight_copies(step):
                cp.start()
        xs[...] = jnp.zeros_like(xs)

    for cp in meta_copies(i, ms):
        cp.wait()

    @pl.when(i + 1 < nt)
    def _():
        for cp in meta_copies(i + 1, 1 - ms):
            cp.start()

    def gather(t, carry):
        rowv = x_ref[pl.ds(t, 1), :]
        xs[pl.ds(meta_s[ms, 0, t], 1), :] = rowv
        xs[pl.ds(meta_s[ms, 1, t], 1), :] = rowv
        return carry

    lax.fori_loop(0, tm, gather, 0, unroll=8)

    def expert(e, carry):
        step = base + e
        sl = step % MOE_WSLOTS
        for cp in weight_copies(step):
            cp.wait()

        @pl.when(step + ahead < nt * N_EXPERTS)
        def _():
            for cp in weight_copies(step + ahead):
                cp.start()

        off = seg_s[ms, 0, N_GROUPS + e]
        cnt = seg_s[ms, 1, N_GROUPS + e]
        wgb = wbg[sl]
        wub = wbu[sl]
        wdb = wbd[sl]
        ridx = lax.broadcasted_iota(jnp.int32, (rc, 1), 0)

        def chunk(k, c2):
            row0 = pl.multiple_of(off + k * rc, 8)
            lhs = xs[pl.ds(row0, rc), :]
            lb = lhs.astype(BF16)
            hg = _dot(lb, wgb)
            hu = _dot(lb, wub)
            hh = hg / (1.0 + jnp.exp(-hg)) * hu
            out = _dot(hh.astype(BF16), wdb)
            xs[pl.ds(row0, rc), :] = jnp.where(ridx < cnt - k * rc, out, lhs)
            return c2

        lax.fori_loop(0, (cnt + rc - 1) // rc, chunk, 0)
        return carry

    lax.fori_loop(0, N_EXPERTS, expert, 0)

    def combine(t, carry):
        y = (wts_s[ms, 0, t] * xs[pl.ds(meta_s[ms, 0, t], 1), :]
             + wts_s[ms, 1, t] * xs[pl.ds(meta_s[ms, 1, t], 1), :])
        o_ref[pl.ds(t, 1), :] = y
        return carry

    lax.fori_loop(0, tm, combine, 0, unroll=8)
    o_ref[...] = _layer_norm(ALPHA * x_ref[...] + o_ref[...], g_ref[...], b_ref[...])


def moe_ffn_ln(x, meta, wts, seg, wg, wu, wd, g, b, layer, tm=MOE_TM):
    m = x.shape[0]
    row = lambda i: (i, 0)
    fixed = lambda i: (0, 0)
    hbm = pl.BlockSpec(memory_space=pl.ANY)
    return pl.pallas_call(
        functools.partial(_moe_kernel, layer=layer, tm=tm),
        grid=(m // tm,),
        in_specs=[pl.BlockSpec((tm, D_MODEL), row), hbm, hbm, hbm, hbm, hbm, hbm,
                  pl.BlockSpec((1, D_MODEL), fixed), pl.BlockSpec((1, D_MODEL), fixed)],
        out_specs=pl.BlockSpec((tm, D_MODEL), row),
        out_shape=jax.ShapeDtypeStruct((m, D_MODEL), F32),
        scratch_shapes=[pltpu.VMEM((MOE_ROWS, D_MODEL), F32),
                        pltpu.VMEM((MOE_WSLOTS, D_MODEL, D_FF), BF16), pltpu.VMEM((MOE_WSLOTS, D_MODEL, D_FF), BF16),
                        pltpu.VMEM((MOE_WSLOTS, D_FF, D_MODEL), BF16), pltpu.SemaphoreType.DMA((MOE_WSLOTS, 3)),
                        pltpu.SMEM((2, 8, tm), I32), pltpu.SMEM((2, 8, tm), F32), pltpu.SMEM((2, 8, 128), I32),
                        pltpu.SemaphoreType.DMA((2, 3))],
        compiler_params=_params("arbitrary"),
        name="moe_ffn_ln",
    )(x, meta, wts, seg, wg, wu, wd, g.reshape(1, D_MODEL), b.reshape(1, D_MODEL))


def _pad_rows(a, rows):
    return jnp.pad(a, ((0, 0), (0, rows - a.shape[1]), (0, 0)))


def _mlstm_layer(x, w_in, b_gate, norm_g, w_out, c0_all, n0_all, m0_all, slot):
    w = jnp.pad(w_in, ((0, 0), (0, M_IN_PAD - M_IN))).astype(BF16)
    p = matmul(x, w)
    bg = jnp.pad(b_gate, (0, 128 - 2 * NH_M)).reshape(1, 128)
    ng = norm_g.reshape(1, M_V)
    hp, cp, np_, mp = mlstm_prompt(p, bg, ng)
    ps = _pad_rows(p[NP:].reshape(DEC_BATCH, DEC_SEQ, M_IN_PAD), 8).reshape(DEC_BATCH * 8, M_IN_PAD)
    hs, cs, ns, ms = mlstm_sample(ps, c0_all, n0_all, m0_all, slot, bg, ng)
    hs = hs.reshape(DEC_BATCH, 8, M_V)[:, :DEC_SEQ].reshape(NS, M_V)
    states = (cp, np_, mp, cs, ns, ms.reshape(DEC_BATCH, NH_M))
    return (hp, hs), w_out.astype(BF16), states


def _moba_sample_part(ps, cache_k, cache_v, page_rows):
    hq = NH_B * HD_B
    hk = KVH_B * HD_B
    slopes_h = jnp.exp2(-8.0 * jnp.arange(1, NH_B + 1, dtype=F32) / NH_B)
    q = (ps[:, :hq] * (HD_B ** -0.5)).reshape(DEC_BATCH, DEC_SEQ, KVH_B, G_B, HD_B)
    eye = jnp.eye(KVH_B, dtype=F32)
    nrow = KVH_B * DEC_SEQ * G_B
    q_aug_t = jnp.einsum("bscgd,ce->besgcd", q, eye).reshape(DEC_BATCH, nrow, hk)
    q_aug = jnp.pad(jnp.swapaxes(q_aug_t, 1, 2), ((0, 0), (0, 0), (0, 128 - nrow)))
    k_new = ps[:, hq:hq + hk].reshape(DEC_BATCH, DEC_SEQ, hk)
    v_new = ps[:, hq + hk:].reshape(DEC_BATCH, DEC_SEQ, hk)
    row = jnp.arange(nrow)
    row_c, row_s, row_g = row // (DEC_SEQ * G_B), (row // G_B) % DEC_SEQ, row % G_B
    slope_row = (slopes_h[row_c * G_B + row_g] * LOG2E)[:, None]
    qs_row = row_s.astype(F32)[:, None]
    past_bias = -slope_row * (float(PAST) + qs_row - jnp.arange(PAST, dtype=F32)[None, :])
    tok = jnp.arange(PAGE_SIZE, dtype=F32)[None, :]
    own_bias = jnp.where((tok <= qs_row) & (tok < float(DEC_SEQ)), -slope_row * (qs_row - tok), NEG_INF)
    bias = jnp.concatenate([past_bias, own_bias], axis=1)
    ot = moba_sample(page_rows, q_aug, q_aug_t, _pad_rows(k_new, 8), _pad_rows(v_new, 8), bias, cache_k, cache_v)
    ot = ot.reshape(DEC_BATCH, KVH_B, HD_B, KVH_B, DEC_SEQ, G_B)
    os_ = jnp.einsum("bcdesg,ce->bscgd", ot, eye).reshape(NS, hq)
    return os_, k_new, v_new


def _moba_layer(x, w_in, w_out, cache_k, cache_v, page_rows):
    p = matmul(x, w_in.astype(BF16))
    hq = NH_B * HD_B
    hk = KVH_B * HD_B
    ac = _moba_aug_const()
    q_aug, k_aug, v_t = moba_select(p, ac)
    op = moba_flash(q_aug, k_aug, v_t)
    os_, k_new, v_new = _moba_sample_part(p[NP:], cache_k, cache_v, page_rows)
    k_rows, v_rows = moba_kv_rows(p)
    kv = (k_rows.reshape(BATCH, SEQ, KVH_B, HD_B), v_rows.reshape(BATCH, SEQ, KVH_B, HD_B),
          k_new.reshape(DEC_BATCH, DEC_SEQ, KVH_B, HD_B), v_new.reshape(DEC_BATCH, DEC_SEQ, KVH_B, HD_B))
    return (op, os_), w_out.astype(BF16), kv


def _swa_layer(x, w_in, sinks, w_out, cache_k, cache_v):
    p = matmul(x, w_in.astype(BF16))
    hq = NH_C * HD_C
    kw = KVH_C * HD_C
    sink_row = jnp.pad(sinks, (0, 128 - NH_C)).reshape(1, 128)
    op = swa_prompt(p, sink_row)
    ps = p[NP:]
    q = _pad_rows(ps[:, :hq].reshape(DEC_BATCH, DEC_SEQ, hq), 8)
    k_new = ps[:, hq:hq + kw].reshape(DEC_BATCH, DEC_SEQ, kw)
    v_new = ps[:, hq + kw:].reshape(DEC_BATCH, DEC_SEQ, kw)
    ck = cache_k.reshape(DEC_BATCH, WINDOW, kw)
    cv = cache_v.reshape(DEC_BATCH, WINDOW, kw)
    os_ = swa_sample(q, _pad_rows(k_new, 8), _pad_rows(v_new, 8), ck, cv, sink_row)[:, :DEC_SEQ].reshape(NS, hq)
    pp = jnp.stack([p[(b + 1) * SEQ - WINDOW:(b + 1) * SEQ] for b in range(BATCH)])
    kv = (pp[..., hq:hq + kw].reshape(BATCH, WINDOW, KVH_C, HD_C), pp[..., hq + kw:].reshape(BATCH, WINDOW, KVH_C, HD_C),
          jnp.concatenate([ck, k_new], axis=1)[:, DEC_SEQ:].reshape(DEC_BATCH, WINDOW, KVH_C, HD_C),
          jnp.concatenate([cv, v_new], axis=1)[:, DEC_SEQ:].reshape(DEC_BATCH, WINDOW, KVH_C, HD_C))
    return (op, os_), w_out.astype(BF16), kv


def _moe_layer(x, w_group, b_group, w_router, b_router, w_gate, w_up, w_down, g, b, layer):
    pad = ROUTER_ROWS - N_GROUPS - N_EXPERTS
    wt = jnp.concatenate([w_group.T, w_router.T, jnp.zeros((pad, D_MODEL), F32)], axis=0)
    bt = jnp.concatenate([b_group, b_router, jnp.zeros((pad,), F32)]).reshape(ROUTER_ROWS, 1)
    meta, wts, seg = moe_router(x, wt, bt)
    return moe_ffn_ln(x, meta, wts, seg, w_gate, w_up, w_down, g, b, layer)


def kernel(x_prompt, x_sample, state_mlstm_C, state_mlstm_n, state_mlstm_m, cache_moba_k, cache_moba_v, cache_swa_k, cache_swa_v, page_table, mlstm_w_in, mlstm_b_gate, mlstm_norm_g, mlstm_w_out, moba_w_in, moba_w_out, swa_w_in, swa_sinks, swa_w_out, ln_mix_g, ln_mix_b, ln_ffn_g, ln_ffn_b, moe_w_group, moe_b_group, moe_w_router, moe_b_router, moe_w_gate, moe_w_up, moe_w_down):
    x = jnp.concatenate([x_prompt.reshape(NP, D_MODEL), x_sample.reshape(NS, D_MODEL)], axis=0)
    n_pool = cache_moba_k.shape[1]
    moba_k = cache_moba_k.reshape(-1, PAGE_SIZE * KVH_B, HD_B)
    moba_v = cache_moba_v.reshape(-1, PAGE_SIZE * KVH_B, HD_B)
    wg_all, wu_all, wd_all = moe_w_gate.astype(BF16), moe_w_up.astype(BF16), moe_w_down.astype(BF16)
    m_states, b_kv, c_kv = [], [], []
    for layer in range(DEPTH):
        kind, slot = layer % 3, layer // 3
        if kind == 0:
            a, w_out, st = _mlstm_layer(x, mlstm_w_in[slot], mlstm_b_gate[slot], mlstm_norm_g[slot], mlstm_w_out[slot],
                                        state_mlstm_C, state_mlstm_n, state_mlstm_m, slot)
            m_states.append(st)
        elif kind == 1:
            a, w_out, kv = _moba_layer(x, moba_w_in[slot], moba_w_out[slot], moba_k, moba_v,
                                       page_table + slot * n_pool)
            b_kv.append(kv)
        else:
            a, w_out, kv = _swa_layer(x, swa_w_in[slot], swa_sinks[slot], swa_w_out[slot], cache_swa_k[slot],
                                      cache_swa_v[slot])
            c_kv.append(kv)
        x = outproj_ln(a[0], a[1], w_out, x, ln_mix_g[layer], ln_mix_b[layer])
        x = _moe_layer(x, moe_w_group[layer], moe_b_group[layer], moe_w_router[layer], moe_b_router[layer],
                       wg_all, wu_all, wd_all, ln_ffn_g[layer], ln_ffn_b[layer], layer)
    stack = lambda items, j: jnp.stack([it[j] for it in items])
    return (x[:NP].reshape(BATCH, SEQ, D_MODEL), x[NP:].reshape(DEC_BATCH, DEC_SEQ, D_MODEL),
            stack(m_states, 0), stack(m_states, 1), stack(m_states, 2),
            stack(m_states, 3), stack(m_states, 4), stack(m_states, 5),
            stack(b_kv, 0), stack(b_kv, 1), stack(b_kv, 2), stack(b_kv, 3),
            stack(c_kv, 0), stack(c_kv, 1), stack(c_kv, 2), stack(c_kv, 3))
```

```python
import functools

import jax
import jax.numpy as jnp
import numpy as np
from jax import lax
from jax.experimental import pallas as pl
from jax.experimental.pallas import tpu as pltpu

F32 = jnp.float32
BF16 = jnp.bfloat16
I32 = jnp.int32
HIGHEST = lax.Precision.HIGHEST

D_MODEL = 1024
BATCH = 2
SEQ = 8192
DEPTH = 4
DEC_BATCH = 128
DEC_SEQ = 4
PAGE_SIZE = 128
N_PAGES = 64
PAST = N_PAGES * PAGE_SIZE
NP = BATCH * SEQ
NS = DEC_BATCH * DEC_SEQ
NT = NP + NS

NH_M = 4
DK_M = 128
DV_M = 256
M_CHUNK = 128
M_QK = NH_M * DK_M
M_V = NH_M * DV_M
M_IN = 2 * M_QK + 2 * M_V + 2 * NH_M
M_IN_PAD = 3200
M_GATE_COL = 2 * M_QK + 2 * M_V
MLSTM_SEQ_PER_STEP = 4

NH_B = 8
KVH_B = 4
G_B = 2
HD_B = 128
MOBA_BLOCK = 256
MOBA_TOPK = 3
NB_B = SEQ // MOBA_BLOCK
NB_S = PAST // MOBA_BLOCK
B_IN = (NH_B + 2 * KVH_B) * HD_B
SAMPLE_PAGES_PER_STEP = 16
SAMPLE_SLOTS = 3
MOBA_QTILE = 2 * MOBA_BLOCK
AUG = 2 * HD_B
POS_LANE = NB_B

NH_C = 16
KVH_C = 2
G_C = 8
HD_C = 64
WINDOW = 128
C_IN = (NH_C + 2 * KVH_C) * HD_C
SWA_SEQ_PER_STEP = 8

N_GROUPS = 4
N_EXP = 8
N_EXPERTS = 32
D_FF = 256
ROUTER_ROWS = 128
MOE_TM = 1536
MOE_RC = 128
MOE_WSLOTS = 4
MOE_ROWS = 2 * MOE_TM + N_EXPERTS * 8 + MOE_RC

ALPHA = (2.0 * DEPTH) ** 0.25
EPS = 1e-5
LOG2E = 1.4426950408889634
NEG_INF = float("-inf")
MASKED = -1e30
VMEM_LIMIT = 56 * 1024 * 1024


def _params(*sem):
    return pltpu.CompilerParams(dimension_semantics=sem, vmem_limit_bytes=VMEM_LIMIT)


def _dot(a, b):
    return jnp.dot(a, b, preferred_element_type=F32)


def _dot_nt(a, b, precision=None):
    return lax.dot_general(a, b, (((1,), (1,)), ((), ())), precision=precision, preferred_element_type=F32)


def _dot_tn(a, b):
    return lax.dot_general(a, b, (((0,), (0,)), ((), ())), preferred_element_type=F32)


def _layer_norm(z, g, b):
    mu = jnp.mean(z, axis=-1, keepdims=True)
    zc = z - mu
    var = jnp.mean(zc * zc, axis=-1, keepdims=True)
    return zc * lax.rsqrt(var + EPS) * g + b


def _bf16_parts(x, n=3):
    parts = []
    for _ in range(n):
        bits = int(np.float32(x).view(np.uint32))
        rounded = ((bits + 0x7FFF + ((bits >> 16) & 1)) >> 16) << 16
        p = float(np.uint32(rounded & 0xFFFFFFFF).view(np.float32))
        parts.append(p)
        x = x - p
    return parts


def _mm_kernel(x_ref, w_ref, o_ref):
    o_ref[...] = _dot(x_ref[...].astype(BF16), w_ref[...])


def matmul(x, w_bf16, tm=512):
    m, k = x.shape
    n = w_bf16.shape[1]
    return pl.pallas_call(
        _mm_kernel,
        grid=(m // tm,),
        in_specs=[pl.BlockSpec((tm, k), lambda i: (i, 0)), pl.BlockSpec((k, n), lambda i: (0, 0))],
        out_specs=pl.BlockSpec((tm, n), lambda i: (i, 0)),
        out_shape=jax.ShapeDtypeStruct((m, n), F32),
        compiler_params=_params("arbitrary"),
        name="in_proj",
    )(x, w_bf16)


def _outproj_ln_kernel(ap_ref, as_ref, w_ref, r_ref, g_ref, b_ref, o_ref, *, n_p):
    a = jnp.where(pl.program_id(0) < n_p, ap_ref[...], as_ref[...])
    y = _dot(a.astype(BF16), w_ref[...])
    o_ref[...] = _layer_norm(ALPHA * r_ref[...] + y, g_ref[...], b_ref[...])


def outproj_ln(a_p, a_s, w_bf16, resid, g, b, tm=512):
    k = a_p.shape[1]
    n = w_bf16.shape[1]
    n_p = a_p.shape[0] // tm
    m = a_p.shape[0] + a_s.shape[0]
    row = lambda i: (i, 0)
    fixed = lambda i: (0, 0)
    return pl.pallas_call(
        functools.partial(_outproj_ln_kernel, n_p=n_p),
        grid=(m // tm,),
        in_specs=[pl.BlockSpec((tm, k), lambda i: (jnp.minimum(i, n_p - 1), 0)),
                  pl.BlockSpec((tm, k), lambda i: (jnp.maximum(i - n_p, 0), 0)),
                  pl.BlockSpec((k, n), fixed), pl.BlockSpec((tm, n), row),
                  pl.BlockSpec((1, n), fixed), pl.BlockSpec((1, n), fixed)],
        out_specs=pl.BlockSpec((tm, n), row),
        out_shape=jax.ShapeDtypeStruct((m, n), F32),
        compiler_params=_params("arbitrary"),
        name="out_proj_ln",
    )(a_p, a_s, w_bf16, resid, g.reshape(1, n), b.reshape(1, n))


def _log_sigmoid(x):
    return jnp.minimum(x, 0.0) - jnp.log1p(jnp.exp(-jnp.abs(x)))


def _mlstm_chunk(p_ref, bg_ref, ng_ref, hg_ref, c_s, n_s, m_s, *, L, L_real):
    gates = p_ref[:, M_GATE_COL:M_IN_PAD] + bg_ref[...]
    lane = lax.broadcasted_iota(jnp.int32, (L, 128), 1)
    x = jnp.where(lane < NH_M, gates, _log_sigmoid(gates))
    if L_real < L:
        tok = lax.broadcasted_iota(jnp.int32, (L, 128), 0)
        x = jnp.where(tok < L_real, x, jnp.where(lane < NH_M, -1e30, 0.0))
    r = lax.broadcasted_iota(jnp.int32, (L, L), 0)
    s = lax.broadcasted_iota(jnp.int32, (L, L), 1)
    causal = r >= s
    tri = causal.astype(F32)
    bcum = jnp.dot(tri, x, precision=HIGHEST, preferred_element_type=F32)
    eye8 = (lax.broadcasted_iota(jnp.int32, (8, 128), 0) == lax.broadcasted_iota(jnp.int32, (8, 128), 1)).astype(F32)
    x_rows = _dot_nt(eye8, x, HIGHEST)
    b_rows = _dot_nt(eye8, bcum, HIGHEST)

    for h in range(NH_M):
        li_row = x_rows[h:h + 1, :]
        li_col = x[:, h:h + 1]
        b_row = b_rows[NH_M + h:NH_M + h + 1, :]
        b_col = bcum[:, NH_M + h:NH_M + h + 1]
        m_old = m_s[:, h:h + 1]
        dm = jnp.where(causal, b_col - b_row + li_row, NEG_INF)
        g_col = b_col + m_old
        mq = jnp.maximum(g_col, jnp.max(dm, axis=1, keepdims=True))
        w_intra = jnp.exp(dm - mq)
        w_inter = jnp.exp(g_col - mq)
        q = p_ref[:, h * DK_M:(h + 1) * DK_M] * (DK_M ** -0.5)
        k = p_ref[:, M_QK + h * DK_M:M_QK + (h + 1) * DK_M]
        v = p_ref[:, 2 * M_QK + h * DV_M:2 * M_QK + (h + 1) * DV_M]
        qb = q.astype(BF16)
        kb = k.astype(BF16)
        cmat = c_s[h]
        nrow = n_s[h:h + 1, :]
        sc = _dot_nt(qb, kb) * w_intra
        num = _dot(sc.astype(BF16), v.astype(BF16)) + w_inter * _dot(qb, cmat.astype(BF16))
        den = jnp.sum(sc, axis=1, keepdims=True) + w_inter * jnp.sum(q * nrow, axis=1, keepdims=True)
        hh = num / jnp.maximum(jnp.abs(den), jnp.exp(-mq))
        bl = b_col[L - 1:L, :]
        lw = bl - b_col + li_col
        m_new = jnp.maximum(bl + m_old, jnp.max(lw, axis=0, keepdims=True))
        wl = jnp.exp(lw - m_new)
        dec = jnp.exp(bl + m_old - m_new)
        c_s[h] = dec * cmat + _dot_tn(kb, (wl * v).astype(BF16))
        n_s[h:h + 1, :] = dec * nrow + jnp.sum(wl * k, axis=0, keepdims=True)
        m_s[:, h:h + 1] = m_new
        mu = jnp.mean(hh, axis=1, keepdims=True)
        hc = hh - mu
        var = jnp.mean(hc * hc, axis=1, keepdims=True)
        hn = hc * lax.rsqrt(var + EPS) * ng_ref[:, h * DV_M:(h + 1) * DV_M]
        o = p_ref[:, 2 * M_QK + M_V + h * DV_M:2 * M_QK + M_V + (h + 1) * DV_M]
        hg_ref[:, h * DV_M:(h + 1) * DV_M] = hn / (1.0 + jnp.exp(-o))


def _mlstm_prompt_kernel(*refs, L, nseq):
    p_refs = refs[:nseq]
    bg_ref, ng_ref, hg_ref, co_ref, no_ref, mo_ref, c_s, n_s, m_s = refs[nseq:]
    c = pl.program_id(0)

    @pl.when(c == 0)
    def _():
        c_s[...] = jnp.zeros_like(c_s)
        n_s[...] = jnp.zeros_like(n_s)
        m_s[...] = jnp.zeros_like(m_s)

    for s in range(nseq):
        _mlstm_chunk(p_refs[s], bg_ref, ng_ref, hg_ref.at[s], c_s.at[s], n_s.at[s], m_s.at[s], L=L, L_real=L)

    @pl.when(c == pl.num_programs(0) - 1)
    def _():
        co_ref[...] = c_s[...]
        no_ref[...] = n_s[...]
        mo_ref[...] = m_s[...]


def mlstm_prompt(p, b_gate_row, norm_g_row):
    L = M_CHUNK
    nc = SEQ // L
    fixed = lambda c: (0, 0)
    whole3 = lambda c: (0, 0, 0)
    p_specs = [pl.BlockSpec((L, M_IN_PAD), functools.partial(lambda c, s: (s * nc + c, 0), s=s)) for s in range(BATCH)]
    hg, cst, nst, mst = pl.pallas_call(
        functools.partial(_mlstm_prompt_kernel, L=L, nseq=BATCH),
        grid=(nc,),
        in_specs=p_specs + [pl.BlockSpec((1, 128), fixed), pl.BlockSpec((1, M_V), fixed)],
        out_specs=[pl.BlockSpec((BATCH, L, M_V), lambda c: (0, c, 0)),
                   pl.BlockSpec((BATCH, NH_M, DK_M, DV_M), lambda c: (0, 0, 0, 0)),
                   pl.BlockSpec((BATCH, NH_M, DK_M), whole3),
                   pl.BlockSpec((BATCH, 1, NH_M), whole3)],
        out_shape=[jax.ShapeDtypeStruct((BATCH, SEQ, M_V), F32),
                   jax.ShapeDtypeStruct((BATCH, NH_M, DK_M, DV_M), F32),
                   jax.ShapeDtypeStruct((BATCH, NH_M, DK_M), F32),
                   jax.ShapeDtypeStruct((BATCH, 1, NH_M), F32)],
        scratch_shapes=[pltpu.VMEM((BATCH, NH_M, DK_M, DV_M), F32), pltpu.VMEM((BATCH, NH_M, DK_M), F32),
                        pltpu.VMEM((BATCH, 1, NH_M), F32)],
        compiler_params=_params("arbitrary"),
        name="mlstm_prompt",
    )(*([p] * BATCH), b_gate_row, norm_g_row)
    return hg.reshape(NP, M_V), cst, nst, mst.reshape(BATCH, NH_M)


def _mlstm_sample_kernel(p_ref, c0_ref, n0_ref, m0_ref, bg_ref, ng_ref, hg_ref, co_ref, no_ref, mo_ref,
                         *, L, L_real, nseq):
    co_ref[...] = c0_ref[...]
    no_ref[...] = n0_ref[...]
    mo_ref[...] = m0_ref[...]
    for s in range(nseq):
        _mlstm_chunk(p_ref.at[pl.ds(s * L, L)], bg_ref, ng_ref, hg_ref.at[pl.ds(s * L, L)],
                     co_ref.at[s], no_ref.at[s], mo_ref.at[s], L=L, L_real=L_real)


def mlstm_sample(p, c0_all, n0_all, m0_all, slot, b_gate_row, norm_g_row, nseq=MLSTM_SEQ_PER_STEP):
    L = 8
    nb = DEC_BATCH
    m0_all = m0_all.reshape(m0_all.shape[0], nb, 1, NH_M)
    fixed = lambda b: (0, 0)
    per_b = lambda b: (b, 0, 0)
    return pl.pallas_call(
        functools.partial(_mlstm_sample_kernel, L=L, L_real=DEC_SEQ, nseq=nseq),
        grid=(nb // nseq,),
        in_specs=[pl.BlockSpec((nseq * L, M_IN_PAD), lambda b: (b, 0)),
                  pl.BlockSpec((None, nseq, NH_M, DK_M, DV_M), lambda b: (slot, b, 0, 0, 0)),
                  pl.BlockSpec((None, nseq, NH_M, DK_M), lambda b: (slot, b, 0, 0)),
                  pl.BlockSpec((None, nseq, 1, NH_M), lambda b: (slot, b, 0, 0)),
                  pl.BlockSpec((1, 128), fixed),
                  pl.BlockSpec((1, M_V), fixed)],
        out_specs=[pl.BlockSpec((nseq * L, M_V), lambda b: (b, 0)),
                   pl.BlockSpec((nseq, NH_M, DK_M, DV_M), lambda b: (b, 0, 0, 0)),
                   pl.BlockSpec((nseq, NH_M, DK_M), per_b),
                   pl.BlockSpec((nseq, 1, NH_M), per_b)],
        out_shape=[jax.ShapeDtypeStruct((nb * L, M_V), F32),
                   jax.ShapeDtypeStruct((nb, NH_M, DK_M, DV_M), F32),
                   jax.ShapeDtypeStruct((nb, NH_M, DK_M), F32),
                   jax.ShapeDtypeStruct((nb, 1, NH_M), F32)],
        compiler_params=_params("arbitrary"),
        name="mlstm_sample",
    )(p, c0_all, n0_all, m0_all, b_gate_row, norm_g_row)


def _topk_mask_lanes(g, k):
    lane = lax.broadcasted_iota(jnp.int32, g.shape, 1)
    n = g.shape[1]
    sel = jnp.zeros(g.shape, F32)
    for _ in range(k):
        mx = jnp.max(g, axis=1, keepdims=True)
        idx = jnp.min(jnp.where(g == mx, lane, n), axis=1, keepdims=True)
        hit = (lane == idx) & (mx > NEG_INF)
        sel = jnp.where(hit, 1.0, sel)
        g = jnp.where(lane == idx, NEG_INF, g)
    return sel


def _moba_select_kernel(q_ref, k_ref, v_ref, ac_ref, qa_ref, ka_ref, vt_ref, kmean_s, *, tq):
    i = pl.program_id(2)

    @pl.when(i == 0)
    def _():
        kmean_s[...] = jnp.zeros_like(kmean_s)
        kmean_s[0:NB_B, :] = jnp.sum(k_ref[...].reshape(NB_B, MOBA_BLOCK, HD_B), axis=1) * (1.0 / MOBA_BLOCK)
        rows = 1024
        lane = lax.broadcasted_iota(jnp.int32, (rows, HD_B), 1)
        rloc = lax.broadcasted_iota(jnp.int32, (rows, HD_B), 0)
        for ch in range(SEQ // rows):
            vt_ref[:, ch * rows:(ch + 1) * rows] = v_ref[ch * rows:(ch + 1) * rows, :].T.astype(BF16)
            row = rloc + ch * rows
            onehot = (lane == row // MOBA_BLOCK).astype(F32)
            offs = (row % MOBA_BLOCK).astype(F32)
            blks = (row // MOBA_BLOCK).astype(F32)
            pat = jnp.where(lane < NB_B, onehot,
                            jnp.where(lane < POS_LANE + 3, offs, jnp.where(lane < POS_LANE + 6, blks, 0.0)))
            ka_ref[ch * rows:(ch + 1) * rows, 0:HD_B] = k_ref[ch * rows:(ch + 1) * rows, :].astype(BF16)
            ka_ref[ch * rows:(ch + 1) * rows, HD_B:AUG] = pat.astype(BF16)

    km = kmean_s[...]
    lane = lax.broadcasted_iota(jnp.int32, (tq, HD_B), 1)
    blk = lax.broadcasted_iota(jnp.int32, (NB_B, tq), 0)
    own = (i * tq + lax.broadcasted_iota(jnp.int32, (NB_B, tq), 1)) // MOBA_BLOCK
    for g in range(G_B):
        q = q_ref[:, g * HD_B:(g + 1) * HD_B]
        gate = _dot_nt(km, q * (HD_B ** -0.5), HIGHEST)[0:NB_B, :]
        gate = jnp.where(blk < own, gate, NEG_INF)
        sel = jnp.zeros((NB_B, tq), F32)
        for _ in range(MOBA_TOPK):
            mx = jnp.max(gate, axis=0, keepdims=True)
            idx = jnp.min(jnp.where(gate == mx, blk, NB_B), axis=0, keepdims=True)
            sel = jnp.where((blk == idx) & (mx > NEG_INF), 1.0, sel)
            gate = jnp.where(blk == idx, NEG_INF, gate)
        sel = jnp.where(blk == own, 1.0, sel)
        maskt = jnp.concatenate([jnp.where(sel > 0.5, 0.0, MASKED), jnp.zeros((128 - NB_B, tq), F32)], axis=0)
        aug = jnp.where(lane < NB_B, maskt.T, ac_ref[g, 0:1, :])
        qa_ref[:, g * AUG:g * AUG + HD_B] = (q * (HD_B ** -0.5 * LOG2E)).astype(BF16)
        qa_ref[:, g * AUG + HD_B:(g + 1) * AUG] = aug.astype(BF16)


def moba_select(p, aug_const, tq=1024):
    kern = functools.partial(_moba_select_kernel, tq=tq)
    nq = SEQ // tq
    per_bc = lambda b, c, i: (b, c, 0, 0)
    return pl.pallas_call(
        kern,
        grid=(BATCH, KVH_B, nq),
        in_specs=[pl.BlockSpec((tq, G_B * HD_B), lambda b, c, i: (b * nq + i, c)),
                  pl.BlockSpec((SEQ, HD_B), lambda b, c, i: (b, NH_B + c)),
                  pl.BlockSpec((SEQ, HD_B), lambda b, c, i: (b, NH_B + KVH_B + c)),
                  pl.BlockSpec((G_B, 8, 128), lambda b, c, i: (c, 0, 0))],
        out_specs=[pl.BlockSpec((tq, G_B * AUG), lambda b, c, i: (b * nq + i, c)),
                   pl.BlockSpec((None, None, SEQ, AUG), per_bc),
                   pl.BlockSpec((None, None, HD_B, SEQ), per_bc)],
        out_shape=[jax.ShapeDtypeStruct((NP, NH_B * AUG), BF16),
                   jax.ShapeDtypeStruct((BATCH, KVH_B, SEQ, AUG), BF16),
                   jax.ShapeDtypeStruct((BATCH, KVH_B, HD_B, SEQ), BF16)],
        scratch_shapes=[pltpu.VMEM((128, HD_B), F32)],
        compiler_params=_params("arbitrary", "arbitrary", "arbitrary"),
        name="moba_select",
    )(p, p, p, aug_const)


def _moba_flash_kernel(qa_ref, ka_ref, vt_ref, o_ref, s_buf):
    i = pl.program_id(2)
    bq = MOBA_QTILE
    tk = MOBA_QTILE
    nl = G_B * bq
    qa = jnp.concatenate([qa_ref[:, g * AUG:(g + 1) * AUG] for g in range(G_B)], axis=0)

    def scores(t, slot):
        off = pl.multiple_of(t * tk, tk)
        s_buf[slot] = _dot_nt(ka_ref[pl.ds(off, tk), :], qa)

    def softmax_pv(t, slot, last, carry):
        m, l, acc = carry
        st = s_buf[slot]
        if last:
            kpos = t * tk + lax.broadcasted_iota(jnp.int32, (tk, nl), 0)
            qpos = i * bq + lax.broadcasted_iota(jnp.int32, (tk, nl), 1) % bq
            st = jnp.where(kpos <= qpos, st, NEG_INF)
        m_new = jnp.maximum(m, jnp.max(st, axis=0, keepdims=True))
        pt = jnp.exp2(st - m_new)
        a = jnp.exp2(m - m_new)
        l = a * l + jnp.sum(pt, axis=0, keepdims=True)
        off = pl.multiple_of(t * tk, tk)
        acc = a * acc + _dot(vt_ref[:, pl.ds(off, tk)], pt.astype(BF16))
        return m_new, l, acc

    def pair(u, carry):
        scores(2 * u + 1, 1)
        carry = softmax_pv(2 * u, 0, False, carry)
        scores(2 * u + 2, 0)
        return softmax_pv(2 * u + 1, 1, False, carry)

    def tail_two(carry):
        t = 2 * npairs
        scores(t + 1, 1)
        return softmax_pv(t + 1, 1, True, softmax_pv(t, 0, False, carry))

    def tail_one(carry):
        return softmax_pv(2 * npairs, 0, True, carry)

    n_full = i
    npairs = n_full // 2
    init = (jnp.full((1, nl), MASKED, F32), jnp.zeros((1, nl), F32), jnp.zeros((HD_B, nl), F32))
    scores(0, 0)
    carry = lax.fori_loop(0, npairs, pair, init)
    m, l, acc = lax.cond(n_full % 2 == 1, tail_two, tail_one, carry)
    out = acc / l
    for g in range(G_B):
        o_ref[:, g * HD_B:(g + 1) * HD_B] = out[:, g * bq:(g + 1) * bq].T


def moba_flash(q_aug, k_aug, v_t):
    nq = SEQ // MOBA_QTILE
    per_bc = lambda b, c, i: (b, c, 0, 0)
    return pl.pallas_call(
        _moba_flash_kernel,
        grid=(BATCH, KVH_B, nq),
        in_specs=[pl.BlockSpec((MOBA_QTILE, G_B * AUG), lambda b, c, i: (b * nq + i, c)),
                  pl.BlockSpec((None, None, SEQ, AUG), per_bc),
                  pl.BlockSpec((None, None, HD_B, SEQ), per_bc)],
        out_specs=pl.BlockSpec((MOBA_QTILE, G_B * HD_B), lambda b, c, i: (b * nq + i, c)),
        out_shape=jax.ShapeDtypeStruct((NP, NH_B * HD_B), F32),
        scratch_shapes=[pltpu.VMEM((2, MOBA_QTILE, G_B * MOBA_QTILE), F32)],
        compiler_params=_params("arbitrary", "arbitrary", "arbitrary"),
        name="moba_flash",
    )(q_aug, k_aug, v_t)


def _kv_rows_kernel(k_ref, v_ref, ko_ref, vo_ref, *, tm):
    for c in range(KVH_B):
        ko_ref[pl.ds(c, tm, stride=KVH_B), :] = k_ref[:, c * HD_B:(c + 1) * HD_B]
        vo_ref[pl.ds(c, tm, stride=KVH_B), :] = v_ref[:, c * HD_B:(c + 1) * HD_B]


def moba_kv_rows(p, tm=512):
    w = KVH_B * HD_B
    kcol = NH_B * HD_B // w
    out = jax.ShapeDtypeStruct((NP * KVH_B, HD_B), F32)
    return pl.pallas_call(
        functools.partial(_kv_rows_kernel, tm=tm),
        grid=(NP // tm,),
        in_specs=[pl.BlockSpec((tm, w), lambda i: (i, kcol)), pl.BlockSpec((tm, w), lambda i: (i, kcol + 1))],
        out_specs=[pl.BlockSpec((tm * KVH_B, HD_B), lambda i: (i, 0))] * 2,
        out_shape=[out, out],
        compiler_params=_params("arbitrary"),
        name="moba_kv_rows",
    )(p, p)


def _moba_aug_const():
    c = np.zeros((NH_B, 8, 128), np.float32)
    for h in range(NH_B):
        beta = 2.0 ** (-8.0 * (h + 1) / NH_B) * LOG2E
        c[h, 0, POS_LANE:POS_LANE + 3] = _bf16_parts(beta)
        c[h, 0, POS_LANE + 3:POS_LANE + 6] = _bf16_parts(beta * MOBA_BLOCK)
    return jnp.asarray(c)


def _moba_sample_kernel(pt_ref, qa_ref, qat_ref, kn_ref, vn_ref, bias_ref, ck_ref, cv_ref,
                        o_ref, buf, sem, s_s, p_s, ksum_s, acc_s, inv_s):
    npg = SAMPLE_PAGES_PER_STEP
    w = KVH_B * HD_B
    nrow = KVH_B * DEC_SEQ * G_B

    def load_page(sl, pg):
        return jnp.concatenate([buf[sl, pg, pl.ds(c, PAGE_SIZE, stride=KVH_B), :] for c in range(KVH_B)], axis=1)

    b = pl.program_id(0)
    t = pl.program_id(1)
    nb = pl.num_programs(0)
    g = b * 8 + t
    slot = g % SAMPLE_SLOTS
    ahead = SAMPLE_SLOTS - 1

    def page_copy(src_ref, page, sl, pg):
        return pltpu.make_async_copy(src_ref.at[page], buf.at[sl, pg], sem.at[sl])

    def start_fetch(bn, tn, sl):
        qn = tn % 4

        @pl.when(tn < 4)
        def _():
            for pg in range(npg):
                page_copy(ck_ref, pt_ref[bn, qn * npg + pg], sl, pg).start()

        @pl.when(tn >= 4)
        def _():
            for pg in range(npg):
                page_copy(cv_ref, pt_ref[bn, qn * npg + pg], sl, pg).start()

    @pl.when(g == 0)
    def _():
        for k in range(ahead):
            start_fetch(0, k, k)

    @pl.when(g + ahead < nb * 8)
    def _():
        gn = g + ahead
        start_fetch(gn // 8, gn % 8, gn % SAMPLE_SLOTS)

    for pg in range(npg):
        page_copy(ck_ref, 0, slot, pg).wait()

    qab = (qa_ref[...] * LOG2E).astype(BF16)

    @pl.when(g == 0)
    def _():
        p_s[...] = jnp.zeros_like(p_s)

    @pl.when(t < 4)
    def _():
        for pg in range(npg):
            page = load_page(slot, pg)
            col0 = pl.multiple_of((t * npg + pg) * PAGE_SIZE, PAGE_SIZE)
            s_s[:, pl.ds(col0, PAGE_SIZE)] = _dot(page.astype(BF16), qab).T
            csum = jnp.sum(page, axis=0, keepdims=True)
            if pg % 2 == 0:
                prev = csum
            else:
                ksum_s[pl.ds(t * (npg // 2) + pg // 2, 1), :] = prev + csum

    @pl.when(t == 3)
    def _():
        qat = qat_ref[...]
        kmean = ksum_s[...] * (1.0 / MOBA_BLOCK)
        sel = _topk_mask_lanes(_dot_nt(qat, kmean, HIGHEST), MOBA_TOPK)
        kn = jnp.concatenate([kn_ref[...], jnp.zeros((PAGE_SIZE - 8, w), F32)], axis=0)
        s_s[0:nrow, PAST:PAST + PAGE_SIZE] = _dot_nt((qat * LOG2E).astype(BF16), kn.astype(BF16))
        selm = jnp.where(sel > 0.5, 0.0, NEG_INF)

        def logits(bk):
            if bk == NB_S:
                return s_s[0:nrow, PAST:PAST + PAGE_SIZE] + bias_ref[:, PAST:PAST + PAGE_SIZE]
            lo = bk * MOBA_BLOCK
            return s_s[0:nrow, lo:lo + MOBA_BLOCK] + bias_ref[:, lo:lo + MOBA_BLOCK] + selm[:, bk:bk + 1]

        macc = logits(0)
        for bk in range(1, NB_S):
            macc = jnp.maximum(macc, logits(bk))
        mx = jnp.maximum(jnp.max(macc, axis=1, keepdims=True), jnp.max(logits(NB_S), axis=1, keepdims=True))
        dacc = jnp.zeros((nrow, MOBA_BLOCK), F32)
        for bk in range(NB_S):
            pr = jnp.exp2(logits(bk) - mx)
            p_s[0:nrow, bk * MOBA_BLOCK:(bk + 1) * MOBA_BLOCK] = pr
            dacc = dacc + pr
        pr = jnp.exp2(logits(NB_S) - mx)
        p_s[0:nrow, PAST:PAST + PAGE_SIZE] = pr
        den = jnp.sum(dacc, axis=1, keepdims=True) + jnp.sum(pr, axis=1, keepdims=True)
        inv_s[...] = jnp.zeros_like(inv_s)
        inv_s[0:nrow, :] = jnp.broadcast_to(1.0 / den, (nrow, 128))
        inv_s[...] = inv_s[...].T

    @pl.when(t == 4)
    def _():
        vn = jnp.concatenate([vn_ref[...], jnp.zeros((PAGE_SIZE - 8, w), F32)], axis=0)
        acc_s[...] = _dot_tn(vn.astype(BF16), p_s[:, PAST:PAST + PAGE_SIZE].T.astype(BF16))

    @pl.when(t >= 4)
    def _():
        acc = acc_s[...]
        for pg in range(npg):
            col0 = pl.multiple_of(((t - 4) * npg + pg) * PAGE_SIZE, PAGE_SIZE)
            pt = p_s[:, pl.ds(col0, PAGE_SIZE)].T
            acc = acc + _dot_tn(load_page(slot, pg).astype(BF16), pt.astype(BF16))
        acc_s[...] = acc

    @pl.when(t == 7)
    def _():
        o_ref[...] = acc_s[:, 0:nrow] * inv_s[0:1, 0:nrow]


def moba_sample(page_table, q_aug, q_aug_t, k_new, v_new, bias, cache_k, cache_v):
    w = KVH_B * HD_B
    nrow = KVH_B * DEC_SEQ * G_B
    nkeys = PAST + PAGE_SIZE
    per_b3 = lambda b, t, pt: (b, 0, 0)
    fixed = lambda b, t, pt: (0, 0)
    grid_spec = pltpu.PrefetchScalarGridSpec(
        num_scalar_prefetch=1,
        grid=(DEC_BATCH, 8),
        in_specs=[pl.BlockSpec((None, w, 128), per_b3),
                  pl.BlockSpec((None, nrow, w), per_b3),
                  pl.BlockSpec((None, 8, w), per_b3),
                  pl.BlockSpec((None, 8, w), per_b3),
                  pl.BlockSpec((nrow, nkeys), fixed),
                  pl.BlockSpec(memory_space=pl.ANY),
                  pl.BlockSpec(memory_space=pl.ANY)],
        out_specs=pl.BlockSpec((None, w, nrow), per_b3),
        scratch_shapes=[pltpu.VMEM((SAMPLE_SLOTS, SAMPLE_PAGES_PER_STEP, PAGE_SIZE * KVH_B, HD_B), F32),
                        pltpu.SemaphoreType.DMA((SAMPLE_SLOTS,)),
                        pltpu.VMEM((128, nkeys), F32),
                        pltpu.VMEM((128, nkeys), F32),
                        pltpu.VMEM((NB_S, w), F32),
                        pltpu.VMEM((w, 128), F32),
                        pltpu.VMEM((128, 128), F32)],
    )
    return pl.pallas_call(
        _moba_sample_kernel,
        grid_spec=grid_spec,
        out_shape=jax.ShapeDtypeStruct((DEC_BATCH, w, nrow), F32),
        compiler_params=_params("arbitrary", "arbitrary"),
        name="moba_sample",
    )(page_table, q_aug, q_aug_t, k_new, v_new, bias, cache_k, cache_v)


def _swa_slope(h):
    return 2.0 ** (-8.0 * (h + 1) / NH_C)


def _by_head(gidx, values):
    out = values[-1]
    for g in range(len(values) - 2, -1, -1):
        out = jnp.where(gidx == g, values[g], out)
    return out


def _swa_kv_aug(k_all, v_all, c):
    nk = k_all.shape[0]
    lane = lax.broadcasted_iota(jnp.int32, (nk, HD_C), 1)
    kidx = lax.broadcasted_iota(jnp.int32, (nk, HD_C), 0).astype(F32)
    kpat = jnp.where(lane < 3, kidx, 0.0)
    ka = jnp.concatenate([k_all[:, c * HD_C:(c + 1) * HD_C], kpat], axis=1).astype(BF16)
    return ka, v_all[:, c * HD_C:(c + 1) * HD_C].astype(BF16)


def _swa_softmax_pv(qa, ka, vc, maskt, sink2):
    s2 = _dot_nt(qa, ka) + maskt
    mx = jnp.maximum(jnp.max(s2, axis=1, keepdims=True), sink2)
    pr = jnp.exp2(s2 - mx)
    den = jnp.sum(pr, axis=1, keepdims=True) + jnp.exp2(sink2 - mx)
    return _dot(pr.astype(BF16), vc) / den


def _store_heads(o_ref, outs, lead=()):
    for j in range(0, NH_C, 2):
        o_ref[lead + (slice(None), slice(j * HD_C, (j + 2) * HD_C))] = jnp.concatenate(outs[j:j + 2], axis=1)


def _swa_prompt_kernel(q_ref, kvp_ref, kvc_ref, sink_ref, o_ref):
    i = pl.program_id(1)
    w = WINDOW
    kw = KVH_C * HD_C
    k_all = jnp.concatenate([kvp_ref[:, 0:kw], kvc_ref[:, 0:kw]], axis=0)
    v_all = jnp.concatenate([kvp_ref[:, kw:2 * kw], kvc_ref[:, kw:2 * kw]], axis=0)
    kk = lax.broadcasted_iota(jnp.int32, (2 * w, w), 0)
    qq = lax.broadcasted_iota(jnp.int32, (2 * w, w), 1)
    di = qq - kk + w
    ok = (di >= 0) & (di <= w) & ((kk >= w) | (i > 0))
    mask1 = jnp.where(ok, 0.0, NEG_INF)
    maskt = jnp.concatenate([mask1] * G_C, axis=1)
    qdist = (lax.broadcasted_iota(jnp.int32, (1, w), 1) + w).astype(F32)
    qlane = lax.broadcasted_iota(jnp.int32, (w, HD_C), 1)
    sink = sink_ref[...]
    for c in range(KVH_C):
        ka, vc = _swa_kv_aug(k_all, v_all, c)
        qas, sinks = [], []
        for gi in range(G_C):
            h = c * G_C + gi
            b1, b2, b3 = _bf16_parts(_swa_slope(h) * LOG2E)
            qpat = jnp.where(qlane == 0, b1, jnp.where(qlane == 1, b2, jnp.where(qlane == 2, b3, 0.0)))
            qa = jnp.concatenate([q_ref[:, h * HD_C:(h + 1) * HD_C] * (HD_C ** -0.5 * LOG2E), qpat], axis=1)
            qas.append(qa.astype(BF16))
            sinks.append((sink[:, h:h + 1] + _swa_slope(h) * qdist) * LOG2E)
        sink2 = jnp.concatenate(sinks, axis=1)
        st = _dot_nt(ka, jnp.concatenate(qas, axis=0)) + maskt
        mx = jnp.maximum(jnp.max(st, axis=0, keepdims=True), sink2)
        pt = jnp.exp2(st - mx)
        den = jnp.sum(pt, axis=0, keepdims=True) + jnp.exp2(sink2 - mx)
        ot = _dot_tn(vc, pt.astype(BF16)) / den
        for gi in range(0, G_C, 2):
            h = c * G_C + gi
            two = jnp.concatenate([ot[:, gi * w:(gi + 1) * w], ot[:, (gi + 1) * w:(gi + 2) * w]], axis=0)
            o_ref[:, h * HD_C:(h + 2) * HD_C] = two.T


def swa_prompt(p, sink_row):
    nblk = SEQ // WINDOW
    kvb = NH_C * HD_C // (2 * KVH_C * HD_C)
    return pl.pallas_call(
        _swa_prompt_kernel,
        grid=(BATCH, nblk),
        in_specs=[pl.BlockSpec((WINDOW, NH_C * HD_C), lambda b, i: (b * nblk + i, 0)),
                  pl.BlockSpec((WINDOW, 2 * KVH_C * HD_C), lambda b, i: (b * nblk + jnp.maximum(i - 1, 0), kvb)),
                  pl.BlockSpec((WINDOW, 2 * KVH_C * HD_C), lambda b, i: (b * nblk + i, kvb)),
                  pl.BlockSpec((1, 128), lambda b, i: (0, 0))],
        out_specs=pl.BlockSpec((WINDOW, NH_C * HD_C), lambda b, i: (b * nblk + i, 0)),
        out_shape=jax.ShapeDtypeStruct((NP, NH_C * HD_C), F32),
        compiler_params=_params("arbitrary", "arbitrary"),
        name="swa_prompt",
    )(p, p, p, sink_row)


def _swa_sample_kernel(q_ref, kn_ref, vn_ref, ck_ref, cv_ref, sink_ref, o_ref):
    w = WINDOW
    nk = w + 8
    nr = G_C * 8
    r = lax.broadcasted_iota(jnp.int32, (nr, nk), 0)
    cc = lax.broadcasted_iota(jnp.int32, (nr, nk), 1)
    di = w + r % 8 - cc
    ok = (di >= 0) & (di <= w) & (cc < w + DEC_SEQ)
    maskt = jnp.where(ok, 0.0, NEG_INF)
    rcol = lax.broadcasted_iota(jnp.int32, (nr, 1), 0)
    gcol = rcol // 8
    qdist = (w + rcol % 8).astype(F32)
    qlane = lax.broadcasted_iota(jnp.int32, (nr, HD_C), 1)
    sink = sink_ref[...]
    qpats, sink2s = [], []
    for c in range(KVH_C):
        heads = [c * G_C + gi for gi in range(G_C)]
        parts = [_bf16_parts(_swa_slope(h) * LOG2E) for h in heads]
        b = [_by_head(gcol, [pp[j] for pp in parts]) for j in range(3)]
        qpats.append(jnp.where(qlane == 0, b[0], jnp.where(qlane == 1, b[1], jnp.where(qlane == 2, b[2], 0.0))))
        slope = _by_head(gcol, [_swa_slope(h) for h in heads])
        sk = _by_head(gcol, [sink[:, h:h + 1] for h in heads])
        sink2s.append((sk + slope * qdist) * LOG2E)
    for sq in range(SWA_SEQ_PER_STEP):
        q = q_ref[sq] * (HD_C ** -0.5 * LOG2E)
        k_all = jnp.concatenate([ck_ref[sq], kn_ref[sq]], axis=0)
        v_all = jnp.concatenate([cv_ref[sq], vn_ref[sq]], axis=0)
        outs = []
        for c in range(KVH_C):
            ka, vc = _swa_kv_aug(k_all, v_all, c)
            qc = jnp.concatenate([q[:, (c * G_C + gi) * HD_C:(c * G_C + gi + 1) * HD_C] for gi in range(G_C)], axis=0)
            qa = jnp.concatenate([qc, qpats[c]], axis=1).astype(BF16)
            oc = _swa_softmax_pv(qa, ka, vc, maskt, sink2s[c])
            outs.extend(oc[gi * 8:(gi + 1) * 8, :] for gi in range(G_C))
        _store_heads(o_ref, outs, lead=(sq,))


def swa_sample(q, k_new, v_new, cache_k, cache_v, sink_row):
    n = SWA_SEQ_PER_STEP
    kw = KVH_C * HD_C
    blk = lambda i: (i, 0, 0)
    return pl.pallas_call(
        _swa_sample_kernel,
        grid=(DEC_BATCH // n,),
        in_specs=[pl.BlockSpec((n, 8, NH_C * HD_C), blk), pl.BlockSpec((n, 8, kw), blk), pl.BlockSpec((n, 8, kw), blk),
                  pl.BlockSpec((n, WINDOW, kw), blk), pl.BlockSpec((n, WINDOW, kw), blk),
                  pl.BlockSpec((1, 128), lambda i: (0, 0))],
        out_specs=pl.BlockSpec((n, 8, NH_C * HD_C), blk),
        out_shape=jax.ShapeDtypeStruct((DEC_BATCH, 8, NH_C * HD_C), F32),
        compiler_params=_params("arbitrary"),
        name="swa_sample",
    )(q, k_new, v_new, cache_k, cache_v, sink_row)


def _router_kernel(x_ref, wt_ref, bt_ref, meta_ref, wts_ref, seg_ref, tri_s, *, tm):
    @pl.when(pl.program_id(0) == 0)
    def _():
        rr = lax.broadcasted_iota(jnp.int32, (128, tm), 0)
        ccn = lax.broadcasted_iota(jnp.int32, (128, tm), 1)
        for ch in range(tm // 128):
            tri_s[ch * 128:(ch + 1) * 128, :] = (rr + ch * 128 < ccn).astype(BF16)

    lt = _dot_nt(wt_ref[...], x_ref[...], HIGHEST) + bt_ref[...]
    row = lax.broadcasted_iota(jnp.int32, lt.shape, 0)
    big = ROUTER_ROWS

    lg = jnp.where(row < N_GROUPS, lt, NEG_INF)
    mg = jnp.max(lg, axis=0, keepdims=True)
    eg = jnp.exp(lg - mg)
    pg = eg / jnp.sum(eg, axis=0, keepdims=True)
    pg1 = jnp.max(pg, axis=0, keepdims=True)
    g1 = jnp.min(jnp.where(pg == pg1, row, big), axis=0, keepdims=True)

    e = row - N_GROUPS
    ingroup = (e >= g1 * N_EXP) & (e < (g1 + 1) * N_EXP)
    le = jnp.where(ingroup, lt, NEG_INF)
    me = jnp.max(le, axis=0, keepdims=True)
    ee = jnp.exp(le - me)
    pe = ee / jnp.sum(ee, axis=0, keepdims=True)
    pe = jnp.where(ingroup, pe, NEG_INF)
    p1 = jnp.max(pe, axis=0, keepdims=True)
    i1 = jnp.min(jnp.where(pe == p1, row, big), axis=0, keepdims=True)
    pe2 = jnp.where(row == i1, NEG_INF, pe)
    p2 = jnp.max(pe2, axis=0, keepdims=True)
    i2 = jnp.min(jnp.where(pe2 == p2, row, big), axis=0, keepdims=True)
    tot = p1 + p2
    w1 = p1 / tot * pg1
    w2 = p2 / tot * pg1

    hit1 = row == i1
    hit2 = row == i2
    oh = jnp.where(hit1 | hit2, 1.0, 0.0)
    ohb = oh.astype(BF16)
    before = _dot(ohb, tri_s[...])
    cnt_col = jnp.sum(oh, axis=1, keepdims=True)
    pad_col = jnp.floor((cnt_col + 7.0) * 0.125) * 8.0
    r128 = lax.broadcasted_iota(jnp.int32, (128, 128), 0)
    c128 = lax.broadcasted_iota(jnp.int32, (128, 128), 1)
    off_col = jnp.dot((c128 < r128).astype(F32), pad_col + jnp.zeros((128, 128), F32),
                      precision=HIGHEST, preferred_element_type=F32)[:, 0:1]
    place = before + off_col
    pos1 = jnp.sum(jnp.where(hit1, place, 0.0), axis=0, keepdims=True)
    pos2 = jnp.sum(jnp.where(hit2, place, 0.0), axis=0, keepdims=True)
    cnt_row = _dot_nt(jnp.ones((8, tm), BF16), ohb)
    pad_row = jnp.floor((cnt_row + 7.0) * 0.125) * 8.0
    off_row = jnp.dot(pad_row, (r128 < c128).astype(F32), precision=HIGHEST, preferred_element_type=F32)

    r8 = lax.broadcasted_iota(jnp.int32, (8, tm), 0)
    meta_ref[...] = jnp.where(r8 == 0, pos1, jnp.where(r8 == 1, pos2, 0.0)).astype(I32)
    wts_ref[...] = jnp.where(r8 == 0, w1, jnp.where(r8 == 1, w2, 0.0))
    s8 = lax.broadcasted_iota(jnp.int32, (8, 128), 0)
    seg_ref[...] = jnp.where(s8 == 0, off_row, jnp.where(s8 == 1, cnt_row, 0.0)).astype(I32)


def moe_router(x, wt_router, bt_router, tm=MOE_TM):
    m = x.shape[0]
    nt = m // tm
    blk = lambda i: (i, 0, 0)
    return pl.pallas_call(
        functools.partial(_router_kernel, tm=tm),
        grid=(nt,),
        in_specs=[pl.BlockSpec((tm, D_MODEL), lambda i: (i, 0)),
                  pl.BlockSpec((ROUTER_ROWS, D_MODEL), lambda i: (0, 0)),
                  pl.BlockSpec((ROUTER_ROWS, 1), lambda i: (0, 0))],
        out_specs=[pl.BlockSpec((None, 8, tm), blk), pl.BlockSpec((None, 8, tm), blk),
                   pl.BlockSpec((None, 8, 128), blk)],
        out_shape=[jax.ShapeDtypeStruct((nt, 8, tm), I32), jax.ShapeDtypeStruct((nt, 8, tm), F32),
                   jax.ShapeDtypeStruct((nt, 8, 128), I32)],
        scratch_shapes=[pltpu.VMEM((tm, tm), BF16)],
        compiler_params=_params("arbitrary"),
        name="moe_router",
    )(x, wt_router, bt_router)


def _moe_kernel(x_ref, meta_hbm, wts_hbm, seg_hbm, wg_hbm, wu_hbm, wd_hbm, g_ref, b_ref, o_ref,
                xs, wbg, wbu, wbd, wsem, meta_s, wts_s, seg_s, msem, *, layer, tm):
    i = pl.program_id(0)
    nt = pl.num_programs(0)
    rc = MOE_RC
    ahead = MOE_WSLOTS - 1
    base = i * N_EXPERTS
    ms = i % 2

    def weight_copies(step):
        e = step % N_EXPERTS
        sl = step % MOE_WSLOTS
        return (pltpu.make_async_copy(wg_hbm.at[layer, e], wbg.at[sl], wsem.at[sl, 0]),
                pltpu.make_async_copy(wu_hbm.at[layer, e], wbu.at[sl], wsem.at[sl, 1]),
                pltpu.make_async_copy(wd_hbm.at[layer, e], wbd.at[sl], wsem.at[sl, 2]))

    def meta_copies(tile, sl):
        return (pltpu.make_async_copy(meta_hbm.at[tile], meta_s.at[sl], msem.at[sl, 0]),
                pltpu.make_async_copy(wts_hbm.at[tile], wts_s.at[sl], msem.at[sl, 1]),
                pltpu.make_async_copy(seg_hbm.at[tile], seg_s.at[sl], msem.at[sl, 2]))

    @pl.when(i == 0)
    def _():
        for cp in meta_copies(0, 0):
            cp.start()
        for step in range(ahead):
            for cp in weight_copies(step):
                cp.start()
        xs[...] = jnp.zeros_like(xs)

    for cp in meta_copies(i, ms):
        cp.wait()

    @pl.when(i + 1 < nt)
    def _():
        for cp in meta_copies(i + 1, 1 - ms):
            cp.start()

    def gather(t, carry):
        rowv = x_ref[pl.ds(t, 1), :]
        xs[pl.ds(meta_s[ms, 0, t], 1), :] = rowv
        xs[pl.ds(meta_s[ms, 1, t], 1), :] = rowv
        return carry

    lax.fori_loop(0, tm, gather, 0, unroll=8)

    def expert(e, carry):
        step = base + e
        sl = step % MOE_WSLOTS
        for cp in weight_copies(step):
            cp.wait()

        @pl.when(step + ahead < nt * N_EXPERTS)
        def _():
            for cp in weight_copies(step + ahead):
                cp.start()

        off = seg_s[ms, 0, N_GROUPS + e]
        cnt = seg_s[ms, 1, N_GROUPS + e]
        wgb = wbg[sl]
        wub = wbu[sl]
        wdb = wbd[sl]
        ridx = lax.broadcasted_iota(jnp.int32, (rc, 1), 0)

        def chunk(k, c2):
            row0 = pl.multiple_of(off + k * rc, 8)
            lhs = xs[pl.ds(row0, rc), :]
            lb = lhs.astype(BF16)
            hg = _dot(lb, wgb)
            hu = _dot(lb, wub)
            hh = hg / (1.0 + jnp.exp(-hg)) * hu
            out = _dot(hh.astype(BF16), wdb)
            xs[pl.ds(row0, rc), :] = jnp.where(ridx < cnt - k * rc, out, lhs)
            return c2

        lax.fori_loop(0, (cnt + rc - 1) // rc, chunk, 0)
        return carry

    lax.fori_loop(0, N_EXPERTS, expert, 0)

    def combine(t, carry):
        y = (wts_s[ms, 0, t] * xs[pl.ds(meta_s[ms, 0, t], 1), :]
             + wts_s[ms, 1, t] * xs[pl.ds(meta_s[ms, 1, t], 1), :])
        o_ref[pl.ds(t, 1), :] = y
        return carry

    lax.fori_loop(0, tm, combine, 0, unroll=8)
    o_ref[...] = _layer_norm(ALPHA * x_ref[...] + o_ref[...], g_ref[...], b_ref[...])


def moe_ffn_ln(x, meta, wts, seg, wg, wu, wd, g, b, layer, tm=MOE_TM):
    m = x.shape[0]
    row = lambda i: (i, 0)
    fixed = lambda i: (0, 0)
    hbm = pl.BlockSpec(memory_space=pl.ANY)
    return pl.pallas_call(
        functools.partial(_moe_kernel, layer=layer, tm=tm),
        grid=(m // tm,),
        in_specs=[pl.BlockSpec((tm, D_MODEL), row), hbm, hbm, hbm, hbm, hbm, hbm,
                  pl.BlockSpec((1, D_MODEL), fixed), pl.BlockSpec((1, D_MODEL), fixed)],
        out_specs=pl.BlockSpec((tm, D_MODEL), row),
        out_shape=jax.ShapeDtypeStruct((m, D_MODEL), F32),
        scratch_shapes=[pltpu.VMEM((MOE_ROWS, D_MODEL), F32),
                        pltpu.VMEM((MOE_WSLOTS, D_MODEL, D_FF), BF16), pltpu.VMEM((MOE_WSLOTS, D_MODEL, D_FF), BF16),
                        pltpu.VMEM((MOE_WSLOTS, D_FF, D_MODEL), BF16), pltpu.SemaphoreType.DMA((MOE_WSLOTS, 3)),
                        pltpu.SMEM((2, 8, tm), I32), pltpu.SMEM((2, 8, tm), F32), pltpu.SMEM((2, 8, 128), I32),
                        pltpu.SemaphoreType.DMA((2, 3))],
        compiler_params=_params("arbitrary"),
        name="moe_ffn_ln",
    )(x, meta, wts, seg, wg, wu, wd, g.reshape(1, D_MODEL), b.reshape(1, D_MODEL))


def _pad_rows(a, rows):
    return jnp.pad(a, ((0, 0), (0, rows - a.shape[1]), (0, 0)))


def _mlstm_layer(x, w_in, b_gate, norm_g, w_out, c0_all, n0_all, m0_all, slot):
    w = jnp.pad(w_in, ((0, 0), (0, M_IN_PAD - M_IN))).astype(BF16)
    p = matmul(x, w)
    bg = jnp.pad(b_gate, (0, 128 - 2 * NH_M)).reshape(1, 128)
    ng = norm_g.reshape(1, M_V)
    hp, cp, np_, mp = mlstm_prompt(p, bg, ng)
    ps = _pad_rows(p[NP:].reshape(DEC_BATCH, DEC_SEQ, M_IN_PAD), 8).reshape(DEC_BATCH * 8, M_IN_PAD)
    hs, cs, ns, ms = mlstm_sample(ps, c0_all, n0_all, m0_all, slot, bg, ng)
    hs = hs.reshape(DEC_BATCH, 8, M_V)[:, :DEC_SEQ].reshape(NS, M_V)
    states = (cp, np_, mp, cs, ns, ms.reshape(DEC_BATCH, NH_M))
    return (hp, hs), w_out.astype(BF16), states


def _moba_sample_part(ps, cache_k, cache_v, page_rows):
    hq = NH_B * HD_B
    hk = KVH_B * HD_B
    slopes_h = jnp.exp2(-8.0 * jnp.arange(1, NH_B + 1, dtype=F32) / NH_B)
    q = (ps[:, :hq] * (HD_B ** -0.5)).reshape(DEC_BATCH, DEC_SEQ, KVH_B, G_B, HD_B)
    eye = jnp.eye(KVH_B, dtype=F32)
    nrow = KVH_B * DEC_SEQ * G_B
    q_aug_t = jnp.einsum("bscgd,ce->besgcd", q, eye).reshape(DEC_BATCH, nrow, hk)
    q_aug = jnp.pad(jnp.swapaxes(q_aug_t, 1, 2), ((0, 0), (0, 0), (0, 128 - nrow)))
    k_new = ps[:, hq:hq + hk].reshape(DEC_BATCH, DEC_SEQ, hk)
    v_new = ps[:, hq + hk:].reshape(DEC_BATCH, DEC_SEQ, hk)
    row = jnp.arange(nrow)
    row_c, row_s, row_g = row // (DEC_SEQ * G_B), (row // G_B) % DEC_SEQ, row % G_B
    slope_row = (slopes_h[row_c * G_B + row_g] * LOG2E)[:, None]
    qs_row = row_s.astype(F32)[:, None]
    past_bias = -slope_row * (float(PAST) + qs_row - jnp.arange(PAST, dtype=F32)[None, :])
    tok = jnp.arange(PAGE_SIZE, dtype=F32)[None, :]
    own_bias = jnp.where((tok <= qs_row) & (tok < float(DEC_SEQ)), -slope_row * (qs_row - tok), NEG_INF)
    bias = jnp.concatenate([past_bias, own_bias], axis=1)
    ot = moba_sample(page_rows, q_aug, q_aug_t, _pad_rows(k_new, 8), _pad_rows(v_new, 8), bias, cache_k, cache_v)
    ot = ot.reshape(DEC_BATCH, KVH_B, HD_B, KVH_B, DEC_SEQ, G_B)
    os_ = jnp.einsum("bcdesg,ce->bscgd", ot, eye).reshape(NS, hq)
    return os_, k_new, v_new


def _moba_layer(x, w_in, w_out, cache_k, cache_v, page_rows):
    p = matmul(x, w_in.astype(BF16))
    hq = NH_B * HD_B
    hk = KVH_B * HD_B
    ac = _moba_aug_const()
    q_aug, k_aug, v_t = moba_select(p, ac)
    op = moba_flash(q_aug, k_aug, v_t)
    os_, k_new, v_new = _moba_sample_part(p[NP:], cache_k, cache_v, page_rows)
    k_rows, v_rows = moba_kv_rows(p)
    kv = (k_rows.reshape(BATCH, SEQ, KVH_B, HD_B), v_rows.reshape(BATCH, SEQ, KVH_B, HD_B),
          k_new.reshape(DEC_BATCH, DEC_SEQ, KVH_B, HD_B), v_new.reshape(DEC_BATCH, DEC_SEQ, KVH_B, HD_B))
    return (op, os_), w_out.astype(BF16), kv


def _swa_layer(x, w_in, sinks, w_out, cache_k, cache_v):
    p = matmul(x, w_in.astype(BF16))
    hq = NH_C * HD_C
    kw = KVH_C * HD_C
    sink_row = jnp.pad(sinks, (0, 128 - NH_C)).reshape(1, 128)
    op = swa_prompt(p, sink_row)
    ps = p[NP:]
    q = _pad_rows(ps[:, :hq].reshape(DEC_BATCH, DEC_SEQ, hq), 8)
    k_new = ps[:, hq:hq + kw].reshape(DEC_BATCH, DEC_SEQ, kw)
    v_new = ps[:, hq + kw:].reshape(DEC_BATCH, DEC_SEQ, kw)
    ck = cache_k.reshape(DEC_BATCH, WINDOW, kw)
    cv = cache_v.reshape(DEC_BATCH, WINDOW, kw)
    os_ = swa_sample(q, _pad_rows(k_new, 8), _pad_rows(v_new, 8), ck, cv, sink_row)[:, :DEC_SEQ].reshape(NS, hq)
    pp = jnp.stack([p[(b + 1) * SEQ - WINDOW:(b + 1) * SEQ] for b in range(BATCH)])
    kv = (pp[..., hq:hq + kw].reshape(BATCH, WINDOW, KVH_C, HD_C), pp[..., hq + kw:].reshape(BATCH, WINDOW, KVH_C, HD_C),
          jnp.concatenate([ck, k_new], axis=1)[:, DEC_SEQ:].reshape(DEC_BATCH, WINDOW, KVH_C, HD_C),
          jnp.concatenate([cv, v_new], axis=1)[:, DEC_SEQ:].reshape(DEC_BATCH, WINDOW, KVH_C, HD_C))
    return (op, os_), w_out.astype(BF16), kv


def _moe_layer(x, w_group, b_group, w_router, b_router, w_gate, w_up, w_down, g, b, layer):
    pad = ROUTER_ROWS - N_GROUPS - N_EXPERTS
    wt = jnp.concatenate([w_group.T, w_router.T, jnp.zeros((pad, D_MODEL), F32)], axis=0)
    bt = jnp.concatenate([b_group, b_router, jnp.zeros((pad,), F32)]).reshape(ROUTER_ROWS, 1)
    meta, wts, seg = moe_router(x, wt, bt)
    return moe_ffn_ln(x, meta, wts, seg, w_gate, w_up, w_down, g, b, layer)


def kernel(x_prompt, x_sample, state_mlstm_C, state_mlstm_n, state_mlstm_m, cache_moba_k, cache_moba_v, cache_swa_k, cache_swa_v, page_table, mlstm_w_in, mlstm_b_gate, mlstm_norm_g, mlstm_w_out, moba_w_in, moba_w_out, swa_w_in, swa_sinks, swa_w_out, ln_mix_g, ln_mix_b, ln_ffn_g, ln_ffn_b, moe_w_group, moe_b_group, moe_w_router, moe_b_router, moe_w_gate, moe_w_up, moe_w_down):
    x = jnp.concatenate([x_prompt.reshape(NP, D_MODEL), x_sample.reshape(NS, D_MODEL)], axis=0)
    n_pool = cache_moba_k.shape[1]
    moba_k = cache_moba_k.reshape(-1, PAGE_SIZE * KVH_B, HD_B)
    moba_v = cache_moba_v.reshape(-1, PAGE_SIZE * KVH_B, HD_B)
    wg_all, wu_all, wd_all = moe_w_gate.astype(BF16), moe_w_up.astype(BF16), moe_w_down.astype(BF16)
    m_states, b_kv, c_kv = [], [], []
    for layer in range(DEPTH):
        kind, slot = layer % 3, layer // 3
        if kind == 0:
            a, w_out, st = _mlstm_layer(x, mlstm_w_in[slot], mlstm_b_gate[slot], mlstm_norm_g[slot], mlstm_w_out[slot],
                                        state_mlstm_C, state_mlstm_n, state_mlstm_m, slot)
            m_states.append(st)
        elif kind == 1:
            a, w_out, kv = _moba_layer(x, moba_w_in[slot], moba_w_out[slot], moba_k, moba_v,
                                       page_table + slot * n_pool)
            b_kv.append(kv)
        else:
            a, w_out, kv = _swa_layer(x, swa_w_in[slot], swa_sinks[slot], swa_w_out[slot], cache_swa_k[slot],
                                      cache_swa_v[slot])
            c_kv.append(kv)
        x = outproj_ln(a[0], a[1], w_out, x, ln_mix_g[layer], ln_mix_b[layer])
        x = _moe_layer(x, moe_w_group[layer], moe_b_group[layer], moe_w_router[layer], moe_b_router[layer],
                       wg_all, wu_all, wd_all, ln_ffn_g[layer], ln_ffn_b[layer], layer)
    stack = lambda items, j: jnp.stack([it[j] for it in items])
    return (x[:NP].reshape(BATCH, SEQ, D_MODEL), x[NP:].reshape(DEC_BATCH, DEC_SEQ, D_MODEL),
            stack(m_states, 0), stack(m_states, 1), stack(m_states, 2),
            stack(m_states, 3), stack(m_states, 4), stack(m_states, 5),
            stack(b_kv, 0), stack(b_kv, 1), stack(b_kv, 2), stack(b_kv, 3),
            stack(c_kv, 0), stack(c_kv, 1), stack(c_kv, 2), stack(c_kv, 3))
```
